```python
import math
import jax, jax.numpy as jnp
from jax import lax
import numpy as np

D_MODEL = 1024
BATCH = 1
SEQ = 16384
DEPTH = 2
DEC_BATCH = 32
DEC_SEQ = 16
PAST_LEN = 2048

CHUNK = 64
QBLK = 128
EPS = 1e-5
NEG_BIG = -1e30

S5_WIDTH = D_MODEL // 4
S5_GROUP = 16
S5_GROUPS = S5_WIDTH // S5_GROUP
S5_STATE = 64
S5_DT_MIN = 1e-3
S5_DT_MAX = 1e-1

HG_HEADS = 4
HG_DK = 64
HG_DV = 64
HG_WIDTH = HG_HEADS * HG_DV

MLA_HEADS = 8
MLA_Q_RANK = 256
MLA_KV_RANK = 128
MLA_NOPE = 64
MLA_ROPE = 32
MLA_V = 64
MLA_WIDTH = MLA_HEADS * MLA_V
ROPE_THETA = 10000.0

MIX_WIDTH = S5_WIDTH + HG_WIDTH + MLA_WIDTH
D_FF = 4 * D_MODEL

IN_SPLITS = (S5_WIDTH, HG_HEADS * HG_DK, HG_HEADS * HG_DK, HG_WIDTH, HG_WIDTH, MLA_Q_RANK, MLA_KV_RANK, MLA_ROPE)
IN_WIDTH = S5_WIDTH + 2 * HG_HEADS * HG_DK + 2 * HG_WIDTH + MLA_Q_RANK + MLA_KV_RANK + MLA_ROPE

kernel_name = 'hybrid_s5_hgrn2_mla_stream_step'


def rms_norm(x, g):
    xf = x.astype(jnp.float32)
    y = xf * lax.rsqrt(jnp.mean(xf * xf, axis=-1, keepdims=True) + EPS)
    return (y * g.astype(jnp.float32)).astype(x.dtype)


def cmul(ar, ai, br, bi):
    return ar * br - ai * bi, ar * bi + ai * br


def split_columns(proj):
    outs, start = [], 0
    for width in IN_SPLITS:
        outs.append(proj[..., start:start + width])
        start += width
    return outs


def s5_mixer(u, h0_re, h0_im, lam_re, lam_im, log_dt, b_re, b_im, c_re, c_im, d, w_glu, b_glu):
    f32 = jnp.float32
    bsz, L, _ = u.shape
    uf = u.astype(f32).reshape(bsz, L, S5_GROUPS, S5_GROUP)
    lam_re = lam_re.astype(f32)
    lam_im = lam_im.astype(f32)
    dt = jnp.exp(log_dt.astype(f32))[:, None]
    mag = jnp.exp(lam_re * dt)
    ab_re, ab_im = mag * jnp.cos(lam_im * dt), mag * jnp.sin(lam_im * dt)
    den = lam_re * lam_re + lam_im * lam_im
    z_re, z_im = cmul(ab_re - 1.0, ab_im, lam_re / den, -lam_im / den)
    bb_re, bb_im = cmul(z_re[..., None], z_im[..., None], b_re.astype(f32), b_im.astype(f32))
    bu_re = jnp.einsum('blgh,gph->blgp', uf, bb_re)
    bu_im = jnp.einsum('blgh,gph->blgp', uf, bb_im)
    s_re, s_im = cmul(ab_re, ab_im, h0_re.astype(f32), h0_im.astype(f32))
    bu_re = bu_re.at[:, 0].add(s_re)
    bu_im = bu_im.at[:, 0].add(s_im)
    a_re = jnp.broadcast_to(ab_re, bu_re.shape)
    a_im = jnp.broadcast_to(ab_im, bu_im.shape)

    def combine(e1, e2):
        a1r, a1i, b1r, b1i = e1
        a2r, a2i, b2r, b2i = e2
        ar, ai = cmul(a2r, a2i, a1r, a1i)
        br, bi = cmul(a2r, a2i, b1r, b1i)
        return ar, ai, br + b2r, bi + b2i

    _, _, h_re, h_im = lax.associative_scan(combine, (a_re, a_im, bu_re, bu_im), axis=1)
    y = (jnp.einsum('blgp,ghp->blgh', h_re, c_re.astype(f32))
         - jnp.einsum('blgp,ghp->blgh', h_im, c_im.astype(f32)))
    y = y.reshape(bsz, L, S5_WIDTH) + d.astype(f32) * u.astype(f32)
    z = jax.nn.gelu(y)
    out = z * jax.nn.sigmoid(z @ w_glu.astype(f32) + b_glu.astype(f32))
    return out.astype(u.dtype), h_re[:, -1], h_im[:, -1]


def hgrn2_mixer(q, f_pre, i, s0, lb):
    f32 = jnp.float32
    bsz, L, _ = q.shape
    C = min(CHUNK, L)
    N = L // C
    lbf = lb.astype(f32)
    fp = f_pre.astype(f32)
    f = lbf + (1.0 - lbf) * jax.nn.sigmoid(fp)
    logf = jnp.log(f)
    k = (1.0 - lbf) * jax.nn.sigmoid(-fp)
    qf = jax.nn.silu(q.astype(f32))
    v = i.astype(f32)

    def blocks(t, dim):
        return t.reshape(bsz, N, C, HG_HEADS, dim).transpose(1, 0, 3, 2, 4)

    mask = jnp.tril(jnp.ones((C, C), dtype=bool))[:, :, None]

    def step(S, blk):
        qb, kb, vb, gb = blk
        b = jnp.cumsum(gb, axis=2)
        inter = jnp.einsum('bhtd,bhdv->bhtv', qb * jnp.exp(b), S)
        diff = b[:, :, :, None, :] - b[:, :, None, :, :]
        decay = jnp.where(mask, jnp.exp(jnp.where(mask, diff, 0.0)), 0.0)
        att = jnp.einsum('bhtd,bhsd,bhtsd->bhts', qb, kb, decay)
        intra = jnp.einsum('bhts,bhsv->bhtv', att, vb)
        b_last = b[:, :, -1:, :]
        S_new = (jnp.exp(b_last[:, :, 0, :])[..., None] * S
                 + jnp.einsum('bhsd,bhsv->bhdv', kb * jnp.exp(b_last - b), vb))
        return S_new, inter + intra

    S, o = lax.scan(step, s0.astype(f32),
                    (blocks(qf, HG_DK), blocks(k, HG_DK), blocks(v, HG_DV), blocks(logf, HG_DK)))
    o = o.transpose(1, 0, 3, 2, 4).reshape(bsz, L, HG_WIDTH)
    return o, S


def rope(x, pos):
    half = MLA_ROPE // 2
    inv = ROPE_THETA ** (-jnp.arange(half, dtype=jnp.float32) / half)
    ang = pos.astype(jnp.float32)[:, None] * inv[None, :]
    ang = ang.reshape((pos.shape[0],) + (1,) * (x.ndim - 3) + (half,))
    cos, sin = jnp.cos(ang), jnp.sin(ang)
    xf = x.astype(jnp.float32)
    x1, x2 = xf[..., :half], xf[..., half:]
    return jnp.concatenate([x1 * cos - x2 * sin, x2 * cos + x1 * sin], axis=-1).astype(x.dtype)


def mla_project(cq, ckv, kpe, pos, q_norm_g, w_uq, kv_norm_g, w_uk):
    bsz, L, _ = cq.shape
    q = (rms_norm(cq, q_norm_g) @ w_uq).reshape(bsz, L, MLA_HEADS, MLA_NOPE + MLA_ROPE)
    q_nope = q[..., :MLA_NOPE]
    q_pe = rope(q[..., MLA_NOPE:], pos)
    c_kv = rms_norm(ckv, kv_norm_g)
    k_pe = rope(kpe, pos)
    q_lat = jnp.einsum('blhd,chd->blhc', q_nope, w_uk)
    return q_lat, q_pe, c_kv, k_pe


def mla_attend(q_lat, q_pe, q_pos, c_kv, k_pe, k_pos, w_uv):
    bsz, Lq = q_lat.shape[:2]
    qb = min(QBLK, Lq)
    nb = Lq // qb
    scale = (MLA_NOPE + MLA_ROPE) ** -0.5
    k_chunk = k_pos // CHUNK

    def block(args):
        ql, qp, qpos = args
        s = (jnp.einsum('bqhc,bkc->bhqk', ql, c_kv)
             + jnp.einsum('bqhr,bkr->bhqk', qp, k_pe)).astype(jnp.float32) * scale
        mask = k_chunk[None, :] <= (qpos // CHUNK)[:, None]
        p = jax.nn.softmax(jnp.where(mask, s, NEG_BIG), axis=-1)
        return jnp.einsum('bhqk,bkc->bqhc', p.astype(c_kv.dtype), c_kv)

    def split(t):
        return t.reshape((bsz, nb, qb) + t.shape[2:]).swapaxes(0, 1)

    o_lat = lax.map(block, (split(q_lat), split(q_pe), q_pos.reshape(nb, qb)))
    o_lat = o_lat.swapaxes(0, 1).reshape(bsz, Lq, MLA_HEADS, MLA_KV_RANK)
    o = jnp.einsum('blhc,chv->blhv', o_lat, w_uv)
    return o.reshape(bsz, Lq, MLA_WIDTH)


def trunk_layer(x, q_pos, k_pos, s5_h0_re, s5_h0_im, hg_s0, kv_past, pe_past, lb, w):
    h = rms_norm(x, w['norm1_g'])
    u, hq, hf, hi, hg, cq, ckv, kpe = split_columns(h @ w['w_in'])
    s5_y, s5_re, s5_im = s5_mixer(u, s5_h0_re, s5_h0_im, w['s5_lambda_re'], w['s5_lambda_im'], w['s5_log_dt'],
                                  w['s5_b_re'], w['s5_b_im'], w['s5_c_re'], w['s5_c_im'], w['s5_d'],
                                  w['s5_w_glu'], w['s5_b_glu'])
    hg_o, hg_s = hgrn2_mixer(hq, hf, hi, hg_s0, lb)
    q_lat, q_pe, c_kv, k_pe = mla_project(cq, ckv, kpe, q_pos, w['mla_q_norm_g'], w['mla_w_uq'],
                                          w['mla_kv_norm_g'], w['mla_w_uk'])
    if kv_past is None:
        kv_all, pe_all = c_kv, k_pe
    else:
        kv_all = jnp.concatenate([kv_past.astype(c_kv.dtype), c_kv], axis=1)
        pe_all = jnp.concatenate([pe_past.astype(k_pe.dtype), k_pe], axis=1)
    mla_o = mla_attend(q_lat, q_pe, q_pos, kv_all, pe_all, k_pos, w['mla_w_uv'])
    g = w['out_norm_g']
    a0, a1 = S5_WIDTH, S5_WIDTH + HG_WIDTH
    mixed = jnp.concatenate([
        rms_norm(s5_y, g[:a0]),
        rms_norm(hg_o, g[a0:a1]) * jax.nn.silu(hg.astype(jnp.float32)),
        rms_norm(mla_o, g[a1:]),
    ], axis=-1).astype(x.dtype)
    x = x + mixed @ w['w_out']
    h2 = rms_norm(x, w['norm2_g'])
    x = x + jnp.square(jax.nn.relu(h2 @ w['w_up'])) @ w['w_down']
    return x, c_kv, k_pe, hg_s, s5_re, s5_im


def setup_inputs(seed: int = 0) -> dict:
    key = jax.random.key(seed)
    ks = iter(jax.random.split(key, 40))

    def nrm(shape, scale):
        return scale * jax.random.normal(next(ks), shape, jnp.float32)

    def gain(shape):
        return 1.0 + 0.01 * jax.random.normal(next(ks), shape, jnp.float32)

    L = DEPTH
    n_idx = jnp.arange(S5_STATE, dtype=jnp.float32)
    return {
        'x_prompt': nrm((BATCH, SEQ, D_MODEL), 1.0),
        'x_sample': nrm((DEC_BATCH, DEC_SEQ, D_MODEL), 1.0),
        'cache_mla_kv': nrm((L, DEC_BATCH, PAST_LEN, MLA_KV_RANK), 1.0),
        'cache_mla_pe': nrm((L, DEC_BATCH, PAST_LEN, MLA_ROPE), 1.0),
        'state_hgrn': nrm((L, DEC_BATCH, HG_HEADS, HG_DK, HG_DV), 0.5),
        'state_s5_re': nrm((L, DEC_BATCH, S5_GROUPS, S5_STATE), 0.05),
        'state_s5_im': nrm((L, DEC_BATCH, S5_GROUPS, S5_STATE), 0.05),
        'norm1_g': gain((L, D_MODEL)),
        'w_in': nrm((L, D_MODEL, IN_WIDTH), D_MODEL ** -0.5),
        's5_lambda_re': -0.5 + nrm((L, S5_GROUPS, S5_STATE), 0.01),
        's5_lambda_im': math.pi * n_idx + nrm((L, S5_GROUPS, S5_STATE), 0.01),
        's5_log_dt': jax.random.uniform(next(ks), (L, S5_GROUPS), jnp.float32,
                                        math.log(S5_DT_MIN), math.log(S5_DT_MAX)),
        's5_b_re': nrm((L, S5_GROUPS, S5_STATE, S5_GROUP), S5_GROUP ** -0.5),
        's5_b_im': nrm((L, S5_GROUPS, S5_STATE, S5_GROUP), S5_GROUP ** -0.5),
        's5_c_re': nrm((L, S5_GROUPS, S5_GROUP, S5_STATE), S5_STATE ** -0.5),
        's5_c_im': nrm((L, S5_GROUPS, S5_GROUP, S5_STATE), S5_STATE ** -0.5),
        's5_d': nrm((L, S5_WIDTH), 1.0),
        's5_w_glu': nrm((L, S5_WIDTH, S5_WIDTH), S5_WIDTH ** -0.5),
        's5_b_glu': nrm((L, S5_WIDTH), 0.01),
        'hgrn_lb_logits': nrm((L, HG_HEADS * HG_DK), 0.5),
        'mla_q_norm_g': gain((L, MLA_Q_RANK)),
        'mla_w_uq': nrm((L, MLA_Q_RANK, MLA_HEADS * (MLA_NOPE + MLA_ROPE)), MLA_Q_RANK ** -0.5),
        'mla_kv_norm_g': gain((L, MLA_KV_RANK)),
        'mla_w_uk': nrm((L, MLA_KV_RANK, MLA_HEADS, MLA_NOPE), MLA_KV_RANK ** -0.5),
        'mla_w_uv': nrm((L, MLA_KV_RANK, MLA_HEADS, MLA_V), MLA_KV_RANK ** -0.5),
        'out_norm_g': gain((L, MIX_WIDTH)),
        'w_out': nrm((L, MIX_WIDTH, D_MODEL), MIX_WIDTH ** -0.5),
        'norm2_g': gain((L, D_MODEL)),
        'w_up': nrm((L, D_MODEL, D_FF), D_MODEL ** -0.5),
        'w_down': nrm((L, D_FF, D_MODEL), D_FF ** -0.5),
        'final_norm_g': gain((D_MODEL,)),
    }


def reference(x_prompt, x_sample, cache_mla_kv, cache_mla_pe, state_hgrn, state_s5_re, state_s5_im,
              norm1_g, w_in, s5_lambda_re, s5_lambda_im, s5_log_dt, s5_b_re, s5_b_im, s5_c_re, s5_c_im,
              s5_d, s5_w_glu, s5_b_glu, hgrn_lb_logits, mla_q_norm_g, mla_w_uq, mla_kv_norm_g, mla_w_uk,
              mla_w_uv, out_norm_g, w_out, norm2_g, w_up, w_down, final_norm_g):
    bp, lp = x_prompt.shape[0], x_prompt.shape[1]
    ls = x_sample.shape[1]
    past = cache_mla_kv.shape[2]
    pos_p = jnp.arange(lp, dtype=jnp.int32)
    pos_s = past + jnp.arange(ls, dtype=jnp.int32)
    kpos_s = jnp.arange(past + ls, dtype=jnp.int32)
    lb_p = jax.nn.softmax(hgrn_lb_logits.astype(jnp.float32), axis=0)
    lb_all = jnp.cumsum(lb_p, axis=0) - lb_p[0]
    zero_s5 = jnp.zeros((bp, S5_GROUPS, S5_STATE), jnp.float32)
    zero_hg = jnp.zeros((bp, HG_HEADS, HG_DK, HG_DV), jnp.float32)

    xp, xs = x_prompt, x_sample
    p_kv, p_pe, p_hg, p_re, p_im = [], [], [], [], []
    s_kv, s_pe, s_hg, s_re, s_im = [], [], [], [], []
    for l in range(DEPTH):
        w = {
            'norm1_g': norm1_g[l], 'w_in': w_in[l],
            's5_lambda_re': s5_lambda_re[l], 's5_lambda_im': s5_lambda_im[l], 's5_log_dt': s5_log_dt[l],
            's5_b_re': s5_b_re[l], 's5_b_im': s5_b_im[l], 's5_c_re': s5_c_re[l], 's5_c_im': s5_c_im[l],
            's5_d': s5_d[l], 's5_w_glu': s5_w_glu[l], 's5_b_glu': s5_b_glu[l],
            'mla_q_norm_g': mla_q_norm_g[l], 'mla_w_uq': mla_w_uq[l], 'mla_kv_norm_g': mla_kv_norm_g[l],
            'mla_w_uk': mla_w_uk[l], 'mla_w_uv': mla_w_uv[l],
            'out_norm_g': out_norm_g[l], 'w_out': w_out[l], 'norm2_g': norm2_g[l],
            'w_up': w_up[l], 'w_down': w_down[l],
        }
        xp, a, b, c, d, e = trunk_layer(xp, pos_p, pos_p, zero_s5, zero_s5, zero_hg, None, None, lb_all[l], w)
        p_kv.append(a); p_pe.append(b); p_hg.append(c); p_re.append(d); p_im.append(e)
        xs, a, b, c, d, e = trunk_layer(xs, pos_s, kpos_s, state_s5_re[l], state_s5_im[l], state_hgrn[l],
                                        cache_mla_kv[l], cache_mla_pe[l], lb_all[l], w)
        s_kv.append(a); s_pe.append(b); s_hg.append(c); s_re.append(d); s_im.append(e)

    y_prompt = rms_norm(xp, final_norm_g)
    y_sample = rms_norm(xs, final_norm_g)
    return (y_prompt, y_sample,
            jnp.stack(p_kv), jnp.stack(p_pe), jnp.stack(p_hg), jnp.stack(p_re), jnp.stack(p_im),
            jnp.stack(s_kv), jnp.stack(s_pe), jnp.stack(s_hg), jnp.stack(s_re), jnp.stack(s_im))
```

```python
import functools
import math

import jax
import jax.numpy as jnp
from jax import lax
from jax.experimental import pallas as pl
from jax.experimental.pallas import tpu as pltpu

F32 = jnp.float32
BF16 = jnp.bfloat16

D_MODEL = 1024
CHUNK = 64
EPS = 1e-5
NEG_BIG = -1e30

S5_WIDTH = 256
S5_GROUP = 16
S5_GROUPS = 16
S5_STATE = 64
S5_T = 8
S5_ROW = S5_T * S5_WIDTH
S5_FLAT = 2 * S5_GROUPS * S5_STATE

HG_HEADS = 4
HG_DK = 64
HG_DV = 64
HG_WIDTH = 256
HG_ROWS = 16

MLA_HEADS = 8
MLA_Q_RANK = 256
MLA_KV_RANK = 128
MLA_NOPE = 64
MLA_ROPE = 32
MLA_V = 64
MLA_WIDTH = 512
ROPE_THETA = 10000.0
KEY_BLOCK = 256
QK_WIDTH = 256

D_FF = 4096
FF_CHUNK = 1024
IN_PAD = 1920

LANES = 128
VMEM_LIMIT = 56 * 1024 * 1024


def _cparams(*sem):
    return pltpu.CompilerParams(dimension_semantics=sem, vmem_limit_bytes=VMEM_LIMIT)


def _const_spec(shape):
    nd = len(shape)
    return pl.BlockSpec(shape, lambda *_: (0,) * nd, pipeline_mode=pl.Buffered(1))


def _rms(x, g):
    y = x * lax.rsqrt(jnp.mean(x * x, axis=-1, keepdims=True) + EPS)
    return y * g


def _dot(a, b):
    return jnp.dot(a, b, preferred_element_type=F32)


def _dot_t0(a, b):
    return lax.dot_general(a, b, (((0,), (0,)), ((), ())), preferred_element_type=F32)


def _dot_t1(a, b):
    return lax.dot_general(a, b, (((1,), (1,)), ((), ())), preferred_element_type=F32)


def _split3(x):
    hi = x.astype(BF16)
    r1 = x - hi.astype(F32)
    mid = r1.astype(BF16)
    lo = (r1 - mid.astype(F32)).astype(BF16)
    return hi, mid, lo


def _in_kernel(x_ref, g1_ref, win_ref, qg_ref, wuq_ref, wuk_ref, place_ref, kvg_ref, cos_ref, sin_ref,
               u_ref, hq_ref, hf_ref, hi_ref, hg_ref, ckv_ref, kpe_ref, qp_ref, kk_ref, vt_ref):
    h = _rms(x_ref[...], g1_ref[...])
    proj = _dot(h.astype(BF16), win_ref[...])
    u_ref[...] = proj[:, 0:256]
    hq_ref[...] = proj[:, 256:512]
    hf_ref[...] = proj[:, 512:768]
    hi_ref[...] = proj[:, 768:1024]
    hg_ref[...] = proj[:, 1024:1280]

    cos = cos_ref[...]
    sin = sin_ref[...]
    cos2 = jnp.concatenate([cos, cos], axis=-1)
    sin2 = jnp.concatenate([sin, sin], axis=-1)

    cqn = _rms(proj[:, 1280:1536], qg_ref[...])
    q = _dot(cqn.astype(BF16), wuq_ref[...])
    scale = (MLA_NOPE + MLA_ROPE) ** -0.5
    q_pe = (q[:, 512:768] * cos2 + q[:, 768:1024] * sin2) * scale
    q_lat = _dot(q[:, 0:512].astype(BF16), wuk_ref[...]) * scale
    pe_pl = _dot(q_pe.astype(BF16), place_ref[...])
    for hd in range(MLA_HEADS):
        qp_ref[:, hd * 256:hd * 256 + 128] = q_lat[:, hd * 128:(hd + 1) * 128].astype(BF16)
        qp_ref[:, hd * 256 + 128:(hd + 1) * 256] = pe_pl[:, hd * 128:(hd + 1) * 128].astype(BF16)

    c_kv = _rms(proj[:, 1536:1664], kvg_ref[...])
    ckv_ref[...] = c_kv
    kpe = proj[:, 1664:1792] * cos + proj[:, 1792:1920] * sin
    kpe_ref[...] = kpe[:, 0:MLA_ROPE]
    kk_ref[...] = jnp.concatenate([c_kv, kpe], axis=-1).astype(BF16)
    tm = c_kv.shape[0]
    for j in range(tm // KEY_BLOCK):
        vt_ref[j] = c_kv[j * KEY_BLOCK:(j + 1) * KEY_BLOCK, :].T.astype(BF16)


def _in_call(x, lw, cos, sin, tm):
    t = x.shape[0]
    row = lambda w: pl.BlockSpec((tm, w), lambda i: (i, 0))
    outs = [jax.ShapeDtypeStruct((t, 256), F32)] * 5 + [
        jax.ShapeDtypeStruct((t, MLA_KV_RANK), F32),
        jax.ShapeDtypeStruct((t, MLA_ROPE), F32),
        jax.ShapeDtypeStruct((t, MLA_HEADS * QK_WIDTH), BF16),
        jax.ShapeDtypeStruct((t, QK_WIDTH), BF16),
        jax.ShapeDtypeStruct((t // KEY_BLOCK, MLA_KV_RANK, KEY_BLOCK), BF16),
    ]
    out_specs = [row(256)] * 5 + [row(MLA_KV_RANK), row(MLA_ROPE), row(MLA_HEADS * QK_WIDTH), row(QK_WIDTH),
                                  pl.BlockSpec((tm // KEY_BLOCK, MLA_KV_RANK, KEY_BLOCK), lambda i: (i, 0, 0))]
    return pl.pallas_call(
        _in_kernel,
        grid=(t // tm,),
        in_specs=[row(D_MODEL), _const_spec((1, D_MODEL)), _const_spec((D_MODEL, IN_PAD)),
                  _const_spec((1, MLA_Q_RANK)), _const_spec((MLA_Q_RANK, 1024)), _const_spec((512, 1024)),
                  _const_spec((256, 1024)), _const_spec((1, MLA_KV_RANK)), row(LANES), row(LANES)],
        out_specs=out_specs,
        out_shape=outs,
        compiler_params=_cparams("arbitrary"),
    )(x, lw["norm1_g"], lw["w_in"], lw["q_norm_g"], lw["w_uq"], lw["w_uk"], lw["place"], lw["kv_norm_g"],
      cos, sin)


def _s5_prompt_kernel(u_ref, m1_ref, m2_ref, m3_ref, ab_ref, y_ref, hfin_ref, s_scr, hp_scr, h_scr):
    half = S5_FLAT // 2
    tn = u_ref.shape[0]

    @pl.when(pl.program_id(0) == 0)
    def _():
        h_scr[...] = jnp.zeros_like(h_scr)

    u = u_ref[...].astype(BF16)
    s_scr[...] = _dot(u, m2_ref[...])
    ar = ab_ref[:, 0:half]
    ai = ab_ref[:, half:S5_FLAT]

    def body(i, carry):
        hr, hi = carry
        hp_scr[pl.ds(i, 1), 0:half] = hr
        hp_scr[pl.ds(i, 1), half:S5_FLAT] = hi
        sr = s_scr[pl.ds(i, 1), 0:half]
        si = s_scr[pl.ds(i, 1), half:S5_FLAT]
        return ar * hr - ai * hi + sr, ar * hi + ai * hr + si

    hr, hi = lax.fori_loop(0, tn, body, (h_scr[:, 0:half], h_scr[:, half:S5_FLAT]), unroll=8)
    h_scr[:, 0:half] = hr
    h_scr[:, half:S5_FLAT] = hi
    y_ref[...] = _dot(u, m1_ref[...]) + _dot(hp_scr[...].astype(BF16), m3_ref[...])
    hfin_ref[...] = h_scr[...]


def _s5_prompt_call(u_rows, sw, tn):
    n = u_rows.shape[0]
    return pl.pallas_call(
        _s5_prompt_kernel,
        grid=(n // tn,),
        in_specs=[pl.BlockSpec((tn, S5_ROW), lambda i: (i, 0)),
                  _const_spec((S5_ROW, S5_ROW)), _const_spec((S5_ROW, S5_FLAT)), _const_spec((S5_FLAT, S5_ROW)),
                  _const_spec((1, S5_FLAT))],
        out_specs=[pl.BlockSpec((tn, S5_ROW), lambda i: (i, 0)), pl.BlockSpec((1, S5_FLAT), lambda i: (0, 0))],
        out_shape=[jax.ShapeDtypeStruct((n, S5_ROW), F32), jax.ShapeDtypeStruct((1, S5_FLAT), F32)],
        scratch_shapes=[pltpu.VMEM((tn, S5_FLAT), F32), pltpu.VMEM((tn, S5_FLAT), F32),
                        pltpu.VMEM((1, S5_FLAT), F32)],
        compiler_params=_cparams("arbitrary"),
    )(u_rows, sw["m1"], sw["m2"], sw["m3"], sw["ab"])


def _s5_sample_kernel(u_ref, h0_ref, m1_ref, m2_ref, m3_ref, ab_ref, y_ref, hfin_ref, *, n_rows):
    half = S5_FLAT // 2
    ar = ab_ref[:, 0:half]
    ai = ab_ref[:, half:S5_FLAT]
    us = [u_ref[:, c * S5_ROW:(c + 1) * S5_ROW].astype(BF16) for c in range(n_rows)]
    u_all = jnp.concatenate(us, axis=0)
    b = us[0].shape[0]
    s_all = _dot(u_all, m2_ref[...])
    y1_all = _dot(u_all, m1_ref[...])
    hr = h0_ref[:, 0:half]
    hi = h0_ref[:, half:S5_FLAT]
    enter = []
    for c in range(n_rows):
        enter.append(jnp.concatenate([hr, hi], axis=-1))
        sr = s_all[c * b:(c + 1) * b, 0:half]
        si = s_all[c * b:(c + 1) * b, half:S5_FLAT]
        hr, hi = ar * hr - ai * hi + sr, ar * hi + ai * hr + si
    y_all = y1_all + _dot(jnp.concatenate(enter, axis=0).astype(BF16), m3_ref[...])
    for c in range(n_rows):
        y_ref[:, c * S5_ROW:(c + 1) * S5_ROW] = y_all[c * b:(c + 1) * b, :]
    hfin_ref[:, 0:half] = hr
    hfin_ref[:, half:S5_FLAT] = hi


def _s5_sample_call(u_seq, h0, sw):
    b, w = u_seq.shape
    n_rows = w // S5_ROW
    full = lambda shape: pl.BlockSpec(shape, lambda i: (0,) * len(shape))
    return pl.pallas_call(
        functools.partial(_s5_sample_kernel, n_rows=n_rows),
        grid=(1,),
        in_specs=[full((b, w)), full((b, S5_FLAT)), _const_spec((S5_ROW, S5_ROW)),
                  _const_spec((S5_ROW, S5_FLAT)), _const_spec((S5_FLAT, S5_ROW)), _const_spec((1, S5_FLAT))],
        out_specs=[full((b, w)), full((b, S5_FLAT))],
        out_shape=[jax.ShapeDtypeStruct((b, w), F32), jax.ShapeDtypeStruct((b, S5_FLAT), F32)],
        compiler_params=_cparams("arbitrary"),
    )(u_seq, h0, sw["m1"], sw["m2"], sw["m3"], sw["ab"])


def _hg_gates(hq, hf, lb):
    sig = jax.nn.sigmoid(hf)
    f = lb + (1.0 - lb) * sig
    k = (1.0 - lb) * jax.nn.sigmoid(-hf)
    qf = hq * jax.nn.sigmoid(hq)
    return qf, k, f


def _hg_intra_kernel(hq_ref, hf_ref, hi_ref, lb_ref, ones_ref, o_ref, k_scr, f_scr, *, r_len):
    lb = lb_ref[...]
    ones_bd = ones_ref[...]
    w = HG_WIDTH
    for r in range(r_len):
        _, k, f = _hg_gates(hq_ref[:, r * w:(r + 1) * w], hf_ref[:, r * w:(r + 1) * w], lb)
        k_scr[r] = k
        f_scr[r] = f
    for r in range(r_len):
        hq = hq_ref[:, r * w:(r + 1) * w]
        qp = hq * jax.nn.sigmoid(hq)
        terms = []
        for s in range(r, -1, -1):
            terms.append((qp * k_scr[s]).astype(BF16))
            if s > 0:
                qp = qp * f_scr[s]
        att = _dot(jnp.concatenate(terms, axis=0), ones_bd)
        rows = hq.shape[0]
        acc = None
        for j, s in enumerate(range(r, -1, -1)):
            part = att[j * rows:(j + 1) * rows, :] * hi_ref[:, s * w:(s + 1) * w]
            acc = part if acc is None else acc + part
        o_ref[:, r * w:(r + 1) * w] = acc


def _hg_intra_call(hq, hf, hi, lb, ones_bd, r_len):
    rows = hq.shape[0]
    spec = pl.BlockSpec((HG_ROWS, r_len * HG_WIDTH), lambda i: (i, 0))
    return pl.pallas_call(
        functools.partial(_hg_intra_kernel, r_len=r_len),
        grid=(rows // HG_ROWS,),
        in_specs=[spec, spec, spec, _const_spec((1, HG_WIDTH)), _const_spec((HG_WIDTH, HG_WIDTH))],
        out_specs=spec,
        out_shape=jax.ShapeDtypeStruct(hq.shape, F32),
        scratch_shapes=[pltpu.VMEM((r_len, HG_ROWS, HG_WIDTH), F32), pltpu.VMEM((r_len, HG_ROWS, HG_WIDTH), F32)],
        compiler_params=_cparams("arbitrary"),
    )(hq, hf, hi, lb, ones_bd)


def _hg_inter_kernel(hq_ref, hf_ref, hi_ref, lb_ref, tril_ref, blk_ref, hmask_ref, s0_ref,
                     o_ref, sout_ref, s_scr, *, r_len, carry):
    n_sub = HG_ROWS
    if carry:
        @pl.when(pl.program_id(0) == 0)
        def _():
            s_scr[...] = s0_ref[0]

    qf, k, f = _hg_gates(hq_ref[...], hf_ref[...], lb_ref[...])
    g = jnp.log(f)
    g3 = _split3(g)
    tril = tril_ref[...]
    blk = blk_ref[...]
    bl = _dot(tril, g3[0]) + _dot(tril, g3[1]) + _dot(tril, g3[2])
    bsum = _dot(blk, g3[0]) + _dot(blk, g3[1]) + _dot(blk, g3[2])
    q_in = (qf * jnp.exp(bl)).astype(BF16)
    k_out = (k * jnp.exp(bsum - bl)).astype(BF16)
    v = hi_ref[...].astype(BF16)
    hmask = hmask_ref[...]
    ones = jnp.ones((r_len, HG_WIDTH), BF16)
    for j in range(n_sub):
        sl = slice(j * r_len, (j + 1) * r_len)
        s_in = s_scr[...] if carry else s0_ref[j]
        o_ref[sl, :] = _dot(q_in[sl], s_in.astype(BF16))
        w_new = _dot_t0(k_out[sl], v[sl]) * hmask
        gsum = _dot_t0(g3[0][sl], ones) + _dot_t0(g3[1][sl], ones) + _dot_t0(g3[2][sl], ones)
        s_new = jnp.exp(gsum) * s_in + w_new
        if carry:
            s_scr[...] = s_new
        else:
            sout_ref[j] = s_new
    if carry:
        sout_ref[0] = s_scr[...]


def _hg_inter_call(hq, hf, hi, lb, consts, s0, r_len, carry):
    t = hq.shape[0]
    tt = HG_ROWS * r_len
    spec = pl.BlockSpec((tt, HG_WIDTH), lambda i: (i, 0))
    if carry:
        s_spec = pl.BlockSpec((1, HG_WIDTH, HG_WIDTH), lambda i: (0, 0, 0))
    else:
        s_spec = pl.BlockSpec((HG_ROWS, HG_WIDTH, HG_WIDTH), lambda i: (i, 0, 0))
    return pl.pallas_call(
        functools.partial(_hg_inter_kernel, r_len=r_len, carry=carry),
        grid=(t // tt,),
        in_specs=[spec, spec, spec, _const_spec((1, HG_WIDTH)), _const_spec((tt, tt)), _const_spec((tt, tt)),
                  _const_spec((HG_WIDTH, HG_WIDTH)), s_spec],
        out_specs=[spec, s_spec],
        out_shape=[jax.ShapeDtypeStruct((t, HG_WIDTH), F32), jax.ShapeDtypeStruct(s0.shape, F32)],
        scratch_shapes=[pltpu.VMEM((HG_WIDTH, HG_WIDTH), F32)],
        compiler_params=_cparams("arbitrary"),
    )(hq, hf, hi, lb, consts["tril"], consts["blk"], consts["hmask"], s0)


def _attn_kernel(qp_ref, kctx_ref, vtctx_ref, kd_ref, vtd_ref, wuv_ref, o_ref,
                 q_scr, m_scr, l_scr, acc_scr, *, tq, prompt, n_ctx, n_valid):
    i = pl.program_id(0)
    ncol = MLA_HEADS * tq
    cw = min(256, ncol)
    for hd in range(MLA_HEADS):
        q_scr[hd * tq:(hd + 1) * tq, :] = qp_ref[:, hd * QK_WIDTH:(hd + 1) * QK_WIDTH]
    m_scr[...] = jnp.full_like(m_scr, NEG_BIG)
    l_scr[...] = jnp.zeros_like(l_scr)
    acc_scr[...] = jnp.zeros_like(acc_scr)

    def block(kblk, vtblk, mask):
        for c0 in range(0, ncol, cw):
            s = _dot_t1(kblk, q_scr[c0:c0 + cw, :])
            if mask is not None:
                s = jnp.where(mask[:, c0:c0 + cw], s, NEG_BIG)
            m_old = m_scr[:, c0:c0 + cw]
            m_new = jnp.maximum(m_old, jnp.max(s, axis=0, keepdims=True))
            alpha = jnp.exp(m_old - m_new)
            p = jnp.exp(s - m_new)
            l_scr[:, c0:c0 + cw] = alpha * l_scr[:, c0:c0 + cw] + jnp.sum(p, axis=0, keepdims=True)
            acc_scr[:, c0:c0 + cw] = alpha * acc_scr[:, c0:c0 + cw] + _dot(vtblk, p.astype(BF16))
            m_scr[:, c0:c0 + cw] = m_new

    if prompt:
        per = KEY_BLOCK // tq
        n_full = i // per
    else:
        n_full = n_ctx

    def ctx_body(kb, carry):
        block(kctx_ref[0, kb], vtctx_ref[0, kb], None)
        return carry

    lax.fori_loop(0, n_full, ctx_body, 0)

    krow = lax.broadcasted_iota(jnp.int32, (KEY_BLOCK, ncol), 0)
    if prompt:
        col = lax.broadcasted_iota(jnp.int32, (KEY_BLOCK, ncol), 1)
        qpos = i * tq + (col & (tq - 1))
        kpos = n_full * KEY_BLOCK + krow
        mask = (kpos >> 6) <= (qpos >> 6)
    else:
        mask = krow < n_valid
    block(kd_ref[0], vtd_ref[0], mask)

    o_lat_t = acc_scr[...] / l_scr[...]
    o_lat = o_lat_t.T.astype(BF16)
    full = _dot(o_lat, wuv_ref[...])
    lane_head = lax.broadcasted_iota(jnp.int32, (tq, MLA_WIDTH), 1) // MLA_V
    out = jnp.zeros((tq, MLA_WIDTH), F32)
    for hd in range(MLA_HEADS):
        out = out + jnp.where(lane_head == hd, full[hd * tq:(hd + 1) * tq, :], 0.0)
    o_ref[...] = out


def _attn_call(qp, kctx, vtctx, kd, vtd, wuv, tq, prompt, n_valid):
    t = qp.shape[0]
    n_ctx = kctx.shape[1]
    ncol = MLA_HEADS * tq
    if prompt:
        per = KEY_BLOCK // tq
        ctx_k = pl.BlockSpec((1, n_ctx, KEY_BLOCK, QK_WIDTH), lambda i: (0, 0, 0, 0), pipeline_mode=pl.Buffered(1))
        ctx_v = pl.BlockSpec((1, n_ctx, MLA_KV_RANK, KEY_BLOCK), lambda i: (0, 0, 0, 0),
                             pipeline_mode=pl.Buffered(1))
        d_k = pl.BlockSpec((1, KEY_BLOCK, QK_WIDTH), lambda i: (i // per, 0, 0))
        d_v = pl.BlockSpec((1, MLA_KV_RANK, KEY_BLOCK), lambda i: (i // per, 0, 0))
    else:
        ctx_k = pl.BlockSpec((1, n_ctx, KEY_BLOCK, QK_WIDTH), lambda i: (i, 0, 0, 0))
        ctx_v = pl.BlockSpec((1, n_ctx, MLA_KV_RANK, KEY_BLOCK), lambda i: (i, 0, 0, 0))
        d_k = pl.BlockSpec((1, KEY_BLOCK, QK_WIDTH), lambda i: (i, 0, 0))
        d_v = pl.BlockSpec((1, MLA_KV_RANK, KEY_BLOCK), lambda i: (i, 0, 0))
    return pl.pallas_call(
        functools.partial(_attn_kernel, tq=tq, prompt=prompt, n_ctx=n_ctx, n_valid=n_valid),
        grid=(t // tq,),
        in_specs=[pl.BlockSpec((tq, MLA_HEADS * QK_WIDTH), lambda i: (i, 0)), ctx_k, ctx_v, d_k, d_v,
                  _const_spec((MLA_KV_RANK, MLA_WIDTH))],
        out_specs=pl.BlockSpec((tq, MLA_WIDTH), lambda i: (i, 0)),
        out_shape=jax.ShapeDtypeStruct((t, MLA_WIDTH), F32),
        scratch_shapes=[pltpu.VMEM((ncol, QK_WIDTH), BF16), pltpu.VMEM((1, ncol), F32), pltpu.VMEM((1, ncol), F32),
                        pltpu.VMEM((MLA_KV_RANK, ncol), F32)],
        compiler_params=_cparams("arbitrary"),
    )(qp, kctx, vtctx, kd, vtd, wuv)


def _out_kernel(x_ref, ys_ref, u_ref, oa_ref, ob_ref, hg_ref, mla_ref, d_ref, wglu_ref, bglu_ref, og_ref,
                wout_ref, g2_ref, wup_ref, wdn_ref, fg_ref, o_ref, *, final):
    y = ys_ref[...] + d_ref[...] * u_ref[...]
    z = jax.nn.gelu(y, approximate=True)
    s5 = z * jax.nn.sigmoid(_dot(z.astype(BF16), wglu_ref[...]) + bglu_ref[...])
    og = og_ref[...]
    hgate = hg_ref[...]
    mixed = jnp.concatenate([
        _rms(s5, og[:, 0:256]),
        _rms(oa_ref[...] + ob_ref[...], og[:, 256:512]) * (hgate * jax.nn.sigmoid(hgate)),
        _rms(mla_ref[...], og[:, 512:1024]),
    ], axis=-1)
    x1 = x_ref[...] + _dot(mixed.astype(BF16), wout_ref[...])
    h2 = _rms(x1, g2_ref[...]).astype(BF16)
    acc = x1
    for c in range(D_FF // FF_CHUNK):
        up = _dot(h2, wup_ref[:, c * FF_CHUNK:(c + 1) * FF_CHUNK])
        act = jnp.square(jnp.maximum(up, 0.0)).astype(BF16)
        acc = acc + _dot(act, wdn_ref[c * FF_CHUNK:(c + 1) * FF_CHUNK, :])
    if final:
        acc = _rms(acc, fg_ref[...])
    o_ref[...] = acc


def _out_call(x, ys, u, oa, ob, hg, mla, lw, final_g, tm, final):
    t = x.shape[0]
    row = lambda w: pl.BlockSpec((tm, w), lambda i: (i, 0))
    return pl.pallas_call(
        functools.partial(_out_kernel, final=final),
        grid=(t // tm,),
        in_specs=[row(D_MODEL), row(256), row(256), row(256), row(256), row(256), row(MLA_WIDTH),
                  _const_spec((1, 256)), _const_spec((256, 256)), _const_spec((1, 256)), _const_spec((1, D_MODEL)),
                  _const_spec((D_MODEL, D_MODEL)), _const_spec((1, D_MODEL)), _const_spec((D_MODEL, D_FF)),
                  _const_spec((D_FF, D_MODEL)), _const_spec((1, D_MODEL))],
        out_specs=row(D_MODEL),
        out_shape=jax.ShapeDtypeStruct((t, D_MODEL), F32),
        compiler_params=_cparams("arbitrary"),
    )(x, ys, u, oa, ob, hg, mla, lw["s5_d"], lw["s5_w_glu"], lw["s5_b_glu"], lw["out_norm_g"], lw["w_out"],
      lw["norm2_g"], lw["w_up"], lw["w_down"], final_g)


def _rot_cols(w):
    half = MLA_ROPE // 2
    return jnp.concatenate([-w[..., half:], w[..., :half]], axis=-1)


def _prep_layer(w_in, w_uq, w_uk, w_uv):
    kpe = w_in[:, 1664:1696]
    pad = jnp.zeros((D_MODEL, 96), F32)
    w_in_p = jnp.concatenate([w_in[:, :1664], kpe, pad, _rot_cols(kpe), pad], axis=1).astype(BF16)
    uq = w_uq.reshape(MLA_Q_RANK, MLA_HEADS, MLA_NOPE + MLA_ROPE)
    nope = uq[:, :, :MLA_NOPE].reshape(MLA_Q_RANK, 512)
    pe = uq[:, :, MLA_NOPE:]
    w_uq_p = jnp.concatenate([nope, pe.reshape(MLA_Q_RANK, 256), _rot_cols(pe).reshape(MLA_Q_RANK, 256)],
                             axis=1).astype(BF16)
    eye_h = jnp.eye(MLA_HEADS, dtype=F32)
    wuk_bd = jnp.einsum("chd,hk->hdkc", w_uk, eye_h).reshape(512, 1024).astype(BF16)
    place = jnp.einsum("hk,rc->hrkc", eye_h, jnp.eye(MLA_ROPE, LANES, dtype=F32)).reshape(256, 1024).astype(BF16)
    return w_in_p, w_uq_p, wuk_bd, place, w_uv.reshape(MLA_KV_RANK, MLA_WIDTH).astype(BF16)


def _prep_s5(lam_re, lam_im, log_dt, b_re, b_im, c_re, c_im):
    hp = lax.Precision.HIGHEST
    t = S5_T
    dt = jnp.exp(log_dt)[:, None]
    mag1 = jnp.exp(lam_re * dt)
    a_re, a_im = mag1 * jnp.cos(lam_im * dt), mag1 * jnp.sin(lam_im * dt)
    pw_re, pw_im = [jnp.ones_like(a_re)], [jnp.zeros_like(a_im)]
    for _ in range(t):
        pr, pi = pw_re[-1], pw_im[-1]
        pw_re.append(pr * a_re - pi * a_im)
        pw_im.append(pr * a_im + pi * a_re)
    p_re, p_im = jnp.stack(pw_re), jnp.stack(pw_im)
    den = lam_re * lam_re + lam_im * lam_im
    i_re, i_im = lam_re / den, -lam_im / den
    z_re = (a_re - 1.0) * i_re - a_im * i_im
    z_im = (a_re - 1.0) * i_im + a_im * i_re
    bb_re = z_re[..., None] * b_re - z_im[..., None] * b_im
    bb_im = z_re[..., None] * b_im + z_im[..., None] * b_re
    cp_re = c_re[None] * p_re[:, :, None, :] - c_im[None] * p_im[:, :, None, :]
    cp_im = c_re[None] * p_im[:, :, None, :] + c_im[None] * p_re[:, :, None, :]
    kern = (jnp.einsum("tgap,gph->tgah", cp_re[:t], bb_re, precision=hp)
            - jnp.einsum("tgap,gph->tgah", cp_im[:t], bb_im, precision=hp))
    eye_g = jnp.eye(S5_GROUPS, dtype=F32)
    s_idx = jnp.arange(t)[:, None]
    t_idx = jnp.arange(t)[None, :]
    lag = jnp.clip(t_idx - s_idx, 0, t - 1)
    kts = jnp.where((t_idx >= s_idx)[:, :, None, None, None], kern[lag], 0.0)
    m1 = jnp.einsum("stgab,gk->sgbtka", kts, eye_g).reshape(S5_ROW, S5_ROW)
    rev_re, rev_im = p_re[t - 1::-1][:t], p_im[t - 1::-1][:t]
    w2_re = rev_re[..., None] * bb_re[None] - rev_im[..., None] * bb_im[None]
    w2_im = rev_re[..., None] * bb_im[None] + rev_im[..., None] * bb_re[None]
    m2 = jnp.einsum("rsgph,gk->sghrkp", jnp.stack([w2_re, w2_im]), eye_g).reshape(S5_ROW, S5_FLAT)
    m3 = jnp.einsum("rtgap,gk->rgptka", jnp.stack([cp_re[1:], -cp_im[1:]]), eye_g).reshape(S5_FLAT, S5_ROW)
    ab = jnp.concatenate([p_re[t].reshape(1, -1), p_im[t].reshape(1, -1)], axis=1)
    return {"m1": m1.astype(BF16), "m2": m2.astype(BF16), "m3": m3.astype(BF16), "ab": ab}


def _hg_consts(r_len):
    tt = HG_ROWS * r_len
    r = jnp.arange(tt)
    same = (r[:, None] // r_len) == (r[None, :] // r_len)
    tril = (same & (r[None, :] <= r[:, None])).astype(BF16)
    hd = jnp.arange(HG_WIDTH) // HG_DK
    hmask = (hd[:, None] == hd[None, :]).astype(F32)
    return {"tril": tril, "blk": same.astype(BF16), "hmask": hmask}


def _rope_tables(pos):
    half = MLA_ROPE // 2
    inv = ROPE_THETA ** (-jnp.arange(half, dtype=F32) / half)
    ang = pos.astype(F32)[:, None] * inv[None, :]
    reps = LANES // half
    return jnp.tile(jnp.cos(ang), (1, reps)), jnp.tile(jnp.sin(ang), (1, reps))


def _state_to_bd(s):
    eye_h = jnp.eye(HG_HEADS, dtype=F32)
    return jnp.einsum("bhdv,hk->bhdkv", s, eye_h).reshape(s.shape[0], HG_WIDTH, HG_WIDTH)


def _state_from_bd(s):
    b = s.shape[0]
    s5 = s.reshape(b, HG_HEADS, HG_DK, HG_HEADS, HG_DV)
    return jnp.stack([s5[:, h, :, h, :] for h in range(HG_HEADS)], axis=1)


def _layer(x, lw, sw, cos, sin, lb, hgc, ones_bd, final_g, final, *, prompt, n_seq, s5_h0, hg_s0, kv_past, pe_past):
    t = x.shape[0]
    seq = t // n_seq
    tm = 512
    u, hq, hf, hi, hg, c_kv, k_pe, qp, kk, vt = _in_call(x, lw, cos, sin, tm)

    if prompt:
        ys, s5_fin = _s5_prompt_call(u.reshape(t // S5_T, S5_ROW), sw, min(256, t // S5_T))
    else:
        ys, s5_fin = _s5_sample_call(u.reshape(n_seq, seq * S5_WIDTH), s5_h0, sw)
    ys = ys.reshape(t, S5_WIDTH)

    r_len = 32 if prompt else seq
    av = lambda a: a.reshape(t // r_len, r_len * HG_WIDTH)
    oa = _hg_intra_call(av(hq), av(hf), av(hi), lb, ones_bd, r_len).reshape(t, HG_WIDTH)
    ob, hg_fin = _hg_inter_call(hq, hf, hi, lb, hgc, hg_s0, r_len, carry=prompt)

    if prompt:
        kctx = kk.reshape(1, t // KEY_BLOCK, KEY_BLOCK, QK_WIDTH)
        vtctx = vt.reshape(1, t // KEY_BLOCK, MLA_KV_RANK, KEY_BLOCK)
        mla = _attn_call(qp, kctx, vtctx, kctx[0], vtctx[0], lw["w_uv"], 128, True, KEY_BLOCK)
    else:
        past = kv_past.shape[1]
        kpast = jnp.concatenate([kv_past, pe_past, jnp.zeros((n_seq, past, QK_WIDTH - 160), F32)],
                                axis=-1).astype(BF16)
        kctx = kpast.reshape(n_seq, past // KEY_BLOCK, KEY_BLOCK, QK_WIDTH)
        vtctx = kv_past.astype(BF16).reshape(n_seq, past // KEY_BLOCK, KEY_BLOCK, MLA_KV_RANK).swapaxes(2, 3)
        kd = jnp.pad(kk.reshape(n_seq, seq, QK_WIDTH), ((0, 0), (0, KEY_BLOCK - seq), (0, 0)))
        vtd = jnp.pad(c_kv.astype(BF16).reshape(n_seq, seq, MLA_KV_RANK).swapaxes(1, 2),
                      ((0, 0), (0, 0), (0, KEY_BLOCK - seq)))
        mla = _attn_call(qp, kctx, vtctx, kd, vtd, lw["w_uv"], seq, False, seq)

    x_new = _out_call(x, ys, u, oa, ob, hg, mla, lw, final_g, tm, final)
    return x_new, c_kv, k_pe, hg_fin, s5_fin


def kernel(x_prompt, x_sample, cache_mla_kv, cache_mla_pe, state_hgrn, state_s5_re, state_s5_im, norm1_g, w_in, s5_lambda_re, s5_lambda_im, s5_log_dt, s5_b_re, s5_b_im, s5_c_re, s5_c_im, s5_d, s5_w_glu, s5_b_glu, hgrn_lb_logits, mla_q_norm_g, mla_w_uq, mla_kv_norm_g, mla_w_uk, mla_w_uv, out_norm_g, w_out, norm2_g, w_up, w_down, final_norm_g):
    depth = w_in.shape[0]
    bp, lp = x_prompt.shape[0], x_prompt.shape[1]
    bs, ls = x_sample.shape[0], x_sample.shape[1]
    past = cache_mla_kv.shape[2]
    assert bp == 1 and ls == 2 * S5_T and bs % HG_ROWS == 0

    cos_p, sin_p = _rope_tables(jnp.arange(lp, dtype=jnp.int32))
    cos_s, sin_s = _rope_tables(past + jnp.arange(ls, dtype=jnp.int32))
    cos_s, sin_s = jnp.tile(cos_s, (bs, 1)), jnp.tile(sin_s, (bs, 1))

    lb_p = jax.nn.softmax(hgrn_lb_logits.astype(F32), axis=0)
    lb_all = jnp.cumsum(lb_p, axis=0) - lb_p[0]
    hgc_p, hgc_s = _hg_consts(32), _hg_consts(ls)
    hd = jnp.arange(HG_WIDTH) // HG_DK
    ones_bd = (hd[:, None] == hd[None, :]).astype(BF16)
    row = lambda v: v.reshape(1, -1).astype(F32)
    final_g = row(final_norm_g)

    xp = x_prompt.reshape(bp * lp, D_MODEL)
    xs = x_sample.reshape(bs * ls, D_MODEL)
    outs_p, outs_s = [], []
    for l in range(depth):
        w_in_p, w_uq_p, wuk_bd, place, w_uv_p = _prep_layer(w_in[l], mla_w_uq[l], mla_w_uk[l], mla_w_uv[l])
        lw = {
            "norm1_g": row(norm1_g[l]), "w_in": w_in_p, "q_norm_g": row(mla_q_norm_g[l]), "w_uq": w_uq_p,
            "w_uk": wuk_bd, "place": place, "kv_norm_g": row(mla_kv_norm_g[l]), "w_uv": w_uv_p,
            "s5_d": row(s5_d[l]), "s5_w_glu": s5_w_glu[l].astype(BF16), "s5_b_glu": row(s5_b_glu[l]),
            "out_norm_g": row(out_norm_g[l]), "w_out": w_out[l].astype(BF16), "norm2_g": row(norm2_g[l]),
            "w_up": w_up[l].astype(BF16), "w_down": w_down[l].astype(BF16),
        }
        sw = _prep_s5(s5_lambda_re[l], s5_lambda_im[l], s5_log_dt[l], s5_b_re[l], s5_b_im[l],
                      s5_c_re[l], s5_c_im[l])
        lb = row(lb_all[l])
        final = l == depth - 1
        xp, a, b, c, d = _layer(xp, lw, sw, cos_p, sin_p, lb, hgc_p, ones_bd, final_g, final, prompt=True,
                                n_seq=1, s5_h0=None, hg_s0=jnp.zeros((1, HG_WIDTH, HG_WIDTH), F32),
                                kv_past=None, pe_past=None)
        outs_p.append((a, b, c, d))
        h0 = jnp.concatenate([state_s5_re[l].reshape(bs, -1), state_s5_im[l].reshape(bs, -1)], axis=1)
        xs, a, b, c, d = _layer(xs, lw, sw, cos_s, sin_s, lb, hgc_s, ones_bd, final_g, final, prompt=False,
                                n_seq=bs, s5_h0=h0, hg_s0=_state_to_bd(state_hgrn[l]),
                                kv_past=cache_mla_kv[l], pe_past=cache_mla_pe[l])
        outs_s.append((a, b, c, d))

    def gather(outs, nb, sl):
        kv = jnp.stack([o[0].reshape(nb, sl, MLA_KV_RANK) for o in outs])
        pe = jnp.stack([o[1].reshape(nb, sl, MLA_ROPE) for o in outs])
        hg = jnp.stack([_state_from_bd(o[2]) for o in outs])
        half = S5_FLAT // 2
        re = jnp.stack([o[3][:, :half].reshape(nb, S5_GROUPS, S5_STATE) for o in outs])
        im = jnp.stack([o[3][:, half:].reshape(nb, S5_GROUPS, S5_STATE) for o in outs])
        return kv, pe, hg, re, im

    p_kv, p_pe, p_hg, p_re, p_im = gather(outs_p, bp, lp)
    s_kv, s_pe, s_hg, s_re, s_im = gather(outs_s, bs, ls)
    return (xp.reshape(bp, lp, D_MODEL), xs.reshape(bs, ls, D_MODEL),
            p_kv, p_pe, p_hg, p_re, p_im, s_kv, s_pe, s_hg, s_re, s_im)
```

```python
import functools
import math

import jax
import jax.numpy as jnp
from jax import lax
from jax.experimental import pallas as pl
from jax.experimental.pallas import tpu as pltpu

F32 = jnp.float32
BF16 = jnp.bfloat16

D_MODEL = 1024
CHUNK = 64
EPS = 1e-5
NEG_BIG = -1e30

S5_WIDTH = 256
S5_GROUP = 16
S5_GROUPS = 16
S5_STATE = 64
S5_T = 8
S5_ROW = S5_T * S5_WIDTH
S5_FLAT = 2 * S5_GROUPS * S5_STATE

HG_HEADS = 4
HG_DK = 64
HG_DV = 64
HG_WIDTH = 256
HG_ROWS = 16

MLA_HEADS = 8
MLA_Q_RANK = 256
MLA_KV_RANK = 128
MLA_NOPE = 64
MLA_ROPE = 32
MLA_V = 64
MLA_WIDTH = 512
ROPE_THETA = 10000.0
KEY_BLOCK_PROMPT = 512
KEY_BLOCK_SAMPLE = 256
QK_WIDTH = 256
VT_ROWS = MLA_KV_RANK + 16
LOG2E = 1.4426950408889634
ATTN_COLS = 512

D_FF = 4096
FF_CHUNK = 1024
IN_PAD = 1920

LANES = 128
VMEM_LIMIT = 56 * 1024 * 1024


def _cparams(*sem):
    return pltpu.CompilerParams(dimension_semantics=sem, vmem_limit_bytes=VMEM_LIMIT)


def _const_spec(shape):
    nd = len(shape)
    return pl.BlockSpec(shape, lambda *_: (0,) * nd, pipeline_mode=pl.Buffered(1))


def _rms(x, g):
    y = x * lax.rsqrt(jnp.mean(x * x, axis=-1, keepdims=True) + EPS)
    return y * g


def _dot(a, b):
    return jnp.dot(a, b, preferred_element_type=F32)


def _dot_t0(a, b):
    return lax.dot_general(a, b, (((0,), (0,)), ((), ())), preferred_element_type=F32)


def _dot_t1(a, b):
    return lax.dot_general(a, b, (((1,), (1,)), ((), ())), preferred_element_type=F32)


def _split3(x):
    hi = x.astype(BF16)
    r1 = x - hi.astype(F32)
    mid = r1.astype(BF16)
    lo = (r1 - mid.astype(F32)).astype(BF16)
    return hi, mid, lo


def _in_kernel(x_ref, g1_ref, win_ref, qg_ref, wuq_ref, wuk_ref, place_ref, kvg_ref, cos_ref, sin_ref,
               u_ref, hq_ref, hf_ref, hi_ref, hg_ref, ckv_ref, kpe_ref, qp_ref, kk_ref, vt_ref):
    h = _rms(x_ref[...], g1_ref[...])
    proj = _dot(h.astype(BF16), win_ref[...])
    u_ref[...] = proj[:, 0:256]
    hq_ref[...] = proj[:, 256:512]
    hf_ref[...] = proj[:, 512:768]
    hi_ref[...] = proj[:, 768:1024]
    hg_ref[...] = proj[:, 1024:1280]

    cos = cos_ref[...]
    sin = sin_ref[...]
    cos2 = jnp.concatenate([cos, cos], axis=-1)
    sin2 = jnp.concatenate([sin, sin], axis=-1)

    cqn = _rms(proj[:, 1280:1536], qg_ref[...])
    q = _dot(cqn.astype(BF16), wuq_ref[...])
    scale = (MLA_NOPE + MLA_ROPE) ** -0.5 * LOG2E
    q_pe = (q[:, 512:768] * cos2 + q[:, 768:1024] * sin2) * scale
    q_lat = _dot(q[:, 0:512].astype(BF16), wuk_ref[...]) * scale
    pe_pl = _dot(q_pe.astype(BF16), place_ref[...])
    for hd in range(MLA_HEADS):
        qp_ref[:, hd * 256:hd * 256 + 128] = q_lat[:, hd * 128:(hd + 1) * 128].astype(BF16)
        qp_ref[:, hd * 256 + 128:(hd + 1) * 256] = pe_pl[:, hd * 128:(hd + 1) * 128].astype(BF16)

    c_kv = _rms(proj[:, 1536:1664], kvg_ref[...])
    ckv_ref[...] = c_kv
    kpe = proj[:, 1664:1792] * cos + proj[:, 1792:1920] * sin
    kpe_ref[...] = kpe[:, 0:MLA_ROPE]
    kk_ref[...] = jnp.concatenate([c_kv, kpe], axis=-1).astype(BF16)
    tm = c_kv.shape[0]
    ones_row = (lax.broadcasted_iota(jnp.int32, (VT_ROWS - MLA_KV_RANK, tm), 0) == 0).astype(F32)
    vt_ref[0] = jnp.concatenate([c_kv.T, ones_row], axis=0).astype(BF16)


def _in_call(x, lw, cos, sin, tm):
    t = x.shape[0]
    row = lambda w: pl.BlockSpec((tm, w), lambda i: (i, 0))
    outs = [jax.ShapeDtypeStruct((t, 256), F32)] * 5 + [
        jax.ShapeDtypeStruct((t, MLA_KV_RANK), F32),
        jax.ShapeDtypeStruct((t, MLA_ROPE), F32),
        jax.ShapeDtypeStruct((t, MLA_HEADS * QK_WIDTH), BF16),
        jax.ShapeDtypeStruct((t, QK_WIDTH), BF16),
        jax.ShapeDtypeStruct((t // tm, VT_ROWS, tm), BF16),
    ]
    out_specs = [row(256)] * 5 + [row(MLA_KV_RANK), row(MLA_ROPE), row(MLA_HEADS * QK_WIDTH), row(QK_WIDTH),
                                  pl.BlockSpec((1, VT_ROWS, tm), lambda i: (i, 0, 0))]
    return pl.pallas_call(
        _in_kernel,
        grid=(t // tm,),
        in_specs=[row(D_MODEL), _const_spec((1, D_MODEL)), _const_spec((D_MODEL, IN_PAD)),
                  _const_spec((1, MLA_Q_RANK)), _const_spec((MLA_Q_RANK, 1024)), _const_spec((512, 1024)),
                  _const_spec((256, 1024)), _const_spec((1, MLA_KV_RANK)), row(LANES), row(LANES)],
        out_specs=out_specs,
        out_shape=outs,
        compiler_params=_cparams("arbitrary"),
    )(x, lw["norm1_g"], lw["w_in"], lw["q_norm_g"], lw["w_uq"], lw["w_uk"], lw["place"], lw["kv_norm_g"],
      cos, sin)


def _s5_prompt_kernel(u_ref, m1_ref, m2_ref, m3_ref, ab_ref, y_ref, hfin_ref, s_scr, hp_scr, h_scr):
    half = S5_FLAT // 2
    tn = u_ref.shape[0]

    @pl.when(pl.program_id(0) == 0)
    def _():
        h_scr[...] = jnp.zeros_like(h_scr)

    u = u_ref[...].astype(BF16)
    s_scr[...] = _dot(u, m2_ref[...])
    ar = ab_ref[:, 0:half]
    ai = ab_ref[:, half:S5_FLAT]

    def body(i, carry):
        hr, hi = carry
        hp_scr[pl.ds(i, 1), 0:half] = hr
        hp_scr[pl.ds(i, 1), half:S5_FLAT] = hi
        sr = s_scr[pl.ds(i, 1), 0:half]
        si = s_scr[pl.ds(i, 1), half:S5_FLAT]
        return ar * hr - ai * hi + sr, ar * hi + ai * hr + si

    hr, hi = lax.fori_loop(0, tn, body, (h_scr[:, 0:half], h_scr[:, half:S5_FLAT]), unroll=8)
    h_scr[:, 0:half] = hr
    h_scr[:, half:S5_FLAT] = hi
    y_ref[...] = _dot(u, m1_ref[...]) + _dot(hp_scr[...].astype(BF16), m3_ref[...])
    hfin_ref[...] = h_scr[...]


def _s5_prompt_call(u_rows, sw, tn):
    n = u_rows.shape[0]
    return pl.pallas_call(
        _s5_prompt_kernel,
        grid=(n // tn,),
        in_specs=[pl.BlockSpec((tn, S5_ROW), lambda i: (i, 0)),
                  _const_spec((S5_ROW, S5_ROW)), _const_spec((S5_ROW, S5_FLAT)), _const_spec((S5_FLAT, S5_ROW)),
                  _const_spec((1, S5_FLAT))],
        out_specs=[pl.BlockSpec((tn, S5_ROW), lambda i: (i, 0)), pl.BlockSpec((1, S5_FLAT), lambda i: (0, 0))],
        out_shape=[jax.ShapeDtypeStruct((n, S5_ROW), F32), jax.ShapeDtypeStruct((1, S5_FLAT), F32)],
        scratch_shapes=[pltpu.VMEM((tn, S5_FLAT), F32), pltpu.VMEM((tn, S5_FLAT), F32),
                        pltpu.VMEM((1, S5_FLAT), F32)],
        compiler_params=_cparams("arbitrary"),
    )(u_rows, sw["m1"], sw["m2"], sw["m3"], sw["ab"])


def _s5_sample_kernel(u_ref, h0_ref, m1_ref, m2_ref, m3_ref, ab_ref, y_ref, hfin_ref, *, n_rows):
    half = S5_FLAT // 2
    ar = ab_ref[:, 0:half]
    ai = ab_ref[:, half:S5_FLAT]
    us = [u_ref[:, c * S5_ROW:(c + 1) * S5_ROW].astype(BF16) for c in range(n_rows)]
    u_all = jnp.concatenate(us, axis=0)
    b = us[0].shape[0]
    s_all = _dot(u_all, m2_ref[...])
    y1_all = _dot(u_all, m1_ref[...])
    hr = h0_ref[:, 0:half]
    hi = h0_ref[:, half:S5_FLAT]
    enter = []
    for c in range(n_rows):
        enter.append(jnp.concatenate([hr, hi], axis=-1))
        sr = s_all[c * b:(c + 1) * b, 0:half]
        si = s_all[c * b:(c + 1) * b, half:S5_FLAT]
        hr, hi = ar * hr - ai * hi + sr, ar * hi + ai * hr + si
    y_all = y1_all + _dot(jnp.concatenate(enter, axis=0).astype(BF16), m3_ref[...])
    for c in range(n_rows):
        y_ref[:, c * S5_ROW:(c + 1) * S5_ROW] = y_all[c * b:(c + 1) * b, :]
    hfin_ref[:, 0:half] = hr
    hfin_ref[:, half:S5_FLAT] = hi


def _s5_sample_call(u_seq, h0, sw):
    b, w = u_seq.shape
    n_rows = w // S5_ROW
    full = lambda shape: pl.BlockSpec(shape, lambda i: (0,) * len(shape))
    return pl.pallas_call(
        functools.partial(_s5_sample_kernel, n_rows=n_rows),
        grid=(1,),
        in_specs=[full((b, w)), full((b, S5_FLAT)), _const_spec((S5_ROW, S5_ROW)),
                  _const_spec((S5_ROW, S5_FLAT)), _const_spec((S5_FLAT, S5_ROW)), _const_spec((1, S5_FLAT))],
        out_specs=[full((b, w)), full((b, S5_FLAT))],
        out_shape=[jax.ShapeDtypeStruct((b, w), F32), jax.ShapeDtypeStruct((b, S5_FLAT), F32)],
        compiler_params=_cparams("arbitrary"),
    )(u_seq, h0, sw["m1"], sw["m2"], sw["m3"], sw["ab"])


def _hg_gates(hq, hf, lb):
    sig = jax.nn.sigmoid(hf)
    f = lb + (1.0 - lb) * sig
    k = (1.0 - lb) * jax.nn.sigmoid(-hf)
    qf = hq * jax.nn.sigmoid(hq)
    return qf, k, f


def _hg_intra_kernel(hq_ref, hf_ref, hi_ref, lb_ref, ones_ref, o_ref, k_scr, f_scr, *, r_len):
    lb = lb_ref[...]
    ones_bd = ones_ref[...]
    w = HG_WIDTH
    for r in range(r_len):
        _, k, f = _hg_gates(hq_ref[:, r * w:(r + 1) * w], hf_ref[:, r * w:(r + 1) * w], lb)
        k_scr[r] = k
        f_scr[r] = f
    for r in range(r_len):
        hq = hq_ref[:, r * w:(r + 1) * w]
        qp = hq * jax.nn.sigmoid(hq)
        terms = []
        for s in range(r, -1, -1):
            terms.append((qp * k_scr[s]).astype(BF16))
            if s > 0:
                qp = qp * f_scr[s]
        att = _dot(jnp.concatenate(terms, axis=0), ones_bd)
        rows = hq.shape[0]
        acc = None
        for j, s in enumerate(range(r, -1, -1)):
            part = att[j * rows:(j + 1) * rows, :] * hi_ref[:, s * w:(s + 1) * w]
            acc = part if acc is None else acc + part
        o_ref[:, r * w:(r + 1) * w] = acc


def _hg_intra_call(hq, hf, hi, lb, ones_bd, r_len):
    rows = hq.shape[0]
    spec = pl.BlockSpec((HG_ROWS, r_len * HG_WIDTH), lambda i: (i, 0))
    return pl.pallas_call(
        functools.partial(_hg_intra_kernel, r_len=r_len),
        grid=(rows // HG_ROWS,),
        in_specs=[spec, spec, spec, _const_spec((1, HG_WIDTH)), _const_spec((HG_WIDTH, HG_WIDTH))],
        out_specs=spec,
        out_shape=jax.ShapeDtypeStruct(hq.shape, F32),
        scratch_shapes=[pltpu.VMEM((r_len, HG_ROWS, HG_WIDTH), F32), pltpu.VMEM((r_len, HG_ROWS, HG_WIDTH), F32)],
        compiler_params=_cparams("arbitrary"),
    )(hq, hf, hi, lb, ones_bd)


def _hg_inter_kernel(hq_ref, hf_ref, hi_ref, lb_ref, tril_ref, blk_ref, hmask_ref, s0_ref,
                     o_ref, sout_ref, s_scr, *, r_len, carry):
    n_sub = HG_ROWS
    if carry:
        @pl.when(pl.program_id(0) == 0)
        def _():
            s_scr[...] = s0_ref[0]

    qf, k, f = _hg_gates(hq_ref[...], hf_ref[...], lb_ref[...])
    g = jnp.log(f)
    g3 = _split3(g)
    tril = tril_ref[...]
    blk = blk_ref[...]
    bl = _dot(tril, g3[0]) + _dot(tril, g3[1]) + _dot(tril, g3[2])
    bsum = _dot(blk, g3[0]) + _dot(blk, g3[1]) + _dot(blk, g3[2])
    q_in = (qf * jnp.exp(bl)).astype(BF16)
    k_out = (k * jnp.exp(bsum - bl)).astype(BF16)
    v = hi_ref[...].astype(BF16)
    hmask = hmask_ref[...]
    ones = jnp.ones((r_len, HG_WIDTH), BF16)
    for j in range(n_sub):
        sl = slice(j * r_len, (j + 1) * r_len)
        s_in = s_scr[...] if carry else s0_ref[j]
        o_ref[sl, :] = _dot(q_in[sl], s_in.astype(BF16))
        w_new = _dot_t0(k_out[sl], v[sl]) * hmask
        gsum = _dot_t0(g3[0][sl], ones) + _dot_t0(g3[1][sl], ones) + _dot_t0(g3[2][sl], ones)
        s_new = jnp.exp(gsum) * s_in + w_new
        if carry:
            s_scr[...] = s_new
        else:
            sout_ref[j] = s_new
    if carry:
        sout_ref[0] = s_scr[...]


def _hg_inter_call(hq, hf, hi, lb, consts, s0, r_len, carry):
    t = hq.shape[0]
    tt = HG_ROWS * r_len
    spec = pl.BlockSpec((tt, HG_WIDTH), lambda i: (i, 0))
    if carry:
        s_spec = pl.BlockSpec((1, HG_WIDTH, HG_WIDTH), lambda i: (0, 0, 0))
    else:
        s_spec = pl.BlockSpec((HG_ROWS, HG_WIDTH, HG_WIDTH), lambda i: (i, 0, 0))
    return pl.pallas_call(
        functools.partial(_hg_inter_kernel, r_len=r_len, carry=carry),
        grid=(t // tt,),
        in_specs=[spec, spec, spec, _const_spec((1, HG_WIDTH)), _const_spec((tt, tt)), _const_spec((tt, tt)),
                  _const_spec((HG_WIDTH, HG_WIDTH)), s_spec],
        out_specs=[spec, s_spec],
        out_shape=[jax.ShapeDtypeStruct((t, HG_WIDTH), F32), jax.ShapeDtypeStruct(s0.shape, F32)],
        scratch_shapes=[pltpu.VMEM((HG_WIDTH, HG_WIDTH), F32)],
        compiler_params=_cparams("arbitrary"),
    )(hq, hf, hi, lb, consts["tril"], consts["blk"], consts["hmask"], s0)


def _attn_kernel(qp_ref, k_ref, vt_ref, wuv_ref, o_ref, q_scr, s_buf, mb_buf, m_scr, acc_scr,
                 *, tq, kb, prompt, n_blocks, n_keys, q_base):
    i = pl.program_id(0)
    ncol = MLA_HEADS * tq
    for hd in range(MLA_HEADS):
        q_scr[hd * tq:(hd + 1) * tq, :] = qp_ref[:, hd * QK_WIDTH:(hd + 1) * QK_WIDTH]
    m_scr[...] = jnp.full_like(m_scr, NEG_BIG)
    acc_scr[...] = jnp.zeros_like(acc_scr)
    q0 = i * tq if prompt else q_base
    n_full = (i * tq) // kb if prompt else n_blocks - 1

    cw = min(ATTN_COLS, ncol)
    n_chunks = ncol // cw

    def scores(b, c, masked):
        cols = slice(c * cw, (c + 1) * cw)
        s = _dot_t1(k_ref[0, b], q_scr[cols, :])
        if masked:
            kpos = b * kb + lax.broadcasted_iota(jnp.int32, (kb, cw), 0)
            qpos = q0 + ((c * cw + lax.broadcasted_iota(jnp.int32, (kb, cw), 1)) & (tq - 1))
            mask = (kpos >> 6) <= (qpos >> 6)
            if n_keys is not None:
                mask = mask & (kpos < n_keys)
            s = jnp.where(mask, s, NEG_BIG)
        s_buf[c % 2] = s
        mb_buf[c] = jnp.max(s, axis=0, keepdims=True)

    def values(b, c):
        cols = slice(c * cw, (c + 1) * cw)
        m_old = m_scr[:, cols]
        m_new = jnp.maximum(m_old, mb_buf[c])
        alpha = jnp.exp2(m_old - m_new)
        p = jnp.exp2(s_buf[c % 2] - m_new).astype(BF16)
        acc_scr[:, cols] = alpha * acc_scr[:, cols] + _dot(vt_ref[0, b], p)
        m_scr[:, cols] = m_new

    def block(b, masked):
        scores(b, 0, masked)
        for c in range(n_chunks):
            if c + 1 < n_chunks:
                scores(b, c + 1, masked)
            values(b, c)

    def body(b, carry):
        block(b, False)
        return carry

    lax.fori_loop(0, n_full, body, 0)
    block(n_full, True)

    o_lat_t = acc_scr[0:MLA_KV_RANK, :] / acc_scr[MLA_KV_RANK:MLA_KV_RANK + 1, :]
    o_lat = o_lat_t.T.astype(BF16)
    full = _dot(o_lat, wuv_ref[...])
    lane_head = lax.broadcasted_iota(jnp.int32, (tq, MLA_WIDTH), 1) // MLA_V
    out = jnp.zeros((tq, MLA_WIDTH), F32)
    for hd in range(MLA_HEADS):
        out = out + jnp.where(lane_head == hd, full[hd * tq:(hd + 1) * tq, :], 0.0)
    o_ref[...] = out


def _attn_call(qp, k_all, vt_all, wuv, tq, prompt, n_keys, q_base):
    t = qp.shape[0]
    n_blocks, kb = k_all.shape[1], k_all.shape[2]
    ncol = MLA_HEADS * tq
    cw = min(ATTN_COLS, ncol)
    if prompt:
        k_spec = pl.BlockSpec((1, n_blocks, kb, QK_WIDTH), lambda i: (0, 0, 0, 0), pipeline_mode=pl.Buffered(1))
        v_spec = pl.BlockSpec((1, n_blocks, VT_ROWS, kb), lambda i: (0, 0, 0, 0), pipeline_mode=pl.Buffered(1))
    else:
        k_spec = pl.BlockSpec((1, n_blocks, kb, QK_WIDTH), lambda i: (i, 0, 0, 0))
        v_spec = pl.BlockSpec((1, n_blocks, VT_ROWS, kb), lambda i: (i, 0, 0, 0))
    return pl.pallas_call(
        functools.partial(_attn_kernel, tq=tq, kb=kb, prompt=prompt, n_blocks=n_blocks, n_keys=n_keys,
                          q_base=q_base),
        grid=(t // tq,),
        in_specs=[pl.BlockSpec((tq, MLA_HEADS * QK_WIDTH), lambda i: (i, 0)), k_spec, v_spec,
                  _const_spec((MLA_KV_RANK, MLA_WIDTH))],
        out_specs=pl.BlockSpec((tq, MLA_WIDTH), lambda i: (i, 0)),
        out_shape=jax.ShapeDtypeStruct((t, MLA_WIDTH), F32),
        scratch_shapes=[pltpu.VMEM((ncol, QK_WIDTH), BF16), pltpu.VMEM((2, kb, cw), F32),
                        pltpu.VMEM((ncol // cw, 1, cw), F32), pltpu.VMEM((1, ncol), F32),
                        pltpu.VMEM((VT_ROWS, ncol), F32)],
        compiler_params=_cparams("arbitrary"),
    )(qp, k_all, vt_all, wuv)


def _out_kernel(x_ref, ys_ref, u_ref, oa_ref, ob_ref, hg_ref, mla_ref, d_ref, wglu_ref, bglu_ref, og_ref,
                wout_ref, g2_ref, wup_ref, wdn_ref, fg_ref, o_ref, *, final):
    y = ys_ref[...] + d_ref[...] * u_ref[...]
    z = jax.nn.gelu(y, approximate=True)
    s5 = z * jax.nn.sigmoid(_dot(z.astype(BF16), wglu_ref[...]) + bglu_ref[...])
    og = og_ref[...]
    hgate = hg_ref[...]
    mixed = jnp.concatenate([
        _rms(s5, og[:, 0:256]),
        _rms(oa_ref[...] + ob_ref[...], og[:, 256:512]) * (hgate * jax.nn.sigmoid(hgate)),
        _rms(mla_ref[...], og[:, 512:1024]),
    ], axis=-1)
    x1 = x_ref[...] + _dot(mixed.astype(BF16), wout_ref[...])
    h2 = _rms(x1, g2_ref[...]).astype(BF16)
    acc = x1
    for c in range(D_FF // FF_CHUNK):
        up = _dot(h2, wup_ref[:, c * FF_CHUNK:(c + 1) * FF_CHUNK])
        act = jnp.square(jnp.maximum(up, 0.0)).astype(BF16)
        acc = acc + _dot(act, wdn_ref[c * FF_CHUNK:(c + 1) * FF_CHUNK, :])
    if final:
        acc = _rms(acc, fg_ref[...])
    o_ref[...] = acc


def _out_call(x, ys, u, oa, ob, hg, mla, lw, final_g, tm, final):
    t = x.shape[0]
    row = lambda w: pl.BlockSpec((tm, w), lambda i: (i, 0))
    return pl.pallas_call(
        functools.partial(_out_kernel, final=final),
        grid=(t // tm,),
        in_specs=[row(D_MODEL), row(256), row(256), row(256), row(256), row(256), row(MLA_WIDTH),
                  _const_spec((1, 256)), _const_spec((256, 256)), _const_spec((1, 256)), _const_spec((1, D_MODEL)),
                  _const_spec((D_MODEL, D_MODEL)), _const_spec((1, D_MODEL)), _const_spec((D_MODEL, D_FF)),
                  _const_spec((D_FF, D_MODEL)), _const_spec((1, D_MODEL))],
        out_specs=row(D_MODEL),
        out_shape=jax.ShapeDtypeStruct((t, D_MODEL), F32),
        compiler_params=_cparams("arbitrary"),
    )(x, ys, u, oa, ob, hg, mla, lw["s5_d"], lw["s5_w_glu"], lw["s5_b_glu"], lw["out_norm_g"], lw["w_out"],
      lw["norm2_g"], lw["w_up"], lw["w_down"], final_g)


def _rot_cols(w):
    half = MLA_ROPE // 2
    return jnp.concatenate([-w[..., half:], w[..., :half]], axis=-1)


def _prep_layer(w_in, w_uq, w_uk, w_uv):
    kpe = w_in[:, 1664:1696]
    pad = jnp.zeros((D_MODEL, 96), F32)
    w_in_p = jnp.concatenate([w_in[:, :1664], kpe, pad, _rot_cols(kpe), pad], axis=1).astype(BF16)
    uq = w_uq.reshape(MLA_Q_RANK, MLA_HEADS, MLA_NOPE + MLA_ROPE)
    nope = uq[:, :, :MLA_NOPE].reshape(MLA_Q_RANK, 512)
    pe = uq[:, :, MLA_NOPE:]
    w_uq_p = jnp.concatenate([nope, pe.reshape(MLA_Q_RANK, 256), _rot_cols(pe).reshape(MLA_Q_RANK, 256)],
                             axis=1).astype(BF16)
    eye_h = jnp.eye(MLA_HEADS, dtype=F32)
    wuk_bd = jnp.einsum("chd,hk->hdkc", w_uk, eye_h).reshape(512, 1024).astype(BF16)
    place = jnp.einsum("hk,rc->hrkc", eye_h, jnp.eye(MLA_ROPE, LANES, dtype=F32)).reshape(256, 1024).astype(BF16)
    return w_in_p, w_uq_p, wuk_bd, place, w_uv.reshape(MLA_KV_RANK, MLA_WIDTH).astype(BF16)


def _prep_s5(lam_re, lam_im, log_dt, b_re, b_im, c_re, c_im):
    hp = lax.Precision.HIGHEST
    t = S5_T
    dt = jnp.exp(log_dt)[:, None]
    mag1 = jnp.exp(lam_re * dt)
    a_re, a_im = mag1 * jnp.cos(lam_im * dt), mag1 * jnp.sin(lam_im * dt)
    pw_re, pw_im = [jnp.ones_like(a_re)], [jnp.zeros_like(a_im)]
    for _ in range(t):
        pr, pi = pw_re[-1], pw_im[-1]
        pw_re.append(pr * a_re - pi * a_im)
        pw_im.append(pr * a_im + pi * a_re)
    p_re, p_im = jnp.stack(pw_re), jnp.stack(pw_im)
    den = lam_re * lam_re + lam_im * lam_im
    i_re, i_im = lam_re / den, -lam_im / den
    z_re = (a_re - 1.0) * i_re - a_im * i_im
    z_im = (a_re - 1.0) * i_im + a_im * i_re
    bb_re = z_re[..., None] * b_re - z_im[..., None] * b_im
    bb_im = z_re[..., None] * b_im + z_im[..., None] * b_re
    cp_re = c_re[None] * p_re[:, :, None, :] - c_im[None] * p_im[:, :, None, :]
    cp_im = c_re[None] * p_im[:, :, None, :] + c_im[None] * p_re[:, :, None, :]
    kern = (jnp.einsum("tgap,gph->tgah", cp_re[:t], bb_re, precision=hp)
            - jnp.einsum("tgap,gph->tgah", cp_im[:t], bb_im, precision=hp))
    eye_g = jnp.eye(S5_GROUPS, dtype=F32)
    s_idx = jnp.arange(t)[:, None]
    t_idx = jnp.arange(t)[None, :]
    lag = jnp.clip(t_idx - s_idx, 0, t - 1)
    kts = jnp.where((t_idx >= s_idx)[:, :, None, None, None], kern[lag], 0.0)
    m1 = jnp.einsum("stgab,gk->sgbtka", kts, eye_g).reshape(S5_ROW, S5_ROW)
    rev_re, rev_im = p_re[t - 1::-1][:t], p_im[t - 1::-1][:t]
    w2_re = rev_re[..., None] * bb_re[None] - rev_im[..., None] * bb_im[None]
    w2_im = rev_re[..., None] * bb_im[None] + rev_im[..., None] * bb_re[None]
    m2 = jnp.einsum("rsgph,gk->sghrkp", jnp.stack([w2_re, w2_im]), eye_g).reshape(S5_ROW, S5_FLAT)
    m3 = jnp.einsum("rtgap,gk->rgptka", jnp.stack([cp_re[1:], -cp_im[1:]]), eye_g).reshape(S5_FLAT, S5_ROW)
    ab = jnp.concatenate([p_re[t].reshape(1, -1), p_im[t].reshape(1, -1)], axis=1)
    return {"m1": m1.astype(BF16), "m2": m2.astype(BF16), "m3": m3.astype(BF16), "ab": ab}


def _hg_consts(r_len):
    tt = HG_ROWS * r_len
    r = jnp.arange(tt)
    same = (r[:, None] // r_len) == (r[None, :] // r_len)
    tril = (same & (r[None, :] <= r[:, None])).astype(BF16)
    hd = jnp.arange(HG_WIDTH) // HG_DK
    hmask = (hd[:, None] == hd[None, :]).astype(F32)
    return {"tril": tril, "blk": same.astype(BF16), "hmask": hmask}


def _rope_tables(pos):
    half = MLA_ROPE // 2
    inv = ROPE_THETA ** (-jnp.arange(half, dtype=F32) / half)
    ang = pos.astype(F32)[:, None] * inv[None, :]
    reps = LANES // half
    return jnp.tile(jnp.cos(ang), (1, reps)), jnp.tile(jnp.sin(ang), (1, reps))


def _state_to_bd(s):
    eye_h = jnp.eye(HG_HEADS, dtype=F32)
    return jnp.einsum("bhdv,hk->bhdkv", s, eye_h).reshape(s.shape[0], HG_WIDTH, HG_WIDTH)


def _state_from_bd(s):
    b = s.shape[0]
    s5 = s.reshape(b, HG_HEADS, HG_DK, HG_HEADS, HG_DV)
    return jnp.stack([s5[:, h, :, h, :] for h in range(HG_HEADS)], axis=1)


def _layer(x, lw, sw, cos, sin, lb, hgc, ones_bd, final_g, final, *, prompt, n_seq, s5_h0, hg_s0, kv_past, pe_past):
    t = x.shape[0]
    seq = t // n_seq
    tm = KEY_BLOCK_PROMPT
    u, hq, hf, hi, hg, c_kv, k_pe, qp, kk, vt = _in_call(x, lw, cos, sin, tm)

    if prompt:
        ys, s5_fin = _s5_prompt_call(u.reshape(t // S5_T, S5_ROW), sw, min(256, t // S5_T))
    else:
        ys, s5_fin = _s5_sample_call(u.reshape(n_seq, seq * S5_WIDTH), s5_h0, sw)
    ys = ys.reshape(t, S5_WIDTH)

    r_len = 32 if prompt else seq
    av = lambda a: a.reshape(t // r_len, r_len * HG_WIDTH)
    oa = _hg_intra_call(av(hq), av(hf), av(hi), lb, ones_bd, r_len).reshape(t, HG_WIDTH)
    ob, hg_fin = _hg_inter_call(hq, hf, hi, lb, hgc, hg_s0, r_len, carry=prompt)

    if prompt:
        k_all = kk.reshape(1, t // tm, tm, QK_WIDTH)
        mla = _attn_call(qp, k_all, vt.reshape(1, t // tm, VT_ROWS, tm), lw["w_uv"], 256, True, None, 0)
    else:
        past = kv_past.shape[1]
        kb = KEY_BLOCK_SAMPLE
        n_keys = past + seq
        n_pad = -n_keys % kb
        n_blocks = (n_keys + n_pad) // kb
        k_past = jnp.concatenate([kv_past, pe_past, jnp.zeros((n_seq, past, QK_WIDTH - 160), F32)], axis=-1)
        k_all = jnp.concatenate([k_past.astype(BF16), kk.reshape(n_seq, seq, QK_WIDTH),
                                 jnp.zeros((n_seq, n_pad, QK_WIDTH), BF16)], axis=1)
        k_all = k_all.reshape(n_seq, n_blocks, kb, QK_WIDTH)
        v_all = jnp.concatenate([kv_past, c_kv.reshape(n_seq, seq, MLA_KV_RANK),
                                 jnp.zeros((n_seq, n_pad, MLA_KV_RANK), F32)], axis=1).astype(BF16)
        vt_all = v_all.reshape(n_seq, n_blocks, kb, MLA_KV_RANK).swapaxes(2, 3)
        ones_row = jnp.zeros((n_seq, n_blocks, VT_ROWS - MLA_KV_RANK, kb), BF16).at[:, :, 0, :].set(1.0)
        vt_all = jnp.concatenate([vt_all, ones_row], axis=2)
        mla = _attn_call(qp, k_all, vt_all, lw["w_uv"], seq, False, n_keys, past)

    x_new = _out_call(x, ys, u, oa, ob, hg, mla, lw, final_g, tm, final)
    return x_new, c_kv, k_pe, hg_fin, s5_fin


def kernel(x_prompt, x_sample, cache_mla_kv, cache_mla_pe, state_hgrn, state_s5_re, state_s5_im, norm1_g, w_in, s5_lambda_re, s5_lambda_im, s5_log_dt, s5_b_re, s5_b_im, s5_c_re, s5_c_im, s5_d, s5_w_glu, s5_b_glu, hgrn_lb_logits, mla_q_norm_g, mla_w_uq, mla_kv_norm_g, mla_w_uk, mla_w_uv, out_norm_g, w_out, norm2_g, w_up, w_down, final_norm_g):
    depth = w_in.shape[0]
    bp, lp = x_prompt.shape[0], x_prompt.shape[1]
    bs, ls = x_sample.shape[0], x_sample.shape[1]
    past = cache_mla_kv.shape[2]
    assert bp == 1 and ls == 2 * S5_T and bs % HG_ROWS == 0

    cos_p, sin_p = _rope_tables(jnp.arange(lp, dtype=jnp.int32))
    cos_s, sin_s = _rope_tables(past + jnp.arange(ls, dtype=jnp.int32))
    cos_s, sin_s = jnp.tile(cos_s, (bs, 1)), jnp.tile(sin_s, (bs, 1))

    lb_p = jax.nn.softmax(hgrn_lb_logits.astype(F32), axis=0)
    lb_all = jnp.cumsum(lb_p, axis=0) - lb_p[0]
    hgc_p, hgc_s = _hg_consts(32), _hg_consts(ls)
    hd = jnp.arange(HG_WIDTH) // HG_DK
    ones_bd = (hd[:, None] == hd[None, :]).astype(BF16)
    row = lambda v: v.reshape(1, -1).astype(F32)
    final_g = row(final_norm_g)

    xp = x_prompt.reshape(bp * lp, D_MODEL)
    xs = x_sample.reshape(bs * ls, D_MODEL)
    outs_p, outs_s = [], []
    for l in range(depth):
        w_in_p, w_uq_p, wuk_bd, place, w_uv_p = _prep_layer(w_in[l], mla_w_uq[l], mla_w_uk[l], mla_w_uv[l])
        lw = {
            "norm1_g": row(norm1_g[l]), "w_in": w_in_p, "q_norm_g": row(mla_q_norm_g[l]), "w_uq": w_uq_p,
            "w_uk": wuk_bd, "place": place, "kv_norm_g": row(mla_kv_norm_g[l]), "w_uv": w_uv_p,
            "s5_d": row(s5_d[l]), "s5_w_glu": s5_w_glu[l].astype(BF16), "s5_b_glu": row(s5_b_glu[l]),
            "out_norm_g": row(out_norm_g[l]), "w_out": w_out[l].astype(BF16), "norm2_g": row(norm2_g[l]),
            "w_up": w_up[l].astype(BF16), "w_down": w_down[l].astype(BF16),
        }
        sw = _prep_s5(s5_lambda_re[l], s5_lambda_im[l], s5_log_dt[l], s5_b_re[l], s5_b_im[l],
                      s5_c_re[l], s5_c_im[l])
        lb = row(lb_all[l])
        final = l == depth - 1
        xp, a, b, c, d = _layer(xp, lw, sw, cos_p, sin_p, lb, hgc_p, ones_bd, final_g, final, prompt=True,
                                n_seq=1, s5_h0=None, hg_s0=jnp.zeros((1, HG_WIDTH, HG_WIDTH), F32),
                                kv_past=None, pe_past=None)
        outs_p.append((a, b, c, d))
        h0 = jnp.concatenate([state_s5_re[l].reshape(bs, -1), state_s5_im[l].reshape(bs, -1)], axis=1)
        xs, a, b, c, d = _layer(xs, lw, sw, cos_s, sin_s, lb, hgc_s, ones_bd, final_g, final, prompt=False,
                                n_seq=bs, s5_h0=h0, hg_s0=_state_to_bd(state_hgrn[l]),
                                kv_past=cache_mla_kv[l], pe_past=cache_mla_pe[l])
        outs_s.append((a, b, c, d))

    def gather(outs, nb, sl):
        kv = jnp.stack([o[0].reshape(nb, sl, MLA_KV_RANK) for o in outs])
        pe = jnp.stack([o[1].reshape(nb, sl, MLA_ROPE) for o in outs])
        hg = jnp.stack([_state_from_bd(o[2]) for o in outs])
        half = S5_FLAT // 2
        re = jnp.stack([o[3][:, :half].reshape(nb, S5_GROUPS, S5_STATE) for o in outs])
        im = jnp.stack([o[3][:, half:].reshape(nb, S5_GROUPS, S5_STATE) for o in outs])
        return kv, pe, hg, re, im

    p_kv, p_pe, p_hg, p_re, p_im = gather(outs_p, bp, lp)
    s_kv, s_pe, s_hg, s_re, s_im = gather(outs_s, bs, ls)
    return (xp.reshape(bp, lp, D_MODEL), xs.reshape(bs, ls, D_MODEL),
            p_kv, p_pe, p_hg, p_re, p_im, s_kv, s_pe, s_hg, s_re, s_im)
```

```python
import functools
import math

import jax
import jax.numpy as jnp
from jax import lax
from jax.experimental import pallas as pl
from jax.experimental.pallas import tpu as pltpu

F32 = jnp.float32
BF16 = jnp.bfloat16

D_MODEL = 1024
CHUNK = 64
EPS = 1e-5
NEG_BIG = -1e30

S5_WIDTH = 256
S5_GROUP = 16
S5_GROUPS = 16
S5_STATE = 64
S5_T = 8
S5_ROW = S5_T * S5_WIDTH
S5_FLAT = 2 * S5_GROUPS * S5_STATE

HG_HEADS = 4
HG_DK = 64
HG_DV = 64
HG_WIDTH = 256
HG_ROWS = 16

MLA_HEADS = 8
MLA_Q_RANK = 256
MLA_KV_RANK = 128
MLA_NOPE = 64
MLA_ROPE = 32
MLA_V = 64
MLA_WIDTH = 512
ROPE_THETA = 10000.0
KEY_BLOCK_PROMPT = 512
QK_WIDTH = 256
VT_ROWS = MLA_KV_RANK + 16
LOG2E = 1.4426950408889634
ATTN_COLS = 512
ATTN_QUERIES = 512

D_FF = 4096
FF_CHUNK = 1024
IN_PAD = 1920

LANES = 128
VMEM_LIMIT = 56 * 1024 * 1024


def _cparams(*sem):
    return pltpu.CompilerParams(dimension_semantics=sem, vmem_limit_bytes=VMEM_LIMIT)


def _const_spec(shape):
    nd = len(shape)
    return pl.BlockSpec(shape, lambda *_: (0,) * nd, pipeline_mode=pl.Buffered(1))


def _rms(x, g):
    y = x * lax.rsqrt(jnp.mean(x * x, axis=-1, keepdims=True) + EPS)
    return y * g


def _dot(a, b):
    return jnp.dot(a, b, preferred_element_type=F32)


def _dot_t0(a, b):
    return lax.dot_general(a, b, (((0,), (0,)), ((), ())), preferred_element_type=F32)


def _dot_t1(a, b):
    return lax.dot_general(a, b, (((1,), (1,)), ((), ())), preferred_element_type=F32)


def _split3(x):
    hi = x.astype(BF16)
    r1 = x - hi.astype(F32)
    mid = r1.astype(BF16)
    lo = (r1 - mid.astype(F32)).astype(BF16)
    return hi, mid, lo


def _in_kernel(x_ref, g1_ref, win_ref, qg_ref, wuq_ref, wuk_ref, place_ref, kvg_ref, cos_ref, sin_ref,
               u_ref, urow_ref, hq_ref, hf_ref, hi_ref, hg_ref, ckv_ref, kpe_ref, qp_ref, kk_ref, vt_ref, u_scr):
    h = _rms(x_ref[...], g1_ref[...])
    proj = _dot(h.astype(BF16), win_ref[...])
    u_ref[...] = proj[:, 0:256]
    n_row = proj.shape[0] // S5_T
    for hv in range(2):
        u_scr[hv] = proj[:, hv * LANES:(hv + 1) * LANES]
    for s in range(S5_T):
        for hv in range(2):
            lo = s * S5_WIDTH + hv * LANES
            urow_ref[:, lo:lo + LANES] = u_scr[hv, pl.ds(s, n_row, stride=S5_T), :]
    hq_ref[...] = proj[:, 256:512]
    hf_ref[...] = proj[:, 512:768]
    hi_ref[...] = proj[:, 768:1024]
    hg_ref[...] = proj[:, 1024:1280]

    cos = cos_ref[...]
    sin = sin_ref[...]
    cos2 = jnp.concatenate([cos, cos], axis=-1)
    sin2 = jnp.concatenate([sin, sin], axis=-1)

    cqn = _rms(proj[:, 1280:1536], qg_ref[...])
    q = _dot(cqn.astype(BF16), wuq_ref[...])
    scale = (MLA_NOPE + MLA_ROPE) ** -0.5 * LOG2E
    q_pe = (q[:, 512:768] * cos2 + q[:, 768:1024] * sin2) * scale
    q_lat = _dot(q[:, 0:512].astype(BF16), wuk_ref[...]) * scale
    pe_pl = _dot(q_pe.astype(BF16), place_ref[...])
    for hd in range(MLA_HEADS):
        qp_ref[:, hd * 256:hd * 256 + 128] = q_lat[:, hd * 128:(hd + 1) * 128].astype(BF16)
        qp_ref[:, hd * 256 + 128:(hd + 1) * 256] = pe_pl[:, hd * 128:(hd + 1) * 128].astype(BF16)

    c_kv = _rms(proj[:, 1536:1664], kvg_ref[...])
    ckv_ref[...] = c_kv
    kpe = proj[:, 1664:1792] * cos + proj[:, 1792:1920] * sin
    kpe_ref[...] = kpe[:, 0:MLA_ROPE]
    kk_ref[...] = jnp.concatenate([c_kv, kpe], axis=-1).astype(BF16)
    tm = c_kv.shape[0]
    ones_row = (lax.broadcasted_iota(jnp.int32, (VT_ROWS - MLA_KV_RANK, tm), 0) == 0).astype(F32)
    vt_ref[0] = jnp.concatenate([c_kv.T, ones_row], axis=0).astype(BF16)


def _in_call(x, lw, cos, sin, tm):
    t = x.shape[0]
    row = lambda w: pl.BlockSpec((tm, w), lambda i: (i, 0))
    outs = [jax.ShapeDtypeStruct((t, 256), F32), jax.ShapeDtypeStruct((t // S5_T, S5_ROW), F32)] + [
        jax.ShapeDtypeStruct((t, 256), F32)] * 4 + [
        jax.ShapeDtypeStruct((t, MLA_KV_RANK), F32),
        jax.ShapeDtypeStruct((t, MLA_ROPE), F32),
        jax.ShapeDtypeStruct((t, MLA_HEADS * QK_WIDTH), BF16),
        jax.ShapeDtypeStruct((t, QK_WIDTH), BF16),
        jax.ShapeDtypeStruct((t // tm, VT_ROWS, tm), BF16),
    ]
    out_specs = [row(256), pl.BlockSpec((tm // S5_T, S5_ROW), lambda i: (i, 0))] + [row(256)] * 4 + [
        row(MLA_KV_RANK), row(MLA_ROPE), row(MLA_HEADS * QK_WIDTH), row(QK_WIDTH),
        pl.BlockSpec((1, VT_ROWS, tm), lambda i: (i, 0, 0))]
    return pl.pallas_call(
        _in_kernel,
        grid=(t // tm,),
        in_specs=[row(D_MODEL), _const_spec((1, D_MODEL)), _const_spec((D_MODEL, IN_PAD)),
                  _const_spec((1, MLA_Q_RANK)), _const_spec((MLA_Q_RANK, 1024)), _const_spec((512, 1024)),
                  _const_spec((256, 1024)), _const_spec((1, MLA_KV_RANK)), row(LANES), row(LANES)],
        out_specs=out_specs,
        out_shape=outs,
        scratch_shapes=[pltpu.VMEM((2, tm, LANES), F32)],
        compiler_params=_cparams("arbitrary"),
    )(x, lw["norm1_g"], lw["w_in"], lw["q_norm_g"], lw["w_uq"], lw["w_uk"], lw["place"], lw["kv_norm_g"],
      cos, sin)


def _s5_in_row(u, krev_ref):
    w = S5_WIDTH
    parts = [_dot(u[:, 0:(t + 1) * w], krev_ref[(S5_T - 1 - t) * w:S5_T * w, :]) for t in range(S5_T)]
    return jnp.concatenate(parts, axis=1)


def _s5_prompt_kernel(u_ref, krev_ref, m2_ref, m3_ref, ab_ref, y_ref, hfin_ref, s_scr, hp_scr, h_scr):
    half = S5_FLAT // 2
    tn = u_ref.shape[0]

    @pl.when(pl.program_id(0) == 0)
    def _():
        h_scr[...] = jnp.zeros_like(h_scr)

    u = u_ref[...].astype(BF16)
    s_scr[...] = _dot(u, m2_ref[...])
    ar = ab_ref[:, 0:half]
    ai = ab_ref[:, half:S5_FLAT]

    def body(i, carry):
        hr, hi = carry
        hp_scr[pl.ds(i, 1), 0:half] = hr
        hp_scr[pl.ds(i, 1), half:S5_FLAT] = hi
        sr = s_scr[pl.ds(i, 1), 0:half]
        si = s_scr[pl.ds(i, 1), half:S5_FLAT]
        return ar * hr - ai * hi + sr, ar * hi + ai * hr + si

    hr, hi = lax.fori_loop(0, tn, body, (h_scr[:, 0:half], h_scr[:, half:S5_FLAT]), unroll=8)
    h_scr[:, 0:half] = hr
    h_scr[:, half:S5_FLAT] = hi
    y = _s5_in_row(u, krev_ref) + _dot(hp_scr[...].astype(BF16), m3_ref[...])
    for s in range(S5_T):
        for hv in range(2):
            lo = s * S5_WIDTH + hv * LANES
            y_ref[hv, pl.ds(s, tn, stride=S5_T), :] = y[:, lo:lo + LANES]
    hfin_ref[...] = h_scr[...]


def _s5_prompt_call(u_rows, sw, tn):
    n = u_rows.shape[0]
    return pl.pallas_call(
        _s5_prompt_kernel,
        grid=(n // tn,),
        in_specs=[pl.BlockSpec((tn, S5_ROW), lambda i: (i, 0)),
                  _const_spec((S5_ROW, S5_WIDTH)), _const_spec((S5_ROW, S5_FLAT)), _const_spec((S5_FLAT, S5_ROW)),
                  _const_spec((1, S5_FLAT))],
        out_specs=[pl.BlockSpec((2, tn * S5_T, LANES), lambda i: (0, i, 0)),
                   pl.BlockSpec((1, S5_FLAT), lambda i: (0, 0))],
        out_shape=[jax.ShapeDtypeStruct((2, n * S5_T, LANES), F32), jax.ShapeDtypeStruct((1, S5_FLAT), F32)],
        scratch_shapes=[pltpu.VMEM((tn, S5_FLAT), F32), pltpu.VMEM((tn, S5_FLAT), F32),
                        pltpu.VMEM((1, S5_FLAT), F32)],
        compiler_params=_cparams("arbitrary"),
    )(u_rows, sw["krev"], sw["m2"], sw["m3"], sw["ab"])


def _s5_sample_kernel(u_ref, h0_ref, krev_ref, m2_ref, m3_ref, ab_ref, y_ref, hfin_ref, *, n_rows):
    half = S5_FLAT // 2
    ar = ab_ref[:, 0:half]
    ai = ab_ref[:, half:S5_FLAT]
    us = [u_ref[:, c * S5_ROW:(c + 1) * S5_ROW].astype(BF16) for c in range(n_rows)]
    u_all = jnp.concatenate(us, axis=0)
    b = us[0].shape[0]
    s_all = _dot(u_all, m2_ref[...])
    y1_all = _s5_in_row(u_all, krev_ref)
    hr = h0_ref[:, 0:half]
    hi = h0_ref[:, half:S5_FLAT]
    enter = []
    for c in range(n_rows):
        enter.append(jnp.concatenate([hr, hi], axis=-1))
        sr = s_all[c * b:(c + 1) * b, 0:half]
        si = s_all[c * b:(c + 1) * b, half:S5_FLAT]
        hr, hi = ar * hr - ai * hi + sr, ar * hi + ai * hr + si
    y_all = y1_all + _dot(jnp.concatenate(enter, axis=0).astype(BF16), m3_ref[...])
    for c in range(n_rows):
        y_ref[:, c * S5_ROW:(c + 1) * S5_ROW] = y_all[c * b:(c + 1) * b, :]
    hfin_ref[:, 0:half] = hr
    hfin_ref[:, half:S5_FLAT] = hi


def _s5_sample_call(u_seq, h0, sw):
    b, w = u_seq.shape
    n_rows = w // S5_ROW
    full = lambda shape: pl.BlockSpec(shape, lambda i: (0,) * len(shape))
    return pl.pallas_call(
        functools.partial(_s5_sample_kernel, n_rows=n_rows),
        grid=(1,),
        in_specs=[full((b, w)), full((b, S5_FLAT)), _const_spec((S5_ROW, S5_WIDTH)),
                  _const_spec((S5_ROW, S5_FLAT)), _const_spec((S5_FLAT, S5_ROW)), _const_spec((1, S5_FLAT))],
        out_specs=[full((b, w)), full((b, S5_FLAT))],
        out_shape=[jax.ShapeDtypeStruct((b, w), F32), jax.ShapeDtypeStruct((b, S5_FLAT), F32)],
        compiler_params=_cparams("arbitrary"),
    )(u_seq, h0, sw["krev"], sw["m2"], sw["m3"], sw["ab"])


def _hg_gates(hq, hf, lb):
    sig = jax.nn.sigmoid(hf)
    f = lb + (1.0 - lb) * sig
    k = (1.0 - lb) * jax.nn.sigmoid(-hf)
    qf = hq * jax.nn.sigmoid(hq)
    return qf, k, f


def _hg_intra_kernel(hq_ref, hf_ref, hi_ref, lb_ref, ones_ref, o_ref, k_scr, f_scr, *, r_len):
    lb = lb_ref[...]
    ones_bd = ones_ref[...]
    w = HG_WIDTH
    for r in range(r_len):
        _, k, f = _hg_gates(hq_ref[:, r * w:(r + 1) * w], hf_ref[:, r * w:(r + 1) * w], lb)
        k_scr[r] = k
        f_scr[r] = f
    for r in range(r_len):
        hq = hq_ref[:, r * w:(r + 1) * w]
        qp = hq * jax.nn.sigmoid(hq)
        terms = []
        for s in range(r, -1, -1):
            terms.append((qp * k_scr[s]).astype(BF16))
            if s > 0:
                qp = qp * f_scr[s]
        att = _dot(jnp.concatenate(terms, axis=0), ones_bd)
        rows = hq.shape[0]
        acc = None
        for j, s in enumerate(range(r, -1, -1)):
            part = att[j * rows:(j + 1) * rows, :] * hi_ref[:, s * w:(s + 1) * w]
            acc = part if acc is None else acc + part
        o_ref[:, r * w:(r + 1) * w] = acc


def _hg_intra_call(hq, hf, hi, lb, ones_bd, r_len):
    rows = hq.shape[0]
    spec = pl.BlockSpec((HG_ROWS, r_len * HG_WIDTH), lambda i: (i, 0))
    return pl.pallas_call(
        functools.partial(_hg_intra_kernel, r_len=r_len),
        grid=(rows // HG_ROWS,),
        in_specs=[spec, spec, spec, _const_spec((1, HG_WIDTH)), _const_spec((HG_WIDTH, HG_WIDTH))],
        out_specs=spec,
        out_shape=jax.ShapeDtypeStruct(hq.shape, F32),
        scratch_shapes=[pltpu.VMEM((r_len, HG_ROWS, HG_WIDTH), F32), pltpu.VMEM((r_len, HG_ROWS, HG_WIDTH), F32)],
        compiler_params=_cparams("arbitrary"),
    )(hq, hf, hi, lb, ones_bd)


def _hg_inter_kernel(hq_ref, hf_ref, hi_ref, lb_ref, tril_ref, blk_ref, hmask_ref, s0_ref,
                     o_ref, sout_ref, s_scr, *, r_len, carry):
    n_sub = HG_ROWS
    if carry:
        @pl.when(pl.program_id(0) == 0)
        def _():
            s_scr[...] = s0_ref[0]

    qf, k, f = _hg_gates(hq_ref[...], hf_ref[...], lb_ref[...])
    g = jnp.log(f)
    g3 = _split3(g)
    tril = tril_ref[...]
    blk = blk_ref[...]
    bl = _dot(tril, g3[0]) + _dot(tril, g3[1]) + _dot(tril, g3[2])
    bsum = _dot(blk, g3[0]) + _dot(blk, g3[1]) + _dot(blk, g3[2])
    q_in = (qf * jnp.exp(bl)).astype(BF16)
    k_out = (k * jnp.exp(bsum - bl)).astype(BF16)
    v = hi_ref[...].astype(BF16)
    hmask = hmask_ref[...]
    decay = jnp.exp(bsum)
    for j in range(n_sub):
        sl = slice(j * r_len, (j + 1) * r_len)
        s_in = s_scr[...] if carry else s0_ref[j]
        o_ref[sl, :] = _dot_t1(q_in[sl], s_in.astype(BF16))
        w_new = _dot_t0(v[sl], k_out[sl]) * hmask
        s_new = decay[j * r_len:j * r_len + 1, :] * s_in + w_new
        if carry:
            s_scr[...] = s_new
        else:
            sout_ref[j] = s_new
    if carry:
        sout_ref[0] = s_scr[...]


def _hg_inter_call(hq, hf, hi, lb, consts, s0, r_len, carry):
    t = hq.shape[0]
    tt = HG_ROWS * r_len
    spec = pl.BlockSpec((tt, HG_WIDTH), lambda i: (i, 0))
    if carry:
        s_spec = pl.BlockSpec((1, HG_WIDTH, HG_WIDTH), lambda i: (0, 0, 0))
    else:
        s_spec = pl.BlockSpec((HG_ROWS, HG_WIDTH, HG_WIDTH), lambda i: (i, 0, 0))
    return pl.pallas_call(
        functools.partial(_hg_inter_kernel, r_len=r_len, carry=carry),
        grid=(t // tt,),
        in_specs=[spec, spec, spec, _const_spec((1, HG_WIDTH)), _const_spec((tt, tt)), _const_spec((tt, tt)),
                  _const_spec((HG_WIDTH, HG_WIDTH)), s_spec],
        out_specs=[spec, s_spec],
        out_shape=[jax.ShapeDtypeStruct((t, HG_WIDTH), F32), jax.ShapeDtypeStruct(s0.shape, F32)],
        scratch_shapes=[pltpu.VMEM((HG_WIDTH, HG_WIDTH), F32)],
        compiler_params=_cparams("arbitrary"),
    )(hq, hf, hi, lb, consts["tril"], consts["blk"], consts["hmask"], s0)


def _attn_kernel(qp_ref, k_ref, vt_ref, wuv_ref, o_ref, q_scr, s_buf, mb_buf, m_scr, acc_scr,
                 *, tq, kb, prompt, n_blocks, n_keys, q_base):
    i = pl.program_id(0)
    ncol = MLA_HEADS * tq
    for hd in range(MLA_HEADS):
        q_scr[hd * tq:(hd + 1) * tq, :] = qp_ref[:, hd * QK_WIDTH:(hd + 1) * QK_WIDTH]
    m_scr[...] = jnp.full_like(m_scr, NEG_BIG)
    acc_scr[...] = jnp.zeros_like(acc_scr)
    q0 = i * tq if prompt else q_base
    n_full = (i * tq) // kb if prompt else n_blocks - 1

    cw = min(ATTN_COLS, ncol)
    n_chunks = ncol // cw

    def scores(b, c, masked):
        cols = slice(c * cw, (c + 1) * cw)
        s = _dot_t1(k_ref[0, b], q_scr[cols, :])
        if masked:
            kpos = b * kb + lax.broadcasted_iota(jnp.int32, (kb, cw), 0)
            qpos = q0 + ((c * cw + lax.broadcasted_iota(jnp.int32, (kb, cw), 1)) & (tq - 1))
            mask = (kpos >> 6) <= (qpos >> 6)
            if n_keys is not None:
                mask = mask & (kpos < n_keys)
            s = jnp.where(mask, s, NEG_BIG)
        s_buf[c] = s
        mb_buf[c] = jnp.max(s, axis=0, keepdims=True)

    def values(b, c):
        cols = slice(c * cw, (c + 1) * cw)
        m_old = m_scr[:, cols]
        m_new = jnp.maximum(m_old, mb_buf[c])
        alpha = jnp.exp2(m_old - m_new)
        p = jnp.exp2(s_buf[c] - m_new).astype(BF16)
        acc_scr[:, cols] = alpha * acc_scr[:, cols] + _dot(vt_ref[0, b], p)
        m_scr[:, cols] = m_new

    def step(b, masked_next):
        for c in range(n_chunks):
            values(b, c)
            scores(b + 1, c, masked_next)

    for c in range(n_chunks):
        scores(0, c, True)

    def body(b, carry):
        step(b, False)
        return carry

    lax.fori_loop(0, n_full - 1, body, 0)
    if prompt:
        @pl.when(n_full >= 1)
        def _():
            step(n_full - 1, True)
    elif n_full >= 1:
        step(n_full - 1, True)
    for c in range(n_chunks):
        values(n_full, c)

    o_lat_t = acc_scr[0:MLA_KV_RANK, :] / acc_scr[MLA_KV_RANK:MLA_KV_RANK + 1, :]
    o_lat = o_lat_t.T.astype(BF16)
    full = _dot(o_lat, wuv_ref[...])
    lane_head = lax.broadcasted_iota(jnp.int32, (tq, MLA_WIDTH), 1) // MLA_V
    out = jnp.zeros((tq, MLA_WIDTH), F32)
    for hd in range(MLA_HEADS):
        out = out + jnp.where(lane_head == hd, full[hd * tq:(hd + 1) * tq, :], 0.0)
    o_ref[...] = out


def _attn_call(qp, k_all, vt_all, wuv, tq, prompt, n_keys, q_base):
    t = qp.shape[0]
    n_blocks, kb = k_all.shape[1], k_all.shape[2]
    ncol = MLA_HEADS * tq
    cw = min(ATTN_COLS, ncol)
    if prompt:
        k_spec = pl.BlockSpec((1, n_blocks, kb, QK_WIDTH), lambda i: (0, 0, 0, 0), pipeline_mode=pl.Buffered(1))
        v_spec = pl.BlockSpec((1, n_blocks, VT_ROWS, kb), lambda i: (0, 0, 0, 0), pipeline_mode=pl.Buffered(1))
    else:
        k_spec = pl.BlockSpec((1, n_blocks, kb, QK_WIDTH), lambda i: (i, 0, 0, 0))
        v_spec = pl.BlockSpec((1, n_blocks, VT_ROWS, kb), lambda i: (i, 0, 0, 0))
    return pl.pallas_call(
        functools.partial(_attn_kernel, tq=tq, kb=kb, prompt=prompt, n_blocks=n_blocks, n_keys=n_keys,
                          q_base=q_base),
        grid=(t // tq,),
        in_specs=[pl.BlockSpec((tq, MLA_HEADS * QK_WIDTH), lambda i: (i, 0)), k_spec, v_spec,
                  _const_spec((MLA_KV_RANK, MLA_WIDTH))],
        out_specs=pl.BlockSpec((tq, MLA_WIDTH), lambda i: (i, 0)),
        out_shape=jax.ShapeDtypeStruct((t, MLA_WIDTH), F32),
        scratch_shapes=[pltpu.VMEM((ncol, QK_WIDTH), BF16), pltpu.VMEM((ncol // cw, kb, cw), F32),
                        pltpu.VMEM((ncol // cw, 1, cw), F32), pltpu.VMEM((1, ncol), F32),
                        pltpu.VMEM((VT_ROWS, ncol), F32)],
        compiler_params=_cparams("arbitrary"),
    )(qp, k_all, vt_all, wuv)


def _out_kernel(x_ref, ys_ref, u_ref, oa_ref, ob_ref, hg_ref, mla_ref, d_ref, wglu_ref, bglu_ref, og_ref,
                wout_ref, g2_ref, wup_ref, wdn_ref, fg_ref, o_ref, *, final):
    y = jnp.concatenate([ys_ref[0], ys_ref[1]], axis=-1) + d_ref[...] * u_ref[...]
    z = jax.nn.gelu(y, approximate=True)
    s5 = z * jax.nn.sigmoid(_dot(z.astype(BF16), wglu_ref[...]) + bglu_ref[...])
    og = og_ref[...]
    hgate = hg_ref[...]
    mixed = jnp.concatenate([
        _rms(s5, og[:, 0:256]),
        _rms(oa_ref[...] + ob_ref[...], og[:, 256:512]) * (hgate * jax.nn.sigmoid(hgate)),
        _rms(mla_ref[...], og[:, 512:1024]),
    ], axis=-1)
    x1 = x_ref[...] + _dot(mixed.astype(BF16), wout_ref[...])
    h2 = _rms(x1, g2_ref[...]).astype(BF16)
    acc = x1
    for c in range(D_FF // FF_CHUNK):
        up = _dot(h2, wup_ref[:, c * FF_CHUNK:(c + 1) * FF_CHUNK])
        act = jnp.square(jnp.maximum(up, 0.0)).astype(BF16)
        acc = acc + _dot(act, wdn_ref[c * FF_CHUNK:(c + 1) * FF_CHUNK, :])
    if final:
        acc = _rms(acc, fg_ref[...])
    o_ref[...] = acc


def _out_call(x, ys, u, oa, ob, hg, mla, lw, final_g, tm, final):
    t = x.shape[0]
    row = lambda w: pl.BlockSpec((tm, w), lambda i: (i, 0))
    return pl.pallas_call(
        functools.partial(_out_kernel, final=final),
        grid=(t // tm,),
        in_specs=[row(D_MODEL), pl.BlockSpec((2, tm, LANES), lambda i: (0, i, 0)), row(256), row(256), row(256),
                  row(256), row(MLA_WIDTH),
                  _const_spec((1, 256)), _const_spec((256, 256)), _const_spec((1, 256)), _const_spec((1, D_MODEL)),
                  _const_spec((D_MODEL, D_MODEL)), _const_spec((1, D_MODEL)), _const_spec((D_MODEL, D_FF)),
                  _const_spec((D_FF, D_MODEL)), _const_spec((1, D_MODEL))],
        out_specs=row(D_MODEL),
        out_shape=jax.ShapeDtypeStruct((t, D_MODEL), F32),
        compiler_params=_cparams("arbitrary"),
    )(x, ys, u, oa, ob, hg, mla, lw["s5_d"], lw["s5_w_glu"], lw["s5_b_glu"], lw["out_norm_g"], lw["w_out"],
      lw["norm2_g"], lw["w_up"], lw["w_down"], final_g)


def _rot_cols(w):
    half = MLA_ROPE // 2
    return jnp.concatenate([-w[..., half:], w[..., :half]], axis=-1)


def _prep_layer(w_in, w_uq, w_uk, w_uv):
    kpe = w_in[:, 1664:1696]
    pad = jnp.zeros((D_MODEL, 96), F32)
    w_in_p = jnp.concatenate([w_in[:, :1664], kpe, pad, _rot_cols(kpe), pad], axis=1).astype(BF16)
    uq = w_uq.reshape(MLA_Q_RANK, MLA_HEADS, MLA_NOPE + MLA_ROPE)
    nope = uq[:, :, :MLA_NOPE].reshape(MLA_Q_RANK, 512)
    pe = uq[:, :, MLA_NOPE:]
    w_uq_p = jnp.concatenate([nope, pe.reshape(MLA_Q_RANK, 256), _rot_cols(pe).reshape(MLA_Q_RANK, 256)],
                             axis=1).astype(BF16)
    eye_h = jnp.eye(MLA_HEADS, dtype=F32)
    wuk_bd = jnp.einsum("chd,hk->hdkc", w_uk, eye_h).reshape(512, 1024).astype(BF16)
    place = jnp.einsum("hk,rc->hrkc", eye_h, jnp.eye(MLA_ROPE, LANES, dtype=F32)).reshape(256, 1024).astype(BF16)
    return w_in_p, w_uq_p, wuk_bd, place, w_uv.reshape(MLA_KV_RANK, MLA_WIDTH).astype(BF16)


def _prep_s5(lam_re, lam_im, log_dt, b_re, b_im, c_re, c_im):
    hp = lax.Precision.HIGHEST
    t = S5_T
    dt = jnp.exp(log_dt)[:, None]
    mag1 = jnp.exp(lam_re * dt)
    a_re, a_im = mag1 * jnp.cos(lam_im * dt), mag1 * jnp.sin(lam_im * dt)
    pw_re, pw_im = [jnp.ones_like(a_re)], [jnp.zeros_like(a_im)]
    for _ in range(t):
        pr, pi = pw_re[-1], pw_im[-1]
        pw_re.append(pr * a_re - pi * a_im)
        pw_im.append(pr * a_im + pi * a_re)
    p_re, p_im = jnp.stack(pw_re), jnp.stack(pw_im)
    den = lam_re * lam_re + lam_im * lam_im
    i_re, i_im = lam_re / den, -lam_im / den
    z_re = (a_re - 1.0) * i_re - a_im * i_im
    z_im = (a_re - 1.0) * i_im + a_im * i_re
    bb_re = z_re[..., None] * b_re - z_im[..., None] * b_im
    bb_im = z_re[..., None] * b_im + z_im[..., None] * b_re
    cp_re = c_re[None] * p_re[:, :, None, :] - c_im[None] * p_im[:, :, None, :]
    cp_im = c_re[None] * p_im[:, :, None, :] + c_im[None] * p_re[:, :, None, :]
    kern = (jnp.einsum("tgap,gph->tgah", cp_re[:t], bb_re, precision=hp)
            - jnp.einsum("tgap,gph->tgah", cp_im[:t], bb_im, precision=hp))
    eye_g = jnp.eye(S5_GROUPS, dtype=F32)
    kfull = jnp.einsum("tgab,gk->tgbka", kern, eye_g).reshape(t, S5_WIDTH, S5_WIDTH)
    krev = jnp.concatenate([kfull[t - 1 - j] for j in range(t)], axis=0)
    rev_re, rev_im = p_re[t - 1::-1][:t], p_im[t - 1::-1][:t]
    w2_re = rev_re[..., None] * bb_re[None] - rev_im[..., None] * bb_im[None]
    w2_im = rev_re[..., None] * bb_im[None] + rev_im[..., None] * bb_re[None]
    m2 = jnp.einsum("rsgph,gk->sghrkp", jnp.stack([w2_re, w2_im]), eye_g).reshape(S5_ROW, S5_FLAT)
    m3 = jnp.einsum("rtgap,gk->rgptka", jnp.stack([cp_re[1:], -cp_im[1:]]), eye_g).reshape(S5_FLAT, S5_ROW)
    ab = jnp.concatenate([p_re[t].reshape(1, -1), p_im[t].reshape(1, -1)], axis=1)
    return {"krev": krev.astype(BF16), "m2": m2.astype(BF16), "m3": m3.astype(BF16), "ab": ab}


def _hg_consts(r_len):
    tt = HG_ROWS * r_len
    r = jnp.arange(tt)
    same = (r[:, None] // r_len) == (r[None, :] // r_len)
    tril = (same & (r[None, :] <= r[:, None])).astype(BF16)
    hd = jnp.arange(HG_WIDTH) // HG_DK
    hmask = (hd[:, None] == hd[None, :]).astype(F32)
    return {"tril": tril, "blk": same.astype(BF16), "hmask": hmask}


def _rope_tables(pos):
    half = MLA_ROPE // 2
    inv = ROPE_THETA ** (-jnp.arange(half, dtype=F32) / half)
    ang = pos.astype(F32)[:, None] * inv[None, :]
    reps = LANES // half
    return jnp.tile(jnp.cos(ang), (1, reps)), jnp.tile(jnp.sin(ang), (1, reps))


def _state_to_bd(s):
    eye_h = jnp.eye(HG_HEADS, dtype=F32)
    return jnp.einsum("bhdv,hk->bhvkd", s, eye_h).reshape(s.shape[0], HG_WIDTH, HG_WIDTH)


def _state_from_bd(s):
    b = s.shape[0]
    s5 = s.reshape(b, HG_HEADS, HG_DV, HG_HEADS, HG_DK)
    return jnp.stack([s5[:, h, :, h, :] for h in range(HG_HEADS)], axis=1).swapaxes(-1, -2)


def _layer(x, lw, sw, cos, sin, lb, hgc, ones_bd, final_g, final, *, prompt, n_seq, s5_h0, hg_s0, kv_past, pe_past):
    t = x.shape[0]
    seq = t // n_seq
    tm = KEY_BLOCK_PROMPT
    u, u_rows, hq, hf, hi, hg, c_kv, k_pe, qp, kk, vt = _in_call(x, lw, cos, sin, tm)

    if prompt:
        ys, s5_fin = _s5_prompt_call(u_rows, sw, min(256, t // S5_T))
    else:
        ys, s5_fin = _s5_sample_call(u_rows.reshape(n_seq, seq * S5_WIDTH), s5_h0, sw)
        ys = ys.reshape(t, 2, LANES).swapaxes(0, 1)

    r_len = 32 if prompt else seq
    av = lambda a: a.reshape(t // r_len, r_len * HG_WIDTH)
    oa = _hg_intra_call(av(hq), av(hf), av(hi), lb, ones_bd, r_len).reshape(t, HG_WIDTH)
    ob, hg_fin = _hg_inter_call(hq, hf, hi, lb, hgc, hg_s0, r_len, carry=prompt)

    if prompt:
        k_all = kk.reshape(1, t // tm, tm, QK_WIDTH)
        mla = _attn_call(qp, k_all, vt.reshape(1, t // tm, VT_ROWS, tm), lw["w_uv"], ATTN_QUERIES, True, None, 0)
    else:
        past = kv_past.shape[1]
        n_keys = past + seq
        n_pad = -n_keys % LANES
        kb = n_keys + n_pad
        n_blocks = 1
        k_past = jnp.concatenate([kv_past, pe_past, jnp.zeros((n_seq, past, QK_WIDTH - 160), F32)], axis=-1)
        k_all = jnp.concatenate([k_past.astype(BF16), kk.reshape(n_seq, seq, QK_WIDTH),
                                 jnp.zeros((n_seq, n_pad, QK_WIDTH), BF16)], axis=1)
        k_all = k_all.reshape(n_seq, n_blocks, kb, QK_WIDTH)
        v_all = jnp.concatenate([kv_past, c_kv.reshape(n_seq, seq, MLA_KV_RANK),
                                 jnp.zeros((n_seq, n_pad, MLA_KV_RANK), F32)], axis=1).astype(BF16)
        vt_all = v_all.reshape(n_seq, n_blocks, kb, MLA_KV_RANK).swapaxes(2, 3)
        ones_row = jnp.zeros((n_seq, n_blocks, VT_ROWS - MLA_KV_RANK, kb), BF16).at[:, :, 0, :].set(1.0)
        vt_all = jnp.concatenate([vt_all, ones_row], axis=2)
        mla = _attn_call(qp, k_all, vt_all, lw["w_uv"], seq, False, n_keys, past)

    x_new = _out_call(x, ys, u, oa, ob, hg, mla, lw, final_g, tm, final)
    return x_new, c_kv, k_pe, hg_fin, s5_fin


def kernel(x_prompt, x_sample, cache_mla_kv, cache_mla_pe, state_hgrn, state_s5_re, state_s5_im, norm1_g, w_in, s5_lambda_re, s5_lambda_im, s5_log_dt, s5_b_re, s5_b_im, s5_c_re, s5_c_im, s5_d, s5_w_glu, s5_b_glu, hgrn_lb_logits, mla_q_norm_g, mla_w_uq, mla_kv_norm_g, mla_w_uk, mla_w_uv, out_norm_g, w_out, norm2_g, w_up, w_down, final_norm_g):
    depth = w_in.shape[0]
    bp, lp = x_prompt.shape[0], x_prompt.shape[1]
    bs, ls = x_sample.shape[0], x_sample.shape[1]
    past = cache_mla_kv.shape[2]
    assert bp == 1 and ls == 2 * S5_T and bs % HG_ROWS == 0

    cos_p, sin_p = _rope_tables(jnp.arange(lp, dtype=jnp.int32))
    cos_s, sin_s = _rope_tables(past + jnp.arange(ls, dtype=jnp.int32))
    cos_s, sin_s = jnp.tile(cos_s, (bs, 1)), jnp.tile(sin_s, (bs, 1))

    lb_p = jax.nn.softmax(hgrn_lb_logits.astype(F32), axis=0)
    lb_all = jnp.cumsum(lb_p, axis=0) - lb_p[0]
    hgc_p, hgc_s = _hg_consts(32), _hg_consts(ls)
    hd = jnp.arange(HG_WIDTH) // HG_DK
    ones_bd = (hd[:, None] == hd[None, :]).astype(BF16)
    row = lambda v: v.reshape(1, -1).astype(F32)
    final_g = row(final_norm_g)

    xp = x_prompt.reshape(bp * lp, D_MODEL)
    xs = x_sample.reshape(bs * ls, D_MODEL)
    outs_p, outs_s = [], []
    for l in range(depth):
        w_in_p, w_uq_p, wuk_bd, place, w_uv_p = _prep_layer(w_in[l], mla_w_uq[l], mla_w_uk[l], mla_w_uv[l])
        lw = {
            "norm1_g": row(norm1_g[l]), "w_in": w_in_p, "q_norm_g": row(mla_q_norm_g[l]), "w_uq": w_uq_p,
            "w_uk": wuk_bd, "place": place, "kv_norm_g": row(mla_kv_norm_g[l]), "w_uv": w_uv_p,
            "s5_d": row(s5_d[l]), "s5_w_glu": s5_w_glu[l].astype(BF16), "s5_b_glu": row(s5_b_glu[l]),
            "out_norm_g": row(out_norm_g[l]), "w_out": w_out[l].astype(BF16), "norm2_g": row(norm2_g[l]),
            "w_up": w_up[l].astype(BF16), "w_down": w_down[l].astype(BF16),
        }
        sw = _prep_s5(s5_lambda_re[l], s5_lambda_im[l], s5_log_dt[l], s5_b_re[l], s5_b_im[l],
                      s5_c_re[l], s5_c_im[l])
        lb = row(lb_all[l])
        final = l == depth - 1
        xp, a, b, c, d = _layer(xp, lw, sw, cos_p, sin_p, lb, hgc_p, ones_bd, final_g, final, prompt=True,
                                n_seq=1, s5_h0=None, hg_s0=jnp.zeros((1, HG_WIDTH, HG_WIDTH), F32),
                                kv_past=None, pe_past=None)
        outs_p.append((a, b, c, d))
        h0 = jnp.concatenate([state_s5_re[l].reshape(bs, -1), state_s5_im[l].reshape(bs, -1)], axis=1)
        xs, a, b, c, d = _layer(xs, lw, sw, cos_s, sin_s, lb, hgc_s, ones_bd, final_g, final, prompt=False,
                                n_seq=bs, s5_h0=h0, hg_s0=_state_to_bd(state_hgrn[l]),
                                kv_past=cache_mla_kv[l], pe_past=cache_mla_pe[l])
        outs_s.append((a, b, c, d))

    def gather(outs, nb, sl):
        kv = jnp.stack([o[0].reshape(nb, sl, MLA_KV_RANK) for o in outs])
        pe = jnp.stack([o[1].reshape(nb, sl, MLA_ROPE) for o in outs])
        hg = jnp.stack([_state_from_bd(o[2]) for o in outs])
        half = S5_FLAT // 2
        re = jnp.stack([o[3][:, :half].reshape(nb, S5_GROUPS, S5_STATE) for o in outs])
        im = jnp.stack([o[3][:, half:].reshape(nb, S5_GROUPS, S5_STATE) for o in outs])
        return kv, pe, hg, re, im

    p_kv, p_pe, p_hg, p_re, p_im = gather(outs_p, bp, lp)
    s_kv, s_pe, s_hg, s_re, s_im = gather(outs_s, bs, ls)
    return (xp.reshape(bp, lp, D_MODEL), xs.reshape(bs, ls, D_MODEL),
            p_kv, p_pe, p_hg, p_re, p_im, s_kv, s_pe, s_hg, s_re, s_im)
```

```python
import functools
import math

import jax
import jax.numpy as jnp
from jax import lax
from jax.experimental import pallas as pl
from jax.experimental.pallas import tpu as pltpu

F32 = jnp.float32
BF16 = jnp.bfloat16

D_MODEL = 1024
CHUNK = 64
CHUNK_SHIFT = CHUNK.bit_length() - 1
EPS = 1e-5
NEG_BIG = -1e30

S5_WIDTH = 256
S5_GROUP = 16
S5_GROUPS = 16
S5_STATE = 64
S5_T = 8
S5_ROW = S5_T * S5_WIDTH
S5_FLAT = 2 * S5_GROUPS * S5_STATE

HG_HEADS = 4
HG_DK = 64
HG_DV = 64
HG_WIDTH = 256
HG_ROWS = 16

MLA_HEADS = 8
MLA_Q_RANK = 256
MLA_KV_RANK = 128
MLA_NOPE = 64
MLA_ROPE = 32
MLA_V = 64
MLA_WIDTH = 512
ROPE_THETA = 10000.0
KEY_BLOCK_PROMPT = 512
QK_WIDTH = 256
LOG2E = 1.4426950408889634
ATTN_COLS = 512
ATTN_QUERIES = 512

D_FF = 4096
FF_CHUNK = 1024
IN_PAD = 1920

LANES = 128
VMEM_LIMIT = 56 * 1024 * 1024


def _cparams(*sem):
    return pltpu.CompilerParams(dimension_semantics=sem, vmem_limit_bytes=VMEM_LIMIT)


def _const_spec(shape):
    nd = len(shape)
    return pl.BlockSpec(shape, lambda *_: (0,) * nd, pipeline_mode=pl.Buffered(1))


def _rms(x, g):
    y = x * lax.rsqrt(jnp.mean(x * x, axis=-1, keepdims=True) + EPS)
    return y * g


def _dot(a, b):
    return jnp.dot(a, b, preferred_element_type=F32)


def _dot_t0(a, b):
    return lax.dot_general(a, b, (((0,), (0,)), ((), ())), preferred_element_type=F32)


def _dot_t1(a, b):
    return lax.dot_general(a, b, (((1,), (1,)), ((), ())), preferred_element_type=F32)


def _split3(x):
    hi = x.astype(BF16)
    r1 = x - hi.astype(F32)
    mid = r1.astype(BF16)
    lo = (r1 - mid.astype(F32)).astype(BF16)
    return hi, mid, lo


def _in_kernel(x_ref, g1_ref, win_ref, qg_ref, wuq_ref, wuk_ref, place_ref, kvg_ref, cos_ref, sin_ref,
               u_ref, urow_ref, hq_ref, hf_ref, hi_ref, hg_ref, ckv_ref, kpe_ref, qp_ref, kk_ref, vt_ref, u_scr):
    h = _rms(x_ref[...], g1_ref[...])
    proj = _dot(h.astype(BF16), win_ref[...])
    u_ref[...] = proj[:, 0:256]
    n_row = proj.shape[0] // S5_T
    for hv in range(2):
        u_scr[hv] = proj[:, hv * LANES:(hv + 1) * LANES]
    for s in range(S5_T):
        for hv in range(2):
            lo = s * S5_WIDTH + hv * LANES
            urow_ref[:, lo:lo + LANES] = u_scr[hv, pl.ds(s, n_row, stride=S5_T), :]
    hq_ref[...] = proj[:, 256:512]
    hf_ref[...] = proj[:, 512:768]
    hi_ref[...] = proj[:, 768:1024]
    hg_ref[...] = proj[:, 1024:1280]

    cos = cos_ref[...]
    sin = sin_ref[...]
    cos2 = jnp.concatenate([cos, cos], axis=-1)
    sin2 = jnp.concatenate([sin, sin], axis=-1)

    cqn = _rms(proj[:, 1280:1536], qg_ref[...])
    q = _dot(cqn.astype(BF16), wuq_ref[...])
    scale = (MLA_NOPE + MLA_ROPE) ** -0.5 * LOG2E
    q_pe = (q[:, 512:768] * cos2 + q[:, 768:1024] * sin2) * scale
    q_lat = _dot(q[:, 0:512].astype(BF16), wuk_ref[...]) * scale
    pe_pl = _dot(q_pe.astype(BF16), place_ref[...])
    for hd in range(MLA_HEADS):
        qp_ref[:, hd * 256:hd * 256 + 128] = q_lat[:, hd * 128:(hd + 1) * 128].astype(BF16)
        qp_ref[:, hd * 256 + 128:(hd + 1) * 256] = pe_pl[:, hd * 128:(hd + 1) * 128].astype(BF16)

    c_kv = _rms(proj[:, 1536:1664], kvg_ref[...])
    ckv_ref[...] = c_kv
    kpe = proj[:, 1664:1792] * cos + proj[:, 1792:1920] * sin
    kpe_ref[...] = kpe[:, 0:MLA_ROPE]
    kk_ref[...] = jnp.concatenate([c_kv, kpe], axis=-1).astype(BF16)
    vt_ref[0] = c_kv.T.astype(BF16)


def _in_call(x, lw, cos, sin, tm):
    t = x.shape[0]
    row = lambda w: pl.BlockSpec((tm, w), lambda i: (i, 0))
    outs = [jax.ShapeDtypeStruct((t, 256), F32), jax.ShapeDtypeStruct((t // S5_T, S5_ROW), F32)] + [
        jax.ShapeDtypeStruct((t, 256), F32)] * 4 + [
        jax.ShapeDtypeStruct((t, MLA_KV_RANK), F32),
        jax.ShapeDtypeStruct((t, MLA_ROPE), F32),
        jax.ShapeDtypeStruct((t, MLA_HEADS * QK_WIDTH), BF16),
        jax.ShapeDtypeStruct((t, QK_WIDTH), BF16),
        jax.ShapeDtypeStruct((t // tm, MLA_KV_RANK, tm), BF16),
    ]
    out_specs = [row(256), pl.BlockSpec((tm // S5_T, S5_ROW), lambda i: (i, 0))] + [row(256)] * 4 + [
        row(MLA_KV_RANK), row(MLA_ROPE), row(MLA_HEADS * QK_WIDTH), row(QK_WIDTH),
        pl.BlockSpec((1, MLA_KV_RANK, tm), lambda i: (i, 0, 0))]
    return pl.pallas_call(
        _in_kernel,
        grid=(t // tm,),
        in_specs=[row(D_MODEL), _const_spec((1, D_MODEL)), _const_spec((D_MODEL, IN_PAD)),
                  _const_spec((1, MLA_Q_RANK)), _const_spec((MLA_Q_RANK, 1024)), _const_spec((512, 1024)),
                  _const_spec((256, 1024)), _const_spec((1, MLA_KV_RANK)), row(LANES), row(LANES)],
        out_specs=out_specs,
        out_shape=outs,
        scratch_shapes=[pltpu.VMEM((2, tm, LANES), F32)],
        compiler_params=_cparams("arbitrary"),
    )(x, lw["norm1_g"], lw["w_in"], lw["q_norm_g"], lw["w_uq"], lw["w_uk"], lw["place"], lw["kv_norm_g"],
      cos, sin)


def _s5_in_row(u, krev_ref):
    w = S5_WIDTH
    parts = [_dot(u[:, 0:(t + 1) * w], krev_ref[(S5_T - 1 - t) * w:S5_T * w, :]) for t in range(S5_T)]
    return jnp.concatenate(parts, axis=1)


def _s5_prompt_kernel(u_ref, krev_ref, m2_ref, m3_ref, ab_ref, y_ref, hfin_ref, s_scr, hp_scr, h_scr):
    half = S5_FLAT // 2
    tn = u_ref.shape[0]

    @pl.when(pl.program_id(0) == 0)
    def _():
        h_scr[...] = jnp.zeros_like(h_scr)

    u = u_ref[...].astype(BF16)
    s_scr[...] = _dot(u, m2_ref[...])
    ar = ab_ref[:, 0:half]
    ai = ab_ref[:, half:S5_FLAT]

    def body(i, carry):
        hr, hi = carry
        hp_scr[pl.ds(i, 1), 0:half] = hr
        hp_scr[pl.ds(i, 1), half:S5_FLAT] = hi
        sr = s_scr[pl.ds(i, 1), 0:half]
        si = s_scr[pl.ds(i, 1), half:S5_FLAT]
        return ar * hr - ai * hi + sr, ar * hi + ai * hr + si

    hr, hi = lax.fori_loop(0, tn, body, (h_scr[:, 0:half], h_scr[:, half:S5_FLAT]), unroll=8)
    h_scr[:, 0:half] = hr
    h_scr[:, half:S5_FLAT] = hi
    y = _s5_in_row(u, krev_ref) + _dot(hp_scr[...].astype(BF16), m3_ref[...])
    for s in range(S5_T):
        for hv in range(2):
            lo = s * S5_WIDTH + hv * LANES
            y_ref[hv, pl.ds(s, tn, stride=S5_T), :] = y[:, lo:lo + LANES]
    hfin_ref[...] = h_scr[...]


def _s5_prompt_call(u_rows, sw, tn):
    n = u_rows.shape[0]
    return pl.pallas_call(
        _s5_prompt_kernel,
        grid=(n // tn,),
        in_specs=[pl.BlockSpec((tn, S5_ROW), lambda i: (i, 0)),
                  _const_spec((S5_ROW, S5_WIDTH)), _const_spec((S5_ROW, S5_FLAT)), _const_spec((S5_FLAT, S5_ROW)),
                  _const_spec((1, S5_FLAT))],
        out_specs=[pl.BlockSpec((2, tn * S5_T, LANES), lambda i: (0, i, 0)),
                   pl.BlockSpec((1, S5_FLAT), lambda i: (0, 0))],
        out_shape=[jax.ShapeDtypeStruct((2, n * S5_T, LANES), F32), jax.ShapeDtypeStruct((1, S5_FLAT), F32)],
        scratch_shapes=[pltpu.VMEM((tn, S5_FLAT), F32), pltpu.VMEM((tn, S5_FLAT), F32),
                        pltpu.VMEM((1, S5_FLAT), F32)],
        compiler_params=_cparams("arbitrary"),
    )(u_rows, sw["krev"], sw["m2"], sw["m3"], sw["ab"])


def _s5_sample_kernel(u_ref, h0_ref, krev_ref, m2_ref, m3_ref, ab_ref, y_ref, hfin_ref, *, n_rows):
    half = S5_FLAT // 2
    ar = ab_ref[:, 0:half]
    ai = ab_ref[:, half:S5_FLAT]
    us = [u_ref[:, c * S5_ROW:(c + 1) * S5_ROW].astype(BF16) for c in range(n_rows)]
    u_all = jnp.concatenate(us, axis=0)
    b = us[0].shape[0]
    s_all = _dot(u_all, m2_ref[...])
    y1_all = _s5_in_row(u_all, krev_ref)
    hr = h0_ref[:, 0:half]
    hi = h0_ref[:, half:S5_FLAT]
    enter = []
    for c in range(n_rows):
        enter.append(jnp.concatenate([hr, hi], axis=-1))
        sr = s_all[c * b:(c + 1) * b, 0:half]
        si = s_all[c * b:(c + 1) * b, half:S5_FLAT]
        hr, hi = ar * hr - ai * hi + sr, ar * hi + ai * hr + si
    y_all = y1_all + _dot(jnp.concatenate(enter, axis=0).astype(BF16), m3_ref[...])
    for c in range(n_rows):
        y_ref[:, c * S5_ROW:(c + 1) * S5_ROW] = y_all[c * b:(c + 1) * b, :]
    hfin_ref[:, 0:half] = hr
    hfin_ref[:, half:S5_FLAT] = hi


def _s5_sample_call(u_seq, h0, sw):
    b, w = u_seq.shape
    n_rows = w // S5_ROW
    full = lambda shape: pl.BlockSpec(shape, lambda i: (0,) * len(shape))
    return pl.pallas_call(
        functools.partial(_s5_sample_kernel, n_rows=n_rows),
        grid=(1,),
        in_specs=[full((b, w)), full((b, S5_FLAT)), _const_spec((S5_ROW, S5_WIDTH)),
                  _const_spec((S5_ROW, S5_FLAT)), _const_spec((S5_FLAT, S5_ROW)), _const_spec((1, S5_FLAT))],
        out_specs=[full((b, w)), full((b, S5_FLAT))],
        out_shape=[jax.ShapeDtypeStruct((b, w), F32), jax.ShapeDtypeStruct((b, S5_FLAT), F32)],
        compiler_params=_cparams("arbitrary"),
    )(u_seq, h0, sw["krev"], sw["m2"], sw["m3"], sw["ab"])


def _hg_gates(hq, hf, lb):
    sig = jax.nn.sigmoid(hf)
    f = lb + (1.0 - lb) * sig
    k = (1.0 - lb) * jax.nn.sigmoid(-hf)
    qf = hq * jax.nn.sigmoid(hq)
    return qf, k, f


def _hg_intra_kernel(hq_ref, hf_ref, hi_ref, lb_ref, ones_ref, o_ref, k_scr, f_scr, *, r_len):
    lb = lb_ref[...]
    ones_bd = ones_ref[...]
    w = HG_WIDTH
    for r in range(r_len):
        _, k, f = _hg_gates(hq_ref[:, r * w:(r + 1) * w], hf_ref[:, r * w:(r + 1) * w], lb)
        k_scr[r] = k
        f_scr[r] = f
    for r in range(r_len):
        hq = hq_ref[:, r * w:(r + 1) * w]
        qp = hq * jax.nn.sigmoid(hq)
        terms = []
        for s in range(r, -1, -1):
            terms.append((qp * k_scr[s]).astype(BF16))
            if s > 0:
                qp = qp * f_scr[s]
        att = _dot(jnp.concatenate(terms, axis=0), ones_bd)
        rows = hq.shape[0]
        acc = None
        for j, s in enumerate(range(r, -1, -1)):
            part = att[j * rows:(j + 1) * rows, :] * hi_ref[:, s * w:(s + 1) * w]
            acc = part if acc is None else acc + part
        o_ref[:, r * w:(r + 1) * w] = acc


def _hg_intra_call(hq, hf, hi, lb, ones_bd, r_len):
    rows = hq.shape[0]
    spec = pl.BlockSpec((HG_ROWS, r_len * HG_WIDTH), lambda i: (i, 0))
    return pl.pallas_call(
        functools.partial(_hg_intra_kernel, r_len=r_len),
        grid=(rows // HG_ROWS,),
        in_specs=[spec, spec, spec, _const_spec((1, HG_WIDTH)), _const_spec((HG_WIDTH, HG_WIDTH))],
        out_specs=spec,
        out_shape=jax.ShapeDtypeStruct(hq.shape, F32),
        scratch_shapes=[pltpu.VMEM((r_len, HG_ROWS, HG_WIDTH), F32), pltpu.VMEM((r_len, HG_ROWS, HG_WIDTH), F32)],
        compiler_params=_cparams("arbitrary"),
    )(hq, hf, hi, lb, ones_bd)


def _hg_inter_kernel(hq_ref, hf_ref, hi_ref, lb_ref, tril_ref, blk_ref, hmask_ref, s0_ref,
                     o_ref, sout_ref, s_scr, *, r_len, carry):
    n_sub = HG_ROWS
    if carry:
        @pl.when(pl.program_id(0) == 0)
        def _():
            s_scr[...] = s0_ref[0]

    qf, k, f = _hg_gates(hq_ref[...], hf_ref[...], lb_ref[...])
    g = jnp.log(f)
    g3 = _split3(g)
    tril = tril_ref[...]
    blk = blk_ref[...]
    bl = _dot(tril, g3[0]) + _dot(tril, g3[1]) + _dot(tril, g3[2])
    bsum = _dot(blk, g3[0]) + _dot(blk, g3[1]) + _dot(blk, g3[2])
    q_in = (qf * jnp.exp(bl)).astype(BF16)
    k_out = (k * jnp.exp(bsum - bl)).astype(BF16)
    v = hi_ref[...].astype(BF16)
    hmask = hmask_ref[...]
    decay = jnp.exp(bsum)
    for j in range(n_sub):
        sl = slice(j * r_len, (j + 1) * r_len)
        s_in = s_scr[...] if carry else s0_ref[j]
        o_ref[sl, :] = _dot_t1(q_in[sl], s_in.astype(BF16))
        w_new = _dot_t0(v[sl], k_out[sl]) * hmask
        s_new = decay[j * r_len:j * r_len + 1, :] * s_in + w_new
        if carry:
            s_scr[...] = s_new
        else:
            sout_ref[j] = s_new
    if carry:
        sout_ref[0] = s_scr[...]


def _hg_inter_call(hq, hf, hi, lb, consts, s0, r_len, carry):
    t = hq.shape[0]
    tt = HG_ROWS * r_len
    spec = pl.BlockSpec((tt, HG_WIDTH), lambda i: (i, 0))
    if carry:
        s_spec = pl.BlockSpec((1, HG_WIDTH, HG_WIDTH), lambda i: (0, 0, 0))
    else:
        s_spec = pl.BlockSpec((HG_ROWS, HG_WIDTH, HG_WIDTH), lambda i: (i, 0, 0))
    return pl.pallas_call(
        functools.partial(_hg_inter_kernel, r_len=r_len, carry=carry),
        grid=(t // tt,),
        in_specs=[spec, spec, spec, _const_spec((1, HG_WIDTH)), _const_spec((tt, tt)), _const_spec((tt, tt)),
                  _const_spec((HG_WIDTH, HG_WIDTH)), s_spec],
        out_specs=[spec, s_spec],
        out_shape=[jax.ShapeDtypeStruct((t, HG_WIDTH), F32), jax.ShapeDtypeStruct(s0.shape, F32)],
        scratch_shapes=[pltpu.VMEM((HG_WIDTH, HG_WIDTH), F32)],
        compiler_params=_cparams("arbitrary"),
    )(hq, hf, hi, lb, consts["tril"], consts["blk"], consts["hmask"], s0)


def _attn_project(o_lat, wuv_ref, tq):
    full = _dot(o_lat, wuv_ref[...])
    lane_head = lax.broadcasted_iota(jnp.int32, (tq, MLA_WIDTH), 1) // MLA_V
    out = jnp.zeros((tq, MLA_WIDTH), F32)
    for hd in range(MLA_HEADS):
        out = out + jnp.where(lane_head == hd, full[hd * tq:(hd + 1) * tq, :], 0.0)
    return out


def _attn_kernel(qp_ref, k_ref, vt_ref, wuv_ref, o_ref, q_scr, s_buf, mb_buf, m_scr, l_scr, acc_scr, *, tq, kb):
    i = pl.program_id(0)
    ncol = MLA_HEADS * tq
    for hd in range(MLA_HEADS):
        q_scr[hd * tq:(hd + 1) * tq, :] = qp_ref[:, hd * QK_WIDTH:(hd + 1) * QK_WIDTH]
    m_scr[...] = jnp.full_like(m_scr, NEG_BIG)
    l_scr[...] = jnp.zeros_like(l_scr)
    acc_scr[...] = jnp.zeros_like(acc_scr)
    q0 = i * tq
    n_full = q0 // kb

    cw = min(ATTN_COLS, ncol)
    n_chunks = ncol // cw

    def scores(b, c, masked):
        cols = slice(c * cw, (c + 1) * cw)
        s = _dot_t1(k_ref[b], q_scr[cols, :])
        if masked:
            kpos = b * kb + lax.broadcasted_iota(jnp.int32, (kb, cw), 0)
            qpos = q0 + ((c * cw + lax.broadcasted_iota(jnp.int32, (kb, cw), 1)) & (tq - 1))
            s = jnp.where((kpos >> CHUNK_SHIFT) <= (qpos >> CHUNK_SHIFT), s, NEG_BIG)
        s_buf[c] = s
        mb_buf[c] = jnp.max(s, axis=0, keepdims=True)

    def values(b, c):
        cols = slice(c * cw, (c + 1) * cw)
        m_old = m_scr[:, cols]
        m_new = jnp.maximum(m_old, mb_buf[c])
        alpha = jnp.exp2(m_old - m_new)
        p = jnp.exp2(s_buf[c] - m_new)
        l_scr[:, cols] = alpha * l_scr[:, cols] + jnp.sum(p, axis=0, keepdims=True)
        acc_scr[:, cols] = alpha * acc_scr[:, cols] + _dot(vt_ref[b], p.astype(BF16))
        m_scr[:, cols] = m_new

    def step(b, masked_next):
        for c in range(n_chunks):
            values(b, c)
            scores(b + 1, c, masked_next)

    for c in range(n_chunks):
        scores(0, c, True)

    def body(b, carry):
        step(b, False)
        return carry

    lax.fori_loop(0, n_full - 1, body, 0)

    @pl.when(n_full >= 1)
    def _():
        step(n_full - 1, True)

    for c in range(n_chunks):
        values(n_full, c)

    o_lat = (acc_scr[...] / l_scr[...]).T.astype(BF16)
    o_ref[...] = _attn_project(o_lat, wuv_ref, tq)


def _attn_call(qp, k_all, vt_all, wuv, tq):
    t = qp.shape[0]
    n_blocks, kb = k_all.shape[0], k_all.shape[1]
    ncol = MLA_HEADS * tq
    cw = min(ATTN_COLS, ncol)
    return pl.pallas_call(
        functools.partial(_attn_kernel, tq=tq, kb=kb),
        grid=(t // tq,),
        in_specs=[pl.BlockSpec((tq, MLA_HEADS * QK_WIDTH), lambda i: (i, 0)),
                  _const_spec((n_blocks, kb, QK_WIDTH)), _const_spec((n_blocks, MLA_KV_RANK, kb)),
                  _const_spec((MLA_KV_RANK, MLA_WIDTH))],
        out_specs=pl.BlockSpec((tq, MLA_WIDTH), lambda i: (i, 0)),
        out_shape=jax.ShapeDtypeStruct((t, MLA_WIDTH), F32),
        scratch_shapes=[pltpu.VMEM((ncol, QK_WIDTH), BF16), pltpu.VMEM((ncol // cw, kb, cw), F32),
                        pltpu.VMEM((ncol // cw, 1, cw), F32), pltpu.VMEM((1, ncol), F32), pltpu.VMEM((1, ncol), F32),
                        pltpu.VMEM((MLA_KV_RANK, ncol), F32)],
        compiler_params=_cparams("arbitrary"),
    )(qp, k_all, vt_all, wuv)


def _attn_sample_kernel(qp_ref, kv_ref, pe_ref, ckv_ref, kpe_ref, wuv_ref, o_ref, *, tq):
    q = jnp.concatenate([qp_ref[:, hd * QK_WIDTH:(hd + 1) * QK_WIDTH] for hd in range(MLA_HEADS)], axis=0)
    q_lat = q[:, 0:MLA_KV_RANK]
    q_pe = q[:, MLA_KV_RANK:MLA_KV_RANK + MLA_ROPE]
    kv_old, pe_old = kv_ref[0].astype(BF16), pe_ref[0].astype(BF16)
    kv_new, pe_new = ckv_ref[0].astype(BF16), kpe_ref[0].astype(BF16)
    s_old = _dot_t1(q_lat, kv_old) + _dot_t1(q_pe, pe_old)
    s_new = _dot_t1(q_lat, kv_new) + _dot_t1(q_pe, pe_new)
    m = jnp.maximum(jnp.max(s_old, axis=-1, keepdims=True), jnp.max(s_new, axis=-1, keepdims=True))
    p_old = jnp.exp2(s_old - m)
    p_new = jnp.exp2(s_new - m)
    denom = jnp.sum(p_old, axis=-1, keepdims=True) + jnp.sum(p_new, axis=-1, keepdims=True)
    o_lat = (_dot(p_old.astype(BF16), kv_old) + _dot(p_new.astype(BF16), kv_new)) / denom
    o_ref[...] = _attn_project(o_lat.astype(BF16), wuv_ref, tq)


def _attn_sample_call(qp, kv_past, pe_past, c_kv, k_pe, wuv):
    n_seq, past, _ = kv_past.shape
    tq = qp.shape[0] // n_seq
    per_seq = lambda r, w: pl.BlockSpec((1, r, w), lambda i: (i, 0, 0))
    return pl.pallas_call(
        functools.partial(_attn_sample_kernel, tq=tq),
        grid=(n_seq,),
        in_specs=[pl.BlockSpec((tq, MLA_HEADS * QK_WIDTH), lambda i: (i, 0)), per_seq(past, MLA_KV_RANK),
                  per_seq(past, MLA_ROPE), per_seq(tq, MLA_KV_RANK), per_seq(tq, MLA_ROPE),
                  _const_spec((MLA_KV_RANK, MLA_WIDTH))],
        out_specs=pl.BlockSpec((tq, MLA_WIDTH), lambda i: (i, 0)),
        out_shape=jax.ShapeDtypeStruct((qp.shape[0], MLA_WIDTH), F32),
        compiler_params=_cparams("arbitrary"),
    )(qp, kv_past, pe_past, c_kv.reshape(n_seq, tq, MLA_KV_RANK), k_pe.reshape(n_seq, tq, MLA_ROPE), wuv)


def _out_kernel(x_ref, ys_ref, u_ref, oa_ref, ob_ref, hg_ref, mla_ref, d_ref, wglu_ref, bglu_ref, og_ref,
                wout_ref, g2_ref, wup_ref, wdn_ref, fg_ref, o_ref, *, final):
    y = jnp.concatenate([ys_ref[0], ys_ref[1]], axis=-1) + d_ref[...] * u_ref[...]
    z = jax.nn.gelu(y, approximate=True)
    s5 = z * jax.nn.sigmoid(_dot(z.astype(BF16), wglu_ref[...]) + bglu_ref[...])
    og = og_ref[...]
    hgate = hg_ref[...]
    mixed = jnp.concatenate([
        _rms(s5, og[:, 0:256]),
        _rms(oa_ref[...] + ob_ref[...], og[:, 256:512]) * (hgate * jax.nn.sigmoid(hgate)),
        _rms(mla_ref[...], og[:, 512:1024]),
    ], axis=-1)
    x1 = x_ref[...] + _dot(mixed.astype(BF16), wout_ref[...])
    h2 = _rms(x1, g2_ref[...]).astype(BF16)
    acc = x1
    for c in range(D_FF // FF_CHUNK):
        up = _dot(h2, wup_ref[:, c * FF_CHUNK:(c + 1) * FF_CHUNK])
        act = jnp.square(jnp.maximum(up, 0.0)).astype(BF16)
        acc = acc + _dot(act, wdn_ref[c * FF_CHUNK:(c + 1) * FF_CHUNK, :])
    if final:
        acc = _rms(acc, fg_ref[...])
    o_ref[...] = acc


def _out_call(x, ys, u, oa, ob, hg, mla, lw, final_g, tm, final):
    t = x.shape[0]
    row = lambda w: pl.BlockSpec((tm, w), lambda i: (i, 0))
    return pl.pallas_call(
        functools.partial(_out_kernel, final=final),
        grid=(t // tm,),
        in_specs=[row(D_MODEL), pl.BlockSpec((2, tm, LANES), lambda i: (0, i, 0)), row(256), row(256), row(256),
                  row(256), row(MLA_WIDTH),
                  _const_spec((1, 256)), _const_spec((256, 256)), _const_spec((1, 256)), _const_spec((1, D_MODEL)),
                  _const_spec((D_MODEL, D_MODEL)), _const_spec((1, D_MODEL)), _const_spec((D_MODEL, D_FF)),
                  _const_spec((D_FF, D_MODEL)), _const_spec((1, D_MODEL))],
        out_specs=row(D_MODEL),
        out_shape=jax.ShapeDtypeStruct((t, D_MODEL), F32),
        compiler_params=_cparams("arbitrary"),
    )(x, ys, u, oa, ob, hg, mla, lw["s5_d"], lw["s5_w_glu"], lw["s5_b_glu"], lw["out_norm_g"], lw["w_out"],
      lw["norm2_g"], lw["w_up"], lw["w_down"], final_g)


def _rot_cols(w):
    half = MLA_ROPE // 2
    return jnp.concatenate([-w[..., half:], w[..., :half]], axis=-1)


def _prep_layer(w_in, w_uq, w_uk, w_uv):
    kpe = w_in[:, 1664:1696]
    pad = jnp.zeros((D_MODEL, 96), F32)
    w_in_p = jnp.concatenate([w_in[:, :1664], kpe, pad, _rot_cols(kpe), pad], axis=1).astype(BF16)
    uq = w_uq.reshape(MLA_Q_RANK, MLA_HEADS, MLA_NOPE + MLA_ROPE)
    nope = uq[:, :, :MLA_NOPE].reshape(MLA_Q_RANK, 512)
    pe = uq[:, :, MLA_NOPE:]
    w_uq_p = jnp.concatenate([nope, pe.reshape(MLA_Q_RANK, 256), _rot_cols(pe).reshape(MLA_Q_RANK, 256)],
                             axis=1).astype(BF16)
    eye_h = jnp.eye(MLA_HEADS, dtype=F32)
    wuk_bd = jnp.einsum("chd,hk->hdkc", w_uk, eye_h).reshape(512, 1024).astype(BF16)
    place = jnp.einsum("hk,rc->hrkc", eye_h, jnp.eye(MLA_ROPE, LANES, dtype=F32)).reshape(256, 1024).astype(BF16)
    return w_in_p, w_uq_p, wuk_bd, place, w_uv.reshape(MLA_KV_RANK, MLA_WIDTH).astype(BF16)


def _prep_s5(lam_re, lam_im, log_dt, b_re, b_im, c_re, c_im):
    hp = lax.Precision.HIGHEST
    t = S5_T
    dt = jnp.exp(log_dt)[:, None]
    mag1 = jnp.exp(lam_re * dt)
    a_re, a_im = mag1 * jnp.cos(lam_im * dt), mag1 * jnp.sin(lam_im * dt)
    pw_re, pw_im = [jnp.ones_like(a_re)], [jnp.zeros_like(a_im)]
    for _ in range(t):
        pr, pi = pw_re[-1], pw_im[-1]
        pw_re.append(pr * a_re - pi * a_im)
        pw_im.append(pr * a_im + pi * a_re)
    p_re, p_im = jnp.stack(pw_re), jnp.stack(pw_im)
    den = lam_re * lam_re + lam_im * lam_im
    i_re, i_im = lam_re / den, -lam_im / den
    z_re = (a_re - 1.0) * i_re - a_im * i_im
    z_im = (a_re - 1.0) * i_im + a_im * i_re
    bb_re = z_re[..., None] * b_re - z_im[..., None] * b_im
    bb_im = z_re[..., None] * b_im + z_im[..., None] * b_re
    cp_re = c_re[None] * p_re[:, :, None, :] - c_im[None] * p_im[:, :, None, :]
    cp_im = c_re[None] * p_im[:, :, None, :] + c_im[None] * p_re[:, :, None, :]
    kern = (jnp.einsum("tgap,gph->tgah", cp_re[:t], bb_re, precision=hp)
            - jnp.einsum("tgap,gph->tgah", cp_im[:t], bb_im, precision=hp))
    same_g = lambda n_rows, per: ((jnp.arange(n_rows) // per) % S5_GROUPS)[:, None] == jnp.arange(S5_GROUPS)[None, :]
    g_in = same_g(S5_WIDTH, S5_GROUP).astype(F32)[:, :, None]
    kfull = (kern.transpose(0, 1, 3, 2).reshape(t, S5_WIDTH, 1, S5_GROUP) * g_in[None]).reshape(t, S5_WIDTH, S5_WIDTH)
    krev = jnp.concatenate([kfull[t - 1 - j] for j in range(t)], axis=0)
    rev_re, rev_im = p_re[t - 1::-1][:t], p_im[t - 1::-1][:t]
    w2_re = rev_re[..., None] * bb_re[None] - rev_im[..., None] * bb_im[None]
    w2_im = rev_re[..., None] * bb_im[None] + rev_im[..., None] * bb_re[None]
    w2 = jnp.stack([w2_re, w2_im]).transpose(1, 2, 4, 0, 3).reshape(S5_ROW, 2, 1, S5_STATE)
    m2 = (w2 * same_g(S5_ROW, S5_GROUP).astype(F32)[:, None, :, None]).reshape(S5_ROW, S5_FLAT)
    cp3 = jnp.stack([cp_re[1:], -cp_im[1:]]).transpose(0, 2, 4, 1, 3).reshape(S5_FLAT, t, 1, S5_GROUP)
    m3 = (cp3 * same_g(S5_FLAT, S5_STATE).astype(F32)[:, None, :, None]).reshape(S5_FLAT, S5_ROW)
    ab = jnp.concatenate([p_re[t].reshape(1, -1), p_im[t].reshape(1, -1)], axis=1)
    return {"krev": krev.astype(BF16), "m2": m2.astype(BF16), "m3": m3.astype(BF16), "ab": ab}


def _hg_consts(r_len):
    tt = HG_ROWS * r_len
    r = jnp.arange(tt)
    same = (r[:, None] // r_len) == (r[None, :] // r_len)
    tril = (same & (r[None, :] <= r[:, None])).astype(BF16)
    hd = jnp.arange(HG_WIDTH) // HG_DK
    hmask = (hd[:, None] == hd[None, :]).astype(F32)
    return {"tril": tril, "blk": same.astype(BF16), "hmask": hmask}


def _rope_tables(pos):
    half = MLA_ROPE // 2
    inv = ROPE_THETA ** (-jnp.arange(half, dtype=F32) / half)
    ang = pos.astype(F32)[:, None] * inv[None, :]
    reps = LANES // half
    return jnp.tile(jnp.cos(ang), (1, reps)), jnp.tile(jnp.sin(ang), (1, reps))


def _state_to_bd(s):
    eye_h = jnp.eye(HG_HEADS, dtype=F32)
    return jnp.einsum("bhdv,hk->bhvkd", s, eye_h).reshape(s.shape[0], HG_WIDTH, HG_WIDTH)


def _state_from_bd(s):
    b = s.shape[0]
    s5 = s.reshape(b, HG_HEADS, HG_DV, HG_HEADS, HG_DK)
    return jnp.stack([s5[:, h, :, h, :] for h in range(HG_HEADS)], axis=1).swapaxes(-1, -2)


def _layer(x, lw, sw, cos, sin, lb, hgc, ones_bd, final_g, final, *, prompt, n_seq, s5_h0, hg_s0, kv_past, pe_past):
    t = x.shape[0]
    seq = t // n_seq
    tm = KEY_BLOCK_PROMPT
    u, u_rows, hq, hf, hi, hg, c_kv, k_pe, qp, kk, vt = _in_call(x, lw, cos, sin, tm)

    if prompt:
        ys, s5_fin = _s5_prompt_call(u_rows, sw, min(256, t // S5_T))
    else:
        ys, s5_fin = _s5_sample_call(u_rows.reshape(n_seq, seq * S5_WIDTH), s5_h0, sw)
        ys = ys.reshape(t, 2, LANES).swapaxes(0, 1)

    r_len = 32 if prompt else seq
    av = lambda a: a.reshape(t // r_len, r_len * HG_WIDTH)
    oa = _hg_intra_call(av(hq), av(hf), av(hi), lb, ones_bd, r_len).reshape(t, HG_WIDTH)
    ob, hg_fin = _hg_inter_call(hq, hf, hi, lb, hgc, hg_s0, r_len, carry=prompt)

    if prompt:
        mla = _attn_call(qp, kk.reshape(t // tm, tm, QK_WIDTH), vt, lw["w_uv"], ATTN_QUERIES)
    else:
        mla = _attn_sample_call(qp, kv_past, pe_past, c_kv, k_pe, lw["w_uv"])

    x_new = _out_call(x, ys, u, oa, ob, hg, mla, lw, final_g, tm, final)
    return x_new, c_kv, k_pe, hg_fin, s5_fin


def kernel(x_prompt, x_sample, cache_mla_kv, cache_mla_pe, state_hgrn, state_s5_re, state_s5_im, norm1_g, w_in, s5_lambda_re, s5_lambda_im, s5_log_dt, s5_b_re, s5_b_im, s5_c_re, s5_c_im, s5_d, s5_w_glu, s5_b_glu, hgrn_lb_logits, mla_q_norm_g, mla_w_uq, mla_kv_norm_g, mla_w_uk, mla_w_uv, out_norm_g, w_out, norm2_g, w_up, w_down, final_norm_g):
    depth = w_in.shape[0]
    bp, lp = x_prompt.shape[0], x_prompt.shape[1]
    bs, ls = x_sample.shape[0], x_sample.shape[1]
    past = cache_mla_kv.shape[2]
    assert bp == 1 and ls == 2 * S5_T and bs % HG_ROWS == 0
    assert past % CHUNK == 0 and ls <= CHUNK

    cos_p, sin_p = _rope_tables(jnp.arange(lp, dtype=jnp.int32))
    cos_s, sin_s = _rope_tables(past + jnp.arange(ls, dtype=jnp.int32))
    cos_s, sin_s = jnp.tile(cos_s, (bs, 1)), jnp.tile(sin_s, (bs, 1))

    lb_p = jax.nn.softmax(hgrn_lb_logits.astype(F32), axis=0)
    lb_all = jnp.cumsum(lb_p, axis=0) - lb_p[0]
    hgc_p, hgc_s = _hg_consts(32), _hg_consts(ls)
    hd = jnp.arange(HG_WIDTH) // HG_DK
    ones_bd = (hd[:, None] == hd[None, :]).astype(BF16)
    row = lambda v: v.reshape(1, -1).astype(F32)
    final_g = row(final_norm_g)

    xp = x_prompt.reshape(bp * lp, D_MODEL)
    xs = x_sample.reshape(bs * ls, D_MODEL)
    outs_p, outs_s = [], []
    for l in range(depth):
        w_in_p, w_uq_p, wuk_bd, place, w_uv_p = _prep_layer(w_in[l], mla_w_uq[l], mla_w_uk[l], mla_w_uv[l])
        lw = {
            "norm1_g": row(norm1_g[l]), "w_in": w_in_p, "q_norm_g": row(mla_q_norm_g[l]), "w_uq": w_uq_p,
            "w_uk": wuk_bd, "place": place, "kv_norm_g": row(mla_kv_norm_g[l]), "w_uv": w_uv_p,
            "s5_d": row(s5_d[l]), "s5_w_glu": s5_w_glu[l].astype(BF16), "s5_b_glu": row(s5_b_glu[l]),
            "out_norm_g": row(out_norm_g[l]), "w_out": w_out[l].astype(BF16), "norm2_g": row(norm2_g[l]),
            "w_up": w_up[l].astype(BF16), "w_down": w_down[l].astype(BF16),
        }
        sw = _prep_s5(s5_lambda_re[l], s5_lambda_im[l], s5_log_dt[l], s5_b_re[l], s5_b_im[l],
                      s5_c_re[l], s5_c_im[l])
        lb = row(lb_all[l])
        final = l == depth - 1
        xp, a, b, c, d = _layer(xp, lw, sw, cos_p, sin_p, lb, hgc_p, ones_bd, final_g, final, prompt=True,
                                n_seq=1, s5_h0=None, hg_s0=jnp.zeros((1, HG_WIDTH, HG_WIDTH), F32),
                                kv_past=None, pe_past=None)
        outs_p.append((a, b, c, d))
        h0 = jnp.concatenate([state_s5_re[l].reshape(bs, -1), state_s5_im[l].reshape(bs, -1)], axis=1)
        xs, a, b, c, d = _layer(xs, lw, sw, cos_s, sin_s, lb, hgc_s, ones_bd, final_g, final, prompt=False,
                                n_seq=bs, s5_h0=h0, hg_s0=_state_to_bd(state_hgrn[l]),
                                kv_past=cache_mla_kv[l], pe_past=cache_mla_pe[l])
        outs_s.append((a, b, c, d))

    def gather(outs, nb, sl):
        kv = jnp.stack([o[0].reshape(nb, sl, MLA_KV_RANK) for o in outs])
        pe = jnp.stack([o[1].reshape(nb, sl, MLA_ROPE) for o in outs])
        hg = jnp.stack([_state_from_bd(o[2]) for o in outs])
        half = S5_FLAT // 2
        re = jnp.stack([o[3][:, :half].reshape(nb, S5_GROUPS, S5_STATE) for o in outs])
        im = jnp.stack([o[3][:, half:].reshape(nb, S5_GROUPS, S5_STATE) for o in outs])
        return kv, pe, hg, re, im

    p_kv, p_pe, p_hg, p_re, p_im = gather(outs_p, bp, lp)
    s_kv, s_pe, s_hg, s_re, s_im = gather(outs_s, bs, ls)
    return (xp.reshape(bp, lp, D_MODEL), xs.reshape(bs, ls, D_MODEL),
            p_kv, p_pe, p_hg, p_re, p_im, s_kv, s_pe, s_hg, s_re, s_im)
```

```python
import functools
import math

import jax
import jax.numpy as jnp
from jax import lax
from jax.experimental import pallas as pl
from jax.experimental.pallas import tpu as pltpu

F32 = jnp.float32
BF16 = jnp.bfloat16

D_MODEL = 1024
CHUNK = 64
CHUNK_SHIFT = CHUNK.bit_length() - 1
EPS = 1e-5
NEG_BIG = -1e30

S5_WIDTH = 256
S5_GROUP = 16
S5_GROUPS = 16
S5_STATE = 64
S5_T = 8
S5_ROW = S5_T * S5_WIDTH
S5_FLAT = 2 * S5_GROUPS * S5_STATE

HG_HEADS = 4
HG_DK = 64
HG_DV = 64
HG_WIDTH = 256
HG_ROWS = 16
HG_PITCH_PAD = 4

MLA_HEADS = 8
MLA_Q_RANK = 256
MLA_KV_RANK = 128
MLA_NOPE = 64
MLA_ROPE = 32
MLA_V = 64
MLA_WIDTH = 512
ROPE_THETA = 10000.0
ROW_TILE = 512
KEY_BLOCK_PROMPT = 512
QK_WIDTH = 256
LOG2E = 1.4426950408889634
ATTN_COLS = 512
ATTN_QUERIES = 512

D_FF = 4096
FF_CHUNK = 1024
IN_PAD = 1920

LANES = 128
VMEM_LIMIT = 56 * 1024 * 1024


def _cparams(*sem):
    return pltpu.CompilerParams(dimension_semantics=sem, vmem_limit_bytes=VMEM_LIMIT)


def _const_spec(shape):
    nd = len(shape)
    return pl.BlockSpec(shape, lambda *_: (0,) * nd, pipeline_mode=pl.Buffered(1))


def _layer_spec(shape, l):
    nd = len(shape)
    return pl.BlockSpec((None,) + tuple(shape), lambda *_: (l,) + (0,) * nd, pipeline_mode=pl.Buffered(1))


def _rms(x, g):
    y = x * lax.rsqrt(jnp.mean(x * x, axis=-1, keepdims=True) + EPS)
    return y * g


def _dot(a, b):
    return jnp.dot(a, b, preferred_element_type=F32)


def _dot_t0(a, b):
    return lax.dot_general(a, b, (((0,), (0,)), ((), ())), preferred_element_type=F32)


def _dot_t1(a, b):
    return lax.dot_general(a, b, (((1,), (1,)), ((), ())), preferred_element_type=F32)


def _split3(x):
    hi = x.astype(BF16)
    r1 = x - hi.astype(F32)
    mid = r1.astype(BF16)
    lo = (r1 - mid.astype(F32)).astype(BF16)
    return hi, mid, lo


def _in_kernel(x_ref, g1_ref, win_ref, qg_ref, wuq_ref, wuk_ref, place_ref, kvg_ref, cos_ref, sin_ref,
               u_ref, urow_ref, hq_ref, hf_ref, hi_ref, hg_ref, ckv_ref, kpe_ref, qp_ref, kk_ref, vt_ref, u_scr):
    h = _rms(x_ref[...], g1_ref[...])
    proj = _dot(h.astype(BF16), win_ref[...])
    u_ref[...] = proj[:, 0:256]
    n_row = proj.shape[0] // S5_T
    for hv in range(2):
        u_scr[hv] = proj[:, hv * LANES:(hv + 1) * LANES]
    for s in range(S5_T):
        for hv in range(2):
            lo = s * S5_WIDTH + hv * LANES
            urow_ref[:, lo:lo + LANES] = u_scr[hv, pl.ds(s, n_row, stride=S5_T), :]
    hq_ref[...] = proj[:, 256:512]
    hf_ref[...] = proj[:, 512:768]
    hi_ref[...] = proj[:, 768:1024]
    hg_ref[...] = proj[:, 1024:1280]

    cos = cos_ref[...]
    sin = sin_ref[...]
    cos2 = jnp.concatenate([cos, cos], axis=-1)
    sin2 = jnp.concatenate([sin, sin], axis=-1)

    cqn = _rms(proj[:, 1280:1536], qg_ref[...])
    q = _dot(cqn.astype(BF16), wuq_ref[...])
    scale = (MLA_NOPE + MLA_ROPE) ** -0.5 * LOG2E
    q_pe = (q[:, 512:768] * cos2 + q[:, 768:1024] * sin2) * scale
    q_lat = _dot(q[:, 0:512].astype(BF16), wuk_ref[...]) * scale
    pe_pl = _dot(q_pe.astype(BF16), place_ref[...])
    for hd in range(MLA_HEADS):
        qp_ref[:, hd * 256:hd * 256 + 128] = q_lat[:, hd * 128:(hd + 1) * 128].astype(BF16)
        qp_ref[:, hd * 256 + 128:(hd + 1) * 256] = pe_pl[:, hd * 128:(hd + 1) * 128].astype(BF16)

    c_kv = _rms(proj[:, 1536:1664], kvg_ref[...])
    ckv_ref[...] = c_kv
    kpe = proj[:, 1664:1792] * cos + proj[:, 1792:1920] * sin
    kpe_ref[...] = kpe[:, 0:MLA_ROPE]
    kk_ref[...] = jnp.concatenate([c_kv, kpe], axis=-1).astype(BF16)
    vt_ref[0] = c_kv.T.astype(BF16)


def _in_call(x, lw, l, cos, sin, tm):
    t = x.shape[0]
    row = lambda w: pl.BlockSpec((tm, w), lambda i: (i, 0))
    outs = [jax.ShapeDtypeStruct((t, 256), F32), jax.ShapeDtypeStruct((t // S5_T, S5_ROW), F32)] + [
        jax.ShapeDtypeStruct((t, 256), F32)] * 4 + [
        jax.ShapeDtypeStruct((t, MLA_KV_RANK), F32),
        jax.ShapeDtypeStruct((t, MLA_ROPE), F32),
        jax.ShapeDtypeStruct((t, MLA_HEADS * QK_WIDTH), BF16),
        jax.ShapeDtypeStruct((t, QK_WIDTH), BF16),
        jax.ShapeDtypeStruct((t // tm, MLA_KV_RANK, tm), BF16),
    ]
    out_specs = [row(256), pl.BlockSpec((tm // S5_T, S5_ROW), lambda i: (i, 0))] + [row(256)] * 4 + [
        row(MLA_KV_RANK), row(MLA_ROPE), row(MLA_HEADS * QK_WIDTH), row(QK_WIDTH),
        pl.BlockSpec((1, MLA_KV_RANK, tm), lambda i: (i, 0, 0))]
    return pl.pallas_call(
        _in_kernel,
        grid=(t // tm,),
        in_specs=[row(D_MODEL), _layer_spec((1, D_MODEL), l), _layer_spec((D_MODEL, IN_PAD), l),
                  _layer_spec((1, MLA_Q_RANK), l), _layer_spec((MLA_Q_RANK, 1024), l), _layer_spec((512, 1024), l),
                  _const_spec((256, 1024)), _layer_spec((1, MLA_KV_RANK), l), row(LANES), row(LANES)],
        out_specs=out_specs,
        out_shape=outs,
        scratch_shapes=[pltpu.VMEM((2, tm, LANES), F32)],
        compiler_params=_cparams("arbitrary"),
    )(x, lw["norm1_g"], lw["w_in"], lw["q_norm_g"], lw["w_uq"], lw["w_uk"], lw["place"], lw["kv_norm_g"],
      cos, sin)


def _s5_in_row(u, krev_ref):
    w = S5_WIDTH
    parts = [_dot(u[:, 0:(t + 1) * w], krev_ref[(S5_T - 1 - t) * w:S5_T * w, :]) for t in range(S5_T)]
    return jnp.concatenate(parts, axis=1)


def _s5_prompt_kernel(u_ref, krev_ref, m2_ref, m3_ref, ab_ref, y_ref, hfin_ref, s_scr, hp_scr, h_scr):
    half = S5_FLAT // 2
    tn = u_ref.shape[0]

    @pl.when(pl.program_id(0) == 0)
    def _():
        h_scr[...] = jnp.zeros_like(h_scr)

    u = u_ref[...].astype(BF16)
    s_scr[...] = _dot(u, m2_ref[...])
    ar = ab_ref[:, 0:half]
    ai = ab_ref[:, half:S5_FLAT]

    def body(i, carry):
        hr, hi = carry
        hp_scr[pl.ds(i, 1), 0:half] = hr
        hp_scr[pl.ds(i, 1), half:S5_FLAT] = hi
        sr = s_scr[pl.ds(i, 1), 0:half]
        si = s_scr[pl.ds(i, 1), half:S5_FLAT]
        return ar * hr - ai * hi + sr, ar * hi + ai * hr + si

    hr, hi = lax.fori_loop(0, tn, body, (h_scr[:, 0:half], h_scr[:, half:S5_FLAT]), unroll=8)
    h_scr[:, 0:half] = hr
    h_scr[:, half:S5_FLAT] = hi
    y = _s5_in_row(u, krev_ref) + _dot(hp_scr[...].astype(BF16), m3_ref[...])
    for s in range(S5_T):
        for hv in range(2):
            lo = s * S5_WIDTH + hv * LANES
            y_ref[hv, pl.ds(s, tn, stride=S5_T), :] = y[:, lo:lo + LANES]
    hfin_ref[...] = h_scr[...]


def _s5_prompt_call(u_rows, sw, l, tn):
    n = u_rows.shape[0]
    return pl.pallas_call(
        _s5_prompt_kernel,
        grid=(n // tn,),
        in_specs=[pl.BlockSpec((tn, S5_ROW), lambda i: (i, 0)),
                  _layer_spec((S5_ROW, S5_WIDTH), l), _layer_spec((S5_ROW, S5_FLAT), l),
                  _layer_spec((S5_FLAT, S5_ROW), l), _layer_spec((1, S5_FLAT), l)],
        out_specs=[pl.BlockSpec((2, tn * S5_T, LANES), lambda i: (0, i, 0)),
                   pl.BlockSpec((1, S5_FLAT), lambda i: (0, 0))],
        out_shape=[jax.ShapeDtypeStruct((2, n * S5_T, LANES), F32), jax.ShapeDtypeStruct((1, S5_FLAT), F32)],
        scratch_shapes=[pltpu.VMEM((tn, S5_FLAT), F32), pltpu.VMEM((tn, S5_FLAT), F32),
                        pltpu.VMEM((1, S5_FLAT), F32)],
        compiler_params=_cparams("arbitrary"),
    )(u_rows, sw["krev"], sw["m2"], sw["m3"], sw["ab"])


def _s5_sample_kernel(u_ref, h0_ref, krev_ref, m2_ref, m3_ref, ab_ref, y_ref, hfin_ref, *, n_rows):
    half = S5_FLAT // 2
    ar = ab_ref[:, 0:half]
    ai = ab_ref[:, half:S5_FLAT]
    us = [u_ref[:, c * S5_ROW:(c + 1) * S5_ROW].astype(BF16) for c in range(n_rows)]
    u_all = jnp.concatenate(us, axis=0)
    b = us[0].shape[0]
    s_all = _dot(u_all, m2_ref[...])
    y1_all = _s5_in_row(u_all, krev_ref)
    hr = h0_ref[:, 0:half]
    hi = h0_ref[:, half:S5_FLAT]
    enter = []
    for c in range(n_rows):
        enter.append(jnp.concatenate([hr, hi], axis=-1))
        sr = s_all[c * b:(c + 1) * b, 0:half]
        si = s_all[c * b:(c + 1) * b, half:S5_FLAT]
        hr, hi = ar * hr - ai * hi + sr, ar * hi + ai * hr + si
    y_all = y1_all + _dot(jnp.concatenate(enter, axis=0).astype(BF16), m3_ref[...])
    for c in range(n_rows):
        y_ref[:, c * S5_ROW:(c + 1) * S5_ROW] = y_all[c * b:(c + 1) * b, :]
    hfin_ref[:, 0:half] = hr
    hfin_ref[:, half:S5_FLAT] = hi


def _s5_sample_call(u_seq, h0, sw, l):
    b, w = u_seq.shape
    n_rows = w // S5_ROW
    full = lambda shape: pl.BlockSpec(shape, lambda i: (0,) * len(shape))
    return pl.pallas_call(
        functools.partial(_s5_sample_kernel, n_rows=n_rows),
        grid=(1,),
        in_specs=[full((b, w)), _layer_spec((b, S5_FLAT), l), _layer_spec((S5_ROW, S5_WIDTH), l),
                  _layer_spec((S5_ROW, S5_FLAT), l), _layer_spec((S5_FLAT, S5_ROW), l), _layer_spec((1, S5_FLAT), l)],
        out_specs=[full((b, w)), full((b, S5_FLAT))],
        out_shape=[jax.ShapeDtypeStruct((b, w), F32), jax.ShapeDtypeStruct((b, S5_FLAT), F32)],
        compiler_params=_cparams("arbitrary"),
    )(u_seq, h0, sw["krev"], sw["m2"], sw["m3"], sw["ab"])


def _hg_gates(hq, hf, lb):
    sig = jax.nn.sigmoid(hf)
    f = lb + (1.0 - lb) * sig
    k = (1.0 - lb) * jax.nn.sigmoid(-hf)
    qf = hq * jax.nn.sigmoid(hq)
    return qf, k, f


def _hg_intra_kernel(hq_ref, hf_ref, hi_ref, lb_ref, ones_ref, o_ref, pad_scr, out_scr, k_scr, f_scr, *, r_len):
    lb = lb_ref[...]
    ones_bd = ones_ref[...]
    pitch = r_len + HG_PITCH_PAD
    for a, ref in enumerate((hq_ref, hf_ref, hi_ref)):
        for hv in range(2):
            for j in range(HG_ROWS):
                pad_scr[a, hv, pl.ds(j * pitch, r_len), :] = ref[j * r_len:(j + 1) * r_len, hv * LANES:(hv + 1) * LANES]

    def slab(a, r):
        return jnp.concatenate([pad_scr[a, hv, pl.ds(r, HG_ROWS, stride=pitch), :] for hv in range(2)], axis=-1)

    for r in range(r_len):
        _, k, f = _hg_gates(slab(0, r), slab(1, r), lb)
        k_scr[r] = k
        f_scr[r] = f
    for r in range(r_len):
        hq = slab(0, r)
        qp = hq * jax.nn.sigmoid(hq)
        terms = []
        for s in range(r, -1, -1):
            terms.append((qp * k_scr[s]).astype(BF16))
            if s > 0:
                qp = qp * f_scr[s]
        att = _dot(jnp.concatenate(terms, axis=0), ones_bd)
        acc = None
        for j, s in enumerate(range(r, -1, -1)):
            part = att[j * HG_ROWS:(j + 1) * HG_ROWS, :] * slab(2, s)
            acc = part if acc is None else acc + part
        for hv in range(2):
            out_scr[hv, pl.ds(r, HG_ROWS, stride=pitch), :] = acc[:, hv * LANES:(hv + 1) * LANES]
    for hv in range(2):
        for j in range(HG_ROWS):
            o_ref[j * r_len:(j + 1) * r_len, hv * LANES:(hv + 1) * LANES] = out_scr[hv, pl.ds(j * pitch, r_len), :]


def _hg_intra_call(hq, hf, hi, lb, l, ones_bd, r_len):
    t = hq.shape[0]
    tt = HG_ROWS * r_len
    pad_rows = HG_ROWS * (r_len + HG_PITCH_PAD)
    spec = pl.BlockSpec((tt, HG_WIDTH), lambda i: (i, 0))
    return pl.pallas_call(
        functools.partial(_hg_intra_kernel, r_len=r_len),
        grid=(t // tt,),
        in_specs=[spec, spec, spec, _layer_spec((1, HG_WIDTH), l), _const_spec((HG_WIDTH, HG_WIDTH))],
        out_specs=spec,
        out_shape=jax.ShapeDtypeStruct(hq.shape, F32),
        scratch_shapes=[pltpu.VMEM((3, 2, pad_rows, LANES), F32), pltpu.VMEM((2, pad_rows, LANES), F32),
                        pltpu.VMEM((r_len, HG_ROWS, HG_WIDTH), F32), pltpu.VMEM((r_len, HG_ROWS, HG_WIDTH), F32)],
        compiler_params=_cparams("arbitrary"),
    )(hq, hf, hi, lb, ones_bd)


def _hg_inter_kernel(hq_ref, hf_ref, hi_ref, lb_ref, tril_ref, blk_ref, hmask_ref, s0_ref,
                     o_ref, sout_ref, s_scr, *, r_len, carry):
    n_sub = HG_ROWS
    if carry:
        @pl.when(pl.program_id(0) == 0)
        def _():
            s_scr[...] = s0_ref[0]

    qf, k, f = _hg_gates(hq_ref[...], hf_ref[...], lb_ref[...])
    g = jnp.log(f)
    g3 = _split3(g)
    tril = tril_ref[...]
    blk = blk_ref[...]
    bl = _dot(tril, g3[0]) + _dot(tril, g3[1]) + _dot(tril, g3[2])
    bsum = _dot(blk, g3[0]) + _dot(blk, g3[1]) + _dot(blk, g3[2])
    q_in = (qf * jnp.exp(bl)).astype(BF16)
    k_out = (k * jnp.exp(bsum - bl)).astype(BF16)
    v = hi_ref[...].astype(BF16)
    hmask = hmask_ref[...]
    decay = jnp.exp(bsum)
    for j in range(n_sub):
        sl = slice(j * r_len, (j + 1) * r_len)
        s_in = s_scr[...] if carry else s0_ref[j]
        o_ref[sl, :] = _dot_t1(q_in[sl], s_in.astype(BF16))
        w_new = _dot_t0(v[sl], k_out[sl]) * hmask
        s_new = decay[j * r_len:j * r_len + 1, :] * s_in + w_new
        if carry:
            s_scr[...] = s_new
        else:
            sout_ref[j] = s_new
    if carry:
        sout_ref[0] = s_scr[...]


def _hg_inter_call(hq, hf, hi, lb, l, consts, s0, r_len, carry):
    t = hq.shape[0]
    tt = HG_ROWS * r_len
    spec = pl.BlockSpec((tt, HG_WIDTH), lambda i: (i, 0))
    if carry:
        s_in_spec = pl.BlockSpec((1, HG_WIDTH, HG_WIDTH), lambda i: (0, 0, 0))
        s_out_spec = s_in_spec
    else:
        s_in_spec = pl.BlockSpec((None, HG_ROWS, HG_WIDTH, HG_WIDTH), lambda i: (l, i, 0, 0))
        s_out_spec = pl.BlockSpec((HG_ROWS, HG_WIDTH, HG_WIDTH), lambda i: (i, 0, 0))
    return pl.pallas_call(
        functools.partial(_hg_inter_kernel, r_len=r_len, carry=carry),
        grid=(t // tt,),
        in_specs=[spec, spec, spec, _layer_spec((1, HG_WIDTH), l), _const_spec((tt, tt)), _const_spec((tt, tt)),
                  _const_spec((HG_WIDTH, HG_WIDTH)), s_in_spec],
        out_specs=[spec, s_out_spec],
        out_shape=[jax.ShapeDtypeStruct((t, HG_WIDTH), F32), jax.ShapeDtypeStruct(s0.shape[-3:], F32)],
        scratch_shapes=[pltpu.VMEM((HG_WIDTH, HG_WIDTH), F32)],
        compiler_params=_cparams("arbitrary"),
    )(hq, hf, hi, lb, consts["tril"], consts["blk"], consts["hmask"], s0)


def _attn_project(o_lat, wuv_ref, tq):
    full = _dot(o_lat, wuv_ref[...])
    lane_head = lax.broadcasted_iota(jnp.int32, (tq, MLA_WIDTH), 1) // MLA_V
    out = jnp.zeros((tq, MLA_WIDTH), F32)
    for hd in range(MLA_HEADS):
        out = out + jnp.where(lane_head == hd, full[hd * tq:(hd + 1) * tq, :], 0.0)
    return out


def _attn_kernel(qp_ref, k_ref, vt_ref, wuv_ref, o_ref, q_scr, s_buf, mb_buf, m_scr, l_scr, acc_scr, *, tq, kb):
    i = pl.program_id(0)
    ncol = MLA_HEADS * tq
    for hd in range(MLA_HEADS):
        q_scr[hd * tq:(hd + 1) * tq, :] = qp_ref[:, hd * QK_WIDTH:(hd + 1) * QK_WIDTH]
    m_scr[...] = jnp.full_like(m_scr, NEG_BIG)
    l_scr[...] = jnp.zeros_like(l_scr)
    acc_scr[...] = jnp.zeros_like(acc_scr)
    q0 = i * tq
    n_full = q0 // kb

    cw = min(ATTN_COLS, ncol)
    n_chunks = ncol // cw
    vt_w = vt_ref.shape[2]
    vt_per = kb // vt_w

    def scores(b, c, masked):
        cols = slice(c * cw, (c + 1) * cw)
        s = _dot_t1(k_ref[b], q_scr[cols, :])
        if masked:
            kpos = b * kb + lax.broadcasted_iota(jnp.int32, (kb, cw), 0)
            qpos = q0 + ((c * cw + lax.broadcasted_iota(jnp.int32, (kb, cw), 1)) & (tq - 1))
            s = jnp.where((kpos >> CHUNK_SHIFT) <= (qpos >> CHUNK_SHIFT), s, NEG_BIG)
        s_buf[c] = s
        mb_buf[c] = jnp.max(s, axis=0, keepdims=True)

    def values(b, c):
        cols = slice(c * cw, (c + 1) * cw)
        m_old = m_scr[:, cols]
        m_new = jnp.maximum(m_old, mb_buf[c])
        alpha = jnp.exp2(m_old - m_new)
        p = jnp.exp2(s_buf[c] - m_new)
        l_scr[:, cols] = alpha * l_scr[:, cols] + jnp.sum(p, axis=0, keepdims=True)
        pb = p.astype(BF16)
        pv = _dot(vt_ref[b * vt_per], pb[0:vt_w, :])
        for j in range(1, vt_per):
            pv = pv + _dot(vt_ref[b * vt_per + j], pb[j * vt_w:(j + 1) * vt_w, :])
        acc_scr[:, cols] = alpha * acc_scr[:, cols] + pv
        m_scr[:, cols] = m_new

    def step(b, masked_next):
        for c in range(n_chunks):
            values(b, c)
            scores(b + 1, c, masked_next)

    for c in range(n_chunks):
        scores(0, c, True)

    def body(b, carry):
        step(b, False)
        return carry

    lax.fori_loop(0, n_full - 1, body, 0)

    @pl.when(n_full >= 1)
    def _():
        step(n_full - 1, True)

    for c in range(n_chunks):
        values(n_full, c)

    o_lat = (acc_scr[...] / l_scr[...]).T.astype(BF16)
    o_ref[...] = _attn_project(o_lat, wuv_ref, tq)


def _attn_call(qp, k_all, vt_all, wuv, l, tq):
    t = qp.shape[0]
    n_blocks, kb = k_all.shape[0], k_all.shape[1]
    ncol = MLA_HEADS * tq
    cw = min(ATTN_COLS, ncol)
    return pl.pallas_call(
        functools.partial(_attn_kernel, tq=tq, kb=kb),
        grid=(t // tq,),
        in_specs=[pl.BlockSpec((tq, MLA_HEADS * QK_WIDTH), lambda i: (i, 0)),
                  _const_spec((n_blocks, kb, QK_WIDTH)), _const_spec(vt_all.shape),
                  _layer_spec((MLA_KV_RANK, MLA_WIDTH), l)],
        out_specs=pl.BlockSpec((tq, MLA_WIDTH), lambda i: (i, 0)),
        out_shape=jax.ShapeDtypeStruct((t, MLA_WIDTH), F32),
        scratch_shapes=[pltpu.VMEM((ncol, QK_WIDTH), BF16), pltpu.VMEM((ncol // cw, kb, cw), F32),
                        pltpu.VMEM((ncol // cw, 1, cw), F32), pltpu.VMEM((1, ncol), F32), pltpu.VMEM((1, ncol), F32),
                        pltpu.VMEM((MLA_KV_RANK, ncol), F32)],
        compiler_params=_cparams("arbitrary"),
    )(qp, k_all, vt_all, wuv)


def _attn_sample_kernel(qp_ref, kv_ref, pe_ref, ckv_ref, kpe_ref, wuv_ref, o_ref, *, tq):
    q = jnp.concatenate([qp_ref[:, hd * QK_WIDTH:(hd + 1) * QK_WIDTH] for hd in range(MLA_HEADS)], axis=0)
    q_lat = q[:, 0:MLA_KV_RANK]
    q_pe = q[:, MLA_KV_RANK:MLA_KV_RANK + MLA_ROPE]
    kv_old, pe_old = kv_ref[0].astype(BF16), pe_ref[0].astype(BF16)
    kv_new, pe_new = ckv_ref[0].astype(BF16), kpe_ref[0].astype(BF16)
    s_old = _dot_t1(q_lat, kv_old) + _dot_t1(q_pe, pe_old)
    s_new = _dot_t1(q_lat, kv_new) + _dot_t1(q_pe, pe_new)
    m = jnp.maximum(jnp.max(s_old, axis=-1, keepdims=True), jnp.max(s_new, axis=-1, keepdims=True))
    p_old = jnp.exp2(s_old - m)
    p_new = jnp.exp2(s_new - m)
    denom = jnp.sum(p_old, axis=-1, keepdims=True) + jnp.sum(p_new, axis=-1, keepdims=True)
    o_lat = (_dot(p_old.astype(BF16), kv_old) + _dot(p_new.astype(BF16), kv_new)) / denom
    o_ref[...] = _attn_project(o_lat.astype(BF16), wuv_ref, tq)


def _attn_sample_call(qp, kv_past, pe_past, c_kv, k_pe, wuv, l):
    _, n_seq, past, _ = kv_past.shape
    tq = qp.shape[0] // n_seq
    per_seq = lambda r, w: pl.BlockSpec((1, r, w), lambda i: (i, 0, 0))
    cached = lambda w: pl.BlockSpec((None, 1, past, w), lambda i: (l, i, 0, 0))
    return pl.pallas_call(
        functools.partial(_attn_sample_kernel, tq=tq),
        grid=(n_seq,),
        in_specs=[pl.BlockSpec((tq, MLA_HEADS * QK_WIDTH), lambda i: (i, 0)), cached(MLA_KV_RANK),
                  cached(MLA_ROPE), per_seq(tq, MLA_KV_RANK), per_seq(tq, MLA_ROPE),
                  _layer_spec((MLA_KV_RANK, MLA_WIDTH), l)],
        out_specs=pl.BlockSpec((tq, MLA_WIDTH), lambda i: (i, 0)),
        out_shape=jax.ShapeDtypeStruct((qp.shape[0], MLA_WIDTH), F32),
        compiler_params=_cparams("arbitrary"),
    )(qp, kv_past, pe_past, c_kv.reshape(n_seq, tq, MLA_KV_RANK), k_pe.reshape(n_seq, tq, MLA_ROPE), wuv)


def _out_kernel(x_ref, ys_ref, u_ref, oa_ref, ob_ref, hg_ref, mla_ref, d_ref, wglu_ref, bglu_ref, og_ref,
                wout_ref, g2_ref, wup_ref, wdn_ref, fg_ref, o_ref, *, final):
    y = jnp.concatenate([ys_ref[0], ys_ref[1]], axis=-1) + d_ref[...] * u_ref[...]
    z = jax.nn.gelu(y, approximate=True)
    s5 = z * jax.nn.sigmoid(_dot(z.astype(BF16), wglu_ref[...]) + bglu_ref[...])
    og = og_ref[...]
    hgate = hg_ref[...]
    mixed = jnp.concatenate([
        _rms(s5, og[:, 0:256]),
        _rms(oa_ref[...] + ob_ref[...], og[:, 256:512]) * (hgate * jax.nn.sigmoid(hgate)),
        _rms(mla_ref[...], og[:, 512:1024]),
    ], axis=-1)
    x1 = x_ref[...] + _dot(mixed.astype(BF16), wout_ref[...])
    h2 = _rms(x1, g2_ref[...]).astype(BF16)
    acc = x1
    for c in range(D_FF // FF_CHUNK):
        up = _dot(h2, wup_ref[:, c * FF_CHUNK:(c + 1) * FF_CHUNK])
        act = jnp.square(jnp.maximum(up, 0.0)).astype(BF16)
        acc = acc + _dot(act, wdn_ref[c * FF_CHUNK:(c + 1) * FF_CHUNK, :])
    if final:
        acc = _rms(acc, fg_ref[...])
    o_ref[...] = acc


def _out_call(x, ys, u, oa, ob, hg, mla, lw, l, final_g, tm, final):
    t = x.shape[0]
    row = lambda w: pl.BlockSpec((tm, w), lambda i: (i, 0))
    return pl.pallas_call(
        functools.partial(_out_kernel, final=final),
        grid=(t // tm,),
        in_specs=[row(D_MODEL), pl.BlockSpec((2, tm, LANES), lambda i: (0, i, 0)), row(256), row(256), row(256),
                  row(256), row(MLA_WIDTH),
                  _layer_spec((1, 256), l), _layer_spec((256, 256), l), _layer_spec((1, 256), l),
                  _layer_spec((1, D_MODEL), l), _layer_spec((D_MODEL, D_MODEL), l), _layer_spec((1, D_MODEL), l),
                  _layer_spec((D_MODEL, D_FF), l), _layer_spec((D_FF, D_MODEL), l), _const_spec((1, D_MODEL))],
        out_specs=row(D_MODEL),
        out_shape=jax.ShapeDtypeStruct((t, D_MODEL), F32),
        compiler_params=_cparams("arbitrary"),
    )(x, ys, u, oa, ob, hg, mla, lw["s5_d"], lw["s5_w_glu"], lw["s5_b_glu"], lw["out_norm_g"], lw["w_out"],
      lw["norm2_g"], lw["w_up"], lw["w_down"], final_g)


def _rot_cols(w):
    half = MLA_ROPE // 2
    return jnp.concatenate([-w[..., half:], w[..., :half]], axis=-1)


def _prep_mla(w_in, w_uq, w_uk, w_uv):
    nl = w_in.shape[0]
    kpe = w_in[..., 1664:1696]
    pad = jnp.zeros((nl, D_MODEL, 96), F32)
    w_in_p = jnp.concatenate([w_in[..., :1664], kpe, pad, _rot_cols(kpe), pad], axis=-1).astype(BF16)
    uq = w_uq.reshape(nl, MLA_Q_RANK, MLA_HEADS, MLA_NOPE + MLA_ROPE)
    nope = uq[..., :MLA_NOPE].reshape(nl, MLA_Q_RANK, 512)
    pe = uq[..., MLA_NOPE:]
    w_uq_p = jnp.concatenate([nope, pe.reshape(nl, MLA_Q_RANK, 256), _rot_cols(pe).reshape(nl, MLA_Q_RANK, 256)],
                             axis=-1).astype(BF16)
    eye_h = jnp.eye(MLA_HEADS, dtype=F32)
    wuk_bd = jnp.einsum("lchd,hk->lhdkc", w_uk, eye_h).reshape(nl, 512, 1024).astype(BF16)
    place = jnp.einsum("hk,rc->hrkc", eye_h, jnp.eye(MLA_ROPE, LANES, dtype=F32)).reshape(256, 1024).astype(BF16)
    return w_in_p, w_uq_p, wuk_bd, place, w_uv.reshape(nl, MLA_KV_RANK, MLA_WIDTH).astype(BF16)


def _prep_s5(lam_re, lam_im, log_dt, b_re, b_im, c_re, c_im):
    hp = lax.Precision.HIGHEST
    t = S5_T
    dt = jnp.exp(log_dt)[:, None]
    mag1 = jnp.exp(lam_re * dt)
    a_re, a_im = mag1 * jnp.cos(lam_im * dt), mag1 * jnp.sin(lam_im * dt)
    pw_re, pw_im = [jnp.ones_like(a_re)], [jnp.zeros_like(a_im)]
    for _ in range(t):
        pr, pi = pw_re[-1], pw_im[-1]
        pw_re.append(pr * a_re - pi * a_im)
        pw_im.append(pr * a_im + pi * a_re)
    p_re, p_im = jnp.stack(pw_re), jnp.stack(pw_im)
    den = lam_re * lam_re + lam_im * lam_im
    i_re, i_im = lam_re / den, -lam_im / den
    z_re = (a_re - 1.0) * i_re - a_im * i_im
    z_im = (a_re - 1.0) * i_im + a_im * i_re
    bb_re = z_re[..., None] * b_re - z_im[..., None] * b_im
    bb_im = z_re[..., None] * b_im + z_im[..., None] * b_re
    cp_re = c_re[None] * p_re[:, :, None, :] - c_im[None] * p_im[:, :, None, :]
    cp_im = c_re[None] * p_im[:, :, None, :] + c_im[None] * p_re[:, :, None, :]
    kern = (jnp.einsum("tgap,gph->tgah", cp_re[:t], bb_re, precision=hp)
            - jnp.einsum("tgap,gph->tgah", cp_im[:t], bb_im, precision=hp))
    same_g = lambda n_rows, per: ((jnp.arange(n_rows) // per) % S5_GROUPS)[:, None] == jnp.arange(S5_GROUPS)[None, :]
    g_in = same_g(S5_WIDTH, S5_GROUP).astype(F32)[:, :, None]
    kfull = (kern.transpose(0, 1, 3, 2).reshape(t, S5_WIDTH, 1, S5_GROUP) * g_in[None]).reshape(t, S5_WIDTH, S5_WIDTH)
    krev = jnp.concatenate([kfull[t - 1 - j] for j in range(t)], axis=0)
    rev_re, rev_im = jnp.stack(pw_re[t - 1::-1]), jnp.stack(pw_im[t - 1::-1])
    w2_re = rev_re[..., None] * bb_re[None] - rev_im[..., None] * bb_im[None]
    w2_im = rev_re[..., None] * bb_im[None] + rev_im[..., None] * bb_re[None]
    w2 = jnp.stack([w2_re, w2_im]).transpose(1, 2, 4, 0, 3).reshape(S5_ROW, 2, 1, S5_STATE)
    m2 = (w2 * same_g(S5_ROW, S5_GROUP).astype(F32)[:, None, :, None]).reshape(S5_ROW, S5_FLAT)
    cp3 = jnp.stack([cp_re[1:], -cp_im[1:]]).transpose(0, 2, 4, 1, 3).reshape(S5_FLAT, t, 1, S5_GROUP)
    m3 = (cp3 * same_g(S5_FLAT, S5_STATE).astype(F32)[:, None, :, None]).reshape(S5_FLAT, S5_ROW)
    ab = jnp.concatenate([p_re[t].reshape(1, -1), p_im[t].reshape(1, -1)], axis=1)
    return {"krev": krev.astype(BF16), "m2": m2.astype(BF16), "m3": m3.astype(BF16), "ab": ab}


def _hg_consts(r_len):
    tt = HG_ROWS * r_len
    r = jnp.arange(tt)
    same = (r[:, None] // r_len) == (r[None, :] // r_len)
    tril = (same & (r[None, :] <= r[:, None])).astype(BF16)
    hd = jnp.arange(HG_WIDTH) // HG_DK
    hmask = (hd[:, None] == hd[None, :]).astype(F32)
    return {"tril": tril, "blk": same.astype(BF16), "hmask": hmask}


def _rope_tables(pos):
    half = MLA_ROPE // 2
    inv = ROPE_THETA ** (-jnp.arange(half, dtype=F32) / half)
    ang = pos.astype(F32)[:, None] * inv[None, :]
    reps = LANES // half
    return jnp.tile(jnp.cos(ang), (1, reps)), jnp.tile(jnp.sin(ang), (1, reps))


def _state_to_bd(s):
    eye_h = jnp.eye(HG_HEADS, dtype=F32)
    return jnp.einsum("bhdv,hk->bhvkd", s, eye_h).reshape(s.shape[0], HG_WIDTH, HG_WIDTH)


def _state_from_bd(s):
    b = s.shape[0]
    s5 = s.reshape(b, HG_HEADS, HG_DV, HG_HEADS, HG_DK)
    return jnp.stack([s5[:, h, :, h, :] for h in range(HG_HEADS)], axis=1).swapaxes(-1, -2)


def _layer(x, lw, sw, l, cos, sin, lb, hgc, ones_bd, final_g, final, *, prompt, n_seq, s5_h0, hg_s0, kv_past,
           pe_past):
    t = x.shape[0]
    seq = t // n_seq
    tm = ROW_TILE
    u, u_rows, hq, hf, hi, hg, c_kv, k_pe, qp, kk, vt = _in_call(x, lw, l, cos, sin, tm)

    if prompt:
        ys, s5_fin = _s5_prompt_call(u_rows, sw, l, min(256, t // S5_T))
    else:
        ys, s5_fin = _s5_sample_call(u_rows.reshape(n_seq, seq * S5_WIDTH), s5_h0, sw, l)
        ys = ys.reshape(t, 2, LANES).swapaxes(0, 1)

    r_len = 32 if prompt else seq
    oa = _hg_intra_call(hq, hf, hi, lb, l, ones_bd, r_len)
    ob, hg_fin = _hg_inter_call(hq, hf, hi, lb, l, hgc, hg_s0, r_len, carry=prompt)

    if prompt:
        kb = min(KEY_BLOCK_PROMPT, t)
        mla = _attn_call(qp, kk.reshape(t // kb, kb, QK_WIDTH), vt, lw["w_uv"], l, ATTN_QUERIES)
    else:
        mla = _attn_sample_call(qp, kv_past, pe_past, c_kv, k_pe, lw["w_uv"], l)

    x_new = _out_call(x, ys, u, oa, ob, hg, mla, lw, l, final_g, tm, final)
    return x_new, c_kv, k_pe, hg_fin, s5_fin


def kernel(x_prompt, x_sample, cache_mla_kv, cache_mla_pe, state_hgrn, state_s5_re, state_s5_im, norm1_g, w_in, s5_lambda_re, s5_lambda_im, s5_log_dt, s5_b_re, s5_b_im, s5_c_re, s5_c_im, s5_d, s5_w_glu, s5_b_glu, hgrn_lb_logits, mla_q_norm_g, mla_w_uq, mla_kv_norm_g, mla_w_uk, mla_w_uv, out_norm_g, w_out, norm2_g, w_up, w_down, final_norm_g):
    depth = w_in.shape[0]
    bp, lp = x_prompt.shape[0], x_prompt.shape[1]
    bs, ls = x_sample.shape[0], x_sample.shape[1]
    past = cache_mla_kv.shape[2]
    assert bp == 1 and ls == 2 * S5_T and bs % HG_ROWS == 0
    assert past % CHUNK == 0 and ls <= CHUNK

    cos_p, sin_p = _rope_tables(jnp.arange(lp, dtype=jnp.int32))
    cos_s, sin_s = _rope_tables(past + jnp.arange(ls, dtype=jnp.int32))
    cos_s, sin_s = jnp.tile(cos_s, (bs, 1)), jnp.tile(sin_s, (bs, 1))

    lb_p = jax.nn.softmax(hgrn_lb_logits.astype(F32), axis=0)
    lb_all = jnp.cumsum(lb_p, axis=0) - lb_p[0]
    hgc_p, hgc_s = _hg_consts(32), _hg_consts(ls)
    hd = jnp.arange(HG_WIDTH) // HG_DK
    ones_bd = (hd[:, None] == hd[None, :]).astype(BF16)
    rows = lambda v: v.reshape(depth, 1, -1).astype(F32)
    final_g = final_norm_g.reshape(1, -1).astype(F32)

    w_in_p, w_uq_p, wuk_bd, place, w_uv_p = _prep_mla(w_in, mla_w_uq, mla_w_uk, mla_w_uv)
    lw = {
        "norm1_g": rows(norm1_g), "w_in": w_in_p, "q_norm_g": rows(mla_q_norm_g), "w_uq": w_uq_p,
        "w_uk": wuk_bd, "place": place, "kv_norm_g": rows(mla_kv_norm_g), "w_uv": w_uv_p,
        "s5_d": rows(s5_d), "s5_w_glu": s5_w_glu.astype(BF16), "s5_b_glu": rows(s5_b_glu),
        "out_norm_g": rows(out_norm_g), "w_out": w_out.astype(BF16), "norm2_g": rows(norm2_g),
        "w_up": w_up.astype(BF16), "w_down": w_down.astype(BF16),
    }
    sw = jax.vmap(_prep_s5)(s5_lambda_re, s5_lambda_im, s5_log_dt, s5_b_re, s5_b_im, s5_c_re, s5_c_im)
    lb = rows(lb_all)
    s5_h0 = jnp.concatenate([state_s5_re.reshape(depth, bs, -1), state_s5_im.reshape(depth, bs, -1)], axis=-1)
    hg_s0 = _state_to_bd(state_hgrn.reshape((depth * bs,) + state_hgrn.shape[2:])).reshape(
        depth, bs, HG_WIDTH, HG_WIDTH)
    hg_zero = jnp.zeros((1, HG_WIDTH, HG_WIDTH), F32)

    xp = x_prompt.reshape(bp * lp, D_MODEL)
    xs = x_sample.reshape(bs * ls, D_MODEL)
    outs_p, outs_s = [], []
    for l in range(depth):
        final = l == depth - 1
        xp, a, b, c, d = _layer(xp, lw, sw, l, cos_p, sin_p, lb, hgc_p, ones_bd, final_g, final, prompt=True,
                                n_seq=1, s5_h0=None, hg_s0=hg_zero, kv_past=None, pe_past=None)
        outs_p.append((a, b, c, d))
        xs, a, b, c, d = _layer(xs, lw, sw, l, cos_s, sin_s, lb, hgc_s, ones_bd, final_g, final, prompt=False,
                                n_seq=bs, s5_h0=s5_h0, hg_s0=hg_s0, kv_past=cache_mla_kv, pe_past=cache_mla_pe)
        outs_s.append((a, b, c, d))

    def gather(outs, nb, sl):
        kv = jnp.stack([o[0].reshape(nb, sl, MLA_KV_RANK) for o in outs])
        pe = jnp.stack([o[1].reshape(nb, sl, MLA_ROPE) for o in outs])
        hg = jnp.stack([_state_from_bd(o[2]) for o in outs])
        half = S5_FLAT // 2
        re = jnp.stack([o[3][:, :half].reshape(nb, S5_GROUPS, S5_STATE) for o in outs])
        im = jnp.stack([o[3][:, half:].reshape(nb, S5_GROUPS, S5_STATE) for o in outs])
        return kv, pe, hg, re, im

    p_kv, p_pe, p_hg, p_re, p_im = gather(outs_p, bp, lp)
    s_kv, s_pe, s_hg, s_re, s_im = gather(outs_s, bs, ls)
    return (xp.reshape(bp, lp, D_MODEL), xs.reshape(bs, ls, D_MODEL),
            p_kv, p_pe, p_hg, p_re, p_im, s_kv, s_pe, s_hg, s_re, s_im)
```

```python
import functools
import math

import jax
import jax.numpy as jnp
from jax import lax
from jax.experimental import pallas as pl
from jax.experimental.pallas import tpu as pltpu

F32 = jnp.float32
BF16 = jnp.bfloat16

D_MODEL = 1024
CHUNK = 64
CHUNK_SHIFT = CHUNK.bit_length() - 1
EPS = 1e-5
NEG_BIG = -1e30

S5_WIDTH = 256
S5_GROUP = 16
S5_GROUPS = 16
S5_STATE = 64
S5_T = 8
S5_ROW = S5_T * S5_WIDTH
S5_FLAT = 2 * S5_GROUPS * S5_STATE

HG_HEADS = 4
HG_DK = 64
HG_DV = 64
HG_WIDTH = 256
HG_ROWS = 16
HG_PITCH_PAD = 4

MLA_HEADS = 8
MLA_Q_RANK = 256
MLA_KV_RANK = 128
MLA_NOPE = 64
MLA_ROPE = 32
MLA_V = 64
MLA_WIDTH = 512
ROPE_THETA = 10000.0
ROW_TILE = 512
KEY_BLOCK_PROMPT = 512
QK_WIDTH = 256
LOG2E = 1.4426950408889634
ATTN_COLS = 512
ATTN_QUERIES = 512

D_FF = 4096
FF_CHUNK = 1024
IN_PAD = 1920

LANES = 128
VMEM_LIMIT = 56 * 1024 * 1024


def _cparams(*sem):
    return pltpu.CompilerParams(dimension_semantics=sem, vmem_limit_bytes=VMEM_LIMIT)


def _const_spec(shape):
    nd = len(shape)
    return pl.BlockSpec(shape, lambda *_: (0,) * nd, pipeline_mode=pl.Buffered(1))


def _layer_spec(shape, l):
    nd = len(shape)
    return pl.BlockSpec((None,) + tuple(shape), lambda *_: (l,) + (0,) * nd, pipeline_mode=pl.Buffered(1))


def _rms(x, g):
    y = x * lax.rsqrt(jnp.mean(x * x, axis=-1, keepdims=True) + EPS)
    return y * g


def _dot(a, b):
    return jnp.dot(a, b, preferred_element_type=F32)


def _dot_t0(a, b):
    return lax.dot_general(a, b, (((0,), (0,)), ((), ())), preferred_element_type=F32)


def _dot_t1(a, b):
    return lax.dot_general(a, b, (((1,), (1,)), ((), ())), preferred_element_type=F32)


def _split3(x):
    hi = x.astype(BF16)
    r1 = x - hi.astype(F32)
    mid = r1.astype(BF16)
    lo = (r1 - mid.astype(F32)).astype(BF16)
    return hi, mid, lo


def _in_kernel(x_ref, g1_ref, win_ref, qg_ref, wuq_ref, wuk_ref, place_ref, kvg_ref, cos_ref, sin_ref,
               u_ref, urow_ref, hq_ref, hf_ref, hi_ref, hg_ref, ckv_ref, kpe_ref, qp_ref, kk_ref, vt_ref, u_scr):
    h = _rms(x_ref[...], g1_ref[...])
    proj = _dot(h.astype(BF16), win_ref[...])
    u_ref[...] = proj[:, 0:256]
    n_row = proj.shape[0] // S5_T
    for hv in range(2):
        u_scr[hv] = proj[:, hv * LANES:(hv + 1) * LANES]
    for s in range(S5_T):
        for hv in range(2):
            lo = s * S5_WIDTH + hv * LANES
            urow_ref[:, lo:lo + LANES] = u_scr[hv, pl.ds(s, n_row, stride=S5_T), :]
    hq_ref[...] = proj[:, 256:512]
    hf_ref[...] = proj[:, 512:768]
    hi_ref[...] = proj[:, 768:1024]
    hg_ref[...] = proj[:, 1024:1280]

    cos = cos_ref[...]
    sin = sin_ref[...]
    cos2 = jnp.concatenate([cos, cos], axis=-1)
    sin2 = jnp.concatenate([sin, sin], axis=-1)

    cqn = _rms(proj[:, 1280:1536], qg_ref[...])
    q = _dot(cqn.astype(BF16), wuq_ref[...])
    scale = (MLA_NOPE + MLA_ROPE) ** -0.5 * LOG2E
    q_pe = (q[:, 512:768] * cos2 + q[:, 768:1024] * sin2) * scale
    q_lat = _dot(q[:, 0:512].astype(BF16), wuk_ref[...]) * scale
    pe_pl = _dot(q_pe.astype(BF16), place_ref[...])
    for hd in range(MLA_HEADS):
        qp_ref[:, hd * 256:hd * 256 + 128] = q_lat[:, hd * 128:(hd + 1) * 128].astype(BF16)
        qp_ref[:, hd * 256 + 128:(hd + 1) * 256] = pe_pl[:, hd * 128:(hd + 1) * 128].astype(BF16)

    c_kv = _rms(proj[:, 1536:1664], kvg_ref[...])
    ckv_ref[...] = c_kv
    kpe = proj[:, 1664:1792] * cos + proj[:, 1792:1920] * sin
    kpe_ref[...] = kpe[:, 0:MLA_ROPE]
    kk_ref[...] = jnp.concatenate([c_kv, kpe], axis=-1).astype(BF16)
    vt_ref[0] = c_kv.T.astype(BF16)


def _in_call(x, lw, l, cos, sin, tm):
    t = x.shape[0]
    row = lambda w: pl.BlockSpec((tm, w), lambda i: (i, 0))
    outs = [jax.ShapeDtypeStruct((t, 256), F32), jax.ShapeDtypeStruct((t // S5_T, S5_ROW), F32)] + [
        jax.ShapeDtypeStruct((t, 256), F32)] * 4 + [
        jax.ShapeDtypeStruct((t, MLA_KV_RANK), F32),
        jax.ShapeDtypeStruct((t, MLA_ROPE), F32),
        jax.ShapeDtypeStruct((t, MLA_HEADS * QK_WIDTH), BF16),
        jax.ShapeDtypeStruct((t, QK_WIDTH), BF16),
        jax.ShapeDtypeStruct((t // tm, MLA_KV_RANK, tm), BF16),
    ]
    out_specs = [row(256), pl.BlockSpec((tm // S5_T, S5_ROW), lambda i: (i, 0))] + [row(256)] * 4 + [
        row(MLA_KV_RANK), row(MLA_ROPE), row(MLA_HEADS * QK_WIDTH), row(QK_WIDTH),
        pl.BlockSpec((1, MLA_KV_RANK, tm), lambda i: (i, 0, 0))]
    return pl.pallas_call(
        _in_kernel,
        grid=(t // tm,),
        in_specs=[row(D_MODEL), _layer_spec((1, D_MODEL), l), _layer_spec((D_MODEL, IN_PAD), l),
                  _layer_spec((1, MLA_Q_RANK), l), _layer_spec((MLA_Q_RANK, 1024), l), _layer_spec((512, 1024), l),
                  _const_spec((256, 1024)), _layer_spec((1, MLA_KV_RANK), l), row(LANES), row(LANES)],
        out_specs=out_specs,
        out_shape=outs,
        scratch_shapes=[pltpu.VMEM((2, tm, LANES), F32)],
        compiler_params=_cparams("arbitrary"),
    )(x, lw["norm1_g"], lw["w_in"], lw["q_norm_g"], lw["w_uq"], lw["w_uk"], lw["place"], lw["kv_norm_g"],
      cos, sin)


def _s5_expand(kd_ref, w2d_ref, cp3d_ref, krev_scr, m2_scr, m3_scr):
    assert S5_ROW == S5_FLAT
    lane = lax.broadcasted_iota(jnp.int32, (S5_ROW, LANES), 1)
    row = lax.broadcasted_iota(jnp.int32, (S5_ROW, LANES), 0)
    g_in = (row // S5_GROUP) % S5_GROUPS
    g_st = (row // S5_STATE) % S5_GROUPS
    per_state = LANES // S5_STATE
    per_chan = LANES // S5_GROUP
    for hv in range(S5_WIDTH // LANES):
        keep = g_in == hv * per_chan + lane // S5_GROUP
        krev_scr[:, hv * LANES:(hv + 1) * LANES] = jnp.where(keep, kd_ref[...], 0.0).astype(BF16)
    for ri in range(2):
        src = w2d_ref[:, ri * LANES:(ri + 1) * LANES]
        for k in range(S5_GROUPS // per_state):
            keep = g_in == k * per_state + lane // S5_STATE
            lo = (ri * (S5_GROUPS // per_state) + k) * LANES
            m2_scr[:, lo:lo + LANES] = jnp.where(keep, src, 0.0).astype(BF16)
    for t in range(S5_T):
        src = cp3d_ref[:, t * LANES:(t + 1) * LANES]
        for hv in range(S5_WIDTH // LANES):
            keep = g_st == hv * per_chan + lane // S5_GROUP
            lo = t * S5_WIDTH + hv * LANES
            m3_scr[:, lo:lo + LANES] = jnp.where(keep, src, 0.0).astype(BF16)


def _s5_in_row(u, krev_ref):
    w = S5_WIDTH
    parts = [_dot(u[:, 0:(t + 1) * w], krev_ref[(S5_T - 1 - t) * w:S5_T * w, :]) for t in range(S5_T)]
    return jnp.concatenate(parts, axis=1)


def _s5_prompt_kernel(u_ref, kd_ref, w2d_ref, cp3d_ref, ab_ref, y_ref, hfin_ref,
                      s_scr, hp_scr, h_scr, krev_ref, m2_ref, m3_ref):
    half = S5_FLAT // 2
    tn = u_ref.shape[0]

    @pl.when(pl.program_id(0) == 0)
    def _():
        h_scr[...] = jnp.zeros_like(h_scr)
        _s5_expand(kd_ref, w2d_ref, cp3d_ref, krev_ref, m2_ref, m3_ref)

    u = u_ref[...].astype(BF16)
    s_scr[...] = _dot(u, m2_ref[...])
    ar = ab_ref[:, 0:half]
    ai = ab_ref[:, half:S5_FLAT]

    def body(i, carry):
        hr, hi = carry
        hp_scr[pl.ds(i, 1), 0:half] = hr
        hp_scr[pl.ds(i, 1), half:S5_FLAT] = hi
        sr = s_scr[pl.ds(i, 1), 0:half]
        si = s_scr[pl.ds(i, 1), half:S5_FLAT]
        return ar * hr - ai * hi + sr, ar * hi + ai * hr + si

    hr, hi = lax.fori_loop(0, tn, body, (h_scr[:, 0:half], h_scr[:, half:S5_FLAT]), unroll=8)
    h_scr[:, 0:half] = hr
    h_scr[:, half:S5_FLAT] = hi
    y = _s5_in_row(u, krev_ref) + _dot(hp_scr[...].astype(BF16), m3_ref[...])
    for s in range(S5_T):
        for hv in range(2):
            lo = s * S5_WIDTH + hv * LANES
            y_ref[hv, pl.ds(s, tn, stride=S5_T), :] = y[:, lo:lo + LANES]
    hfin_ref[...] = h_scr[...]


def _s5_matrix_scratch():
    return [pltpu.VMEM((S5_ROW, S5_WIDTH), BF16), pltpu.VMEM((S5_ROW, S5_FLAT), BF16),
            pltpu.VMEM((S5_FLAT, S5_ROW), BF16)]


def _s5_prompt_call(u_rows, sw, l, tn):
    n = u_rows.shape[0]
    return pl.pallas_call(
        _s5_prompt_kernel,
        grid=(n // tn,),
        in_specs=[pl.BlockSpec((tn, S5_ROW), lambda i: (i, 0)),
                  _layer_spec((S5_ROW, LANES), l), _layer_spec((S5_ROW, 2 * LANES), l),
                  _layer_spec((S5_FLAT, S5_T * LANES), l), _layer_spec((1, S5_FLAT), l)],
        out_specs=[pl.BlockSpec((2, tn * S5_T, LANES), lambda i: (0, i, 0)),
                   pl.BlockSpec((1, S5_FLAT), lambda i: (0, 0))],
        out_shape=[jax.ShapeDtypeStruct((2, n * S5_T, LANES), F32), jax.ShapeDtypeStruct((1, S5_FLAT), F32)],
        scratch_shapes=[pltpu.VMEM((tn, S5_FLAT), F32), pltpu.VMEM((tn, S5_FLAT), F32),
                        pltpu.VMEM((1, S5_FLAT), F32)] + _s5_matrix_scratch(),
        compiler_params=_cparams("arbitrary"),
    )(u_rows, sw["kd"], sw["w2d"], sw["cp3d"], sw["ab"])


def _s5_sample_kernel(u_ref, h0_ref, kd_ref, w2d_ref, cp3d_ref, ab_ref, y_ref, hfin_ref,
                      krev_ref, m2_ref, m3_ref, *, n_rows):
    half = S5_FLAT // 2
    _s5_expand(kd_ref, w2d_ref, cp3d_ref, krev_ref, m2_ref, m3_ref)
    ar = ab_ref[:, 0:half]
    ai = ab_ref[:, half:S5_FLAT]
    us = [u_ref[:, c * S5_ROW:(c + 1) * S5_ROW].astype(BF16) for c in range(n_rows)]
    u_all = jnp.concatenate(us, axis=0)
    b = us[0].shape[0]
    s_all = _dot(u_all, m2_ref[...])
    y1_all = _s5_in_row(u_all, krev_ref)
    hr = h0_ref[:, 0:half]
    hi = h0_ref[:, half:S5_FLAT]
    enter = []
    for c in range(n_rows):
        enter.append(jnp.concatenate([hr, hi], axis=-1))
        sr = s_all[c * b:(c + 1) * b, 0:half]
        si = s_all[c * b:(c + 1) * b, half:S5_FLAT]
        hr, hi = ar * hr - ai * hi + sr, ar * hi + ai * hr + si
    y_all = y1_all + _dot(jnp.concatenate(enter, axis=0).astype(BF16), m3_ref[...])
    for c in range(n_rows):
        y_ref[:, c * S5_ROW:(c + 1) * S5_ROW] = y_all[c * b:(c + 1) * b, :]
    hfin_ref[:, 0:half] = hr
    hfin_ref[:, half:S5_FLAT] = hi


def _s5_sample_call(u_seq, h0, sw, l):
    b, w = u_seq.shape
    n_rows = w // S5_ROW
    full = lambda shape: pl.BlockSpec(shape, lambda i: (0,) * len(shape))
    return pl.pallas_call(
        functools.partial(_s5_sample_kernel, n_rows=n_rows),
        grid=(1,),
        in_specs=[full((b, w)), _layer_spec((b, S5_FLAT), l), _layer_spec((S5_ROW, LANES), l),
                  _layer_spec((S5_ROW, 2 * LANES), l), _layer_spec((S5_FLAT, S5_T * LANES), l),
                  _layer_spec((1, S5_FLAT), l)],
        out_specs=[full((b, w)), full((b, S5_FLAT))],
        out_shape=[jax.ShapeDtypeStruct((b, w), F32), jax.ShapeDtypeStruct((b, S5_FLAT), F32)],
        scratch_shapes=_s5_matrix_scratch(),
        compiler_params=_cparams("arbitrary"),
    )(u_seq, h0, sw["kd"], sw["w2d"], sw["cp3d"], sw["ab"])


def _hg_gates(hq, hf, lb):
    sig = jax.nn.sigmoid(hf)
    f = lb + (1.0 - lb) * sig
    k = (1.0 - lb) * jax.nn.sigmoid(-hf)
    qf = hq * jax.nn.sigmoid(hq)
    return qf, k, f


def _hg_intra_kernel(hq_ref, hf_ref, hi_ref, lb_ref, ones_ref, o_ref, pad_scr, out_scr, k_scr, f_scr, *, r_len):
    lb = lb_ref[...]
    ones_bd = ones_ref[...]
    pitch = r_len + HG_PITCH_PAD
    for a, ref in enumerate((hq_ref, hf_ref, hi_ref)):
        for hv in range(2):
            for j in range(HG_ROWS):
                pad_scr[a, hv, pl.ds(j * pitch, r_len), :] = ref[j * r_len:(j + 1) * r_len, hv * LANES:(hv + 1) * LANES]

    def slab(a, r):
        return jnp.concatenate([pad_scr[a, hv, pl.ds(r, HG_ROWS, stride=pitch), :] for hv in range(2)], axis=-1)

    for r in range(r_len):
        _, k, f = _hg_gates(slab(0, r), slab(1, r), lb)
        k_scr[r] = k
        f_scr[r] = f
    for r in range(r_len):
        hq = slab(0, r)
        qp = hq * jax.nn.sigmoid(hq)
        terms = []
        for s in range(r, -1, -1):
            terms.append((qp * k_scr[s]).astype(BF16))
            if s > 0:
                qp = qp * f_scr[s]
        att = _dot(jnp.concatenate(terms, axis=0), ones_bd)
        acc = None
        for j, s in enumerate(range(r, -1, -1)):
            part = att[j * HG_ROWS:(j + 1) * HG_ROWS, :] * slab(2, s)
            acc = part if acc is None else acc + part
        for hv in range(2):
            out_scr[hv, pl.ds(r, HG_ROWS, stride=pitch), :] = acc[:, hv * LANES:(hv + 1) * LANES]
    for hv in range(2):
        for j in range(HG_ROWS):
            o_ref[j * r_len:(j + 1) * r_len, hv * LANES:(hv + 1) * LANES] = out_scr[hv, pl.ds(j * pitch, r_len), :]


def _hg_intra_call(hq, hf, hi, lb, l, ones_bd, r_len):
    t = hq.shape[0]
    tt = HG_ROWS * r_len
    pad_rows = HG_ROWS * (r_len + HG_PITCH_PAD)
    spec = pl.BlockSpec((tt, HG_WIDTH), lambda i: (i, 0))
    return pl.pallas_call(
        functools.partial(_hg_intra_kernel, r_len=r_len),
        grid=(t // tt,),
        in_specs=[spec, spec, spec, _layer_spec((1, HG_WIDTH), l), _const_spec((HG_WIDTH, HG_WIDTH))],
        out_specs=spec,
        out_shape=jax.ShapeDtypeStruct(hq.shape, F32),
        scratch_shapes=[pltpu.VMEM((3, 2, pad_rows, LANES), F32), pltpu.VMEM((2, pad_rows, LANES), F32),
                        pltpu.VMEM((r_len, HG_ROWS, HG_WIDTH), F32), pltpu.VMEM((r_len, HG_ROWS, HG_WIDTH), F32)],
        compiler_params=_cparams("arbitrary"),
    )(hq, hf, hi, lb, ones_bd)


def _hg_inter_kernel(hq_ref, hf_ref, hi_ref, lb_ref, tril_ref, blk_ref, hmask_ref, s0_ref,
                     o_ref, sout_ref, s_scr, *, r_len, carry):
    n_sub = HG_ROWS
    if carry:
        @pl.when(pl.program_id(0) == 0)
        def _():
            s_scr[...] = s0_ref[0]

    qf, k, f = _hg_gates(hq_ref[...], hf_ref[...], lb_ref[...])
    g = jnp.log(f)
    g3 = _split3(g)
    tril = tril_ref[...]
    blk = blk_ref[...]
    bl = _dot(tril, g3[0]) + _dot(tril, g3[1]) + _dot(tril, g3[2])
    bsum = _dot(blk, g3[0]) + _dot(blk, g3[1]) + _dot(blk, g3[2])
    q_in = (qf * jnp.exp(bl)).astype(BF16)
    k_out = (k * jnp.exp(bsum - bl)).astype(BF16)
    v = hi_ref[...].astype(BF16)
    hmask = hmask_ref[...]
    decay = jnp.exp(bsum)
    for j in range(n_sub):
        sl = slice(j * r_len, (j + 1) * r_len)
        s_in = s_scr[...] if carry else s0_ref[j]
        o_ref[sl, :] = _dot_t1(q_in[sl], s_in.astype(BF16))
        w_new = _dot_t0(v[sl], k_out[sl]) * hmask
        s_new = decay[j * r_len:j * r_len + 1, :] * s_in + w_new
        if carry:
            s_scr[...] = s_new
        else:
            sout_ref[j] = s_new
    if carry:
        sout_ref[0] = s_scr[...]


def _hg_inter_call(hq, hf, hi, lb, l, consts, s0, r_len, carry):
    t = hq.shape[0]
    tt = HG_ROWS * r_len
    spec = pl.BlockSpec((tt, HG_WIDTH), lambda i: (i, 0))
    if carry:
        s_in_spec = pl.BlockSpec((1, HG_WIDTH, HG_WIDTH), lambda i: (0, 0, 0))
        s_out_spec = s_in_spec
    else:
        s_in_spec = pl.BlockSpec((None, HG_ROWS, HG_WIDTH, HG_WIDTH), lambda i: (l, i, 0, 0))
        s_out_spec = pl.BlockSpec((HG_ROWS, HG_WIDTH, HG_WIDTH), lambda i: (i, 0, 0))
    return pl.pallas_call(
        functools.partial(_hg_inter_kernel, r_len=r_len, carry=carry),
        grid=(t // tt,),
        in_specs=[spec, spec, spec, _layer_spec((1, HG_WIDTH), l), _const_spec((tt, tt)), _const_spec((tt, tt)),
                  _const_spec((HG_WIDTH, HG_WIDTH)), s_in_spec],
        out_specs=[spec, s_out_spec],
        out_shape=[jax.ShapeDtypeStruct((t, HG_WIDTH), F32), jax.ShapeDtypeStruct(s0.shape[-3:], F32)],
        scratch_shapes=[pltpu.VMEM((HG_WIDTH, HG_WIDTH), F32)],
        compiler_params=_cparams("arbitrary"),
    )(hq, hf, hi, lb, consts["tril"], consts["blk"], consts["hmask"], s0)


def _attn_project(o_lat, wuv_ref, tq):
    full = _dot(o_lat, wuv_ref[...])
    lane_head = lax.broadcasted_iota(jnp.int32, (tq, MLA_WIDTH), 1) // MLA_V
    out = jnp.zeros((tq, MLA_WIDTH), F32)
    for hd in range(MLA_HEADS):
        out = out + jnp.where(lane_head == hd, full[hd * tq:(hd + 1) * tq, :], 0.0)
    return out


def _attn_kernel(qp_ref, k_ref, vt_ref, wuv_ref, o_ref, q_scr, s_buf, mb_buf, m_scr, l_scr, acc_scr, bias_scr,
                 *, tq, kb):
    i = pl.program_id(0)
    ncol = MLA_HEADS * tq

    @pl.when(i == 0)
    def _():
        k_chunk = lax.broadcasted_iota(jnp.int32, (kb, tq), 0) >> CHUNK_SHIFT
        q_chunk = lax.broadcasted_iota(jnp.int32, (kb, tq), 1) >> CHUNK_SHIFT
        bias_scr[...] = jnp.where(k_chunk <= q_chunk, 0.0, NEG_BIG)

    for hd in range(MLA_HEADS):
        q_scr[hd * tq:(hd + 1) * tq, :] = qp_ref[:, hd * QK_WIDTH:(hd + 1) * QK_WIDTH]
    m_scr[...] = jnp.full_like(m_scr, NEG_BIG)
    l_scr[...] = jnp.zeros_like(l_scr)
    acc_scr[...] = jnp.zeros_like(acc_scr)
    q0 = i * tq
    n_full = q0 // kb

    cw = min(ATTN_COLS, ncol)
    n_chunks = ncol // cw
    vt_w = vt_ref.shape[2]
    vt_per = kb // vt_w

    def scores(b, c, diagonal):
        cols = slice(c * cw, (c + 1) * cw)
        s = _dot_t1(k_ref[b], q_scr[cols, :])
        if diagonal is not None:
            off = (c * cw) % tq
            bias = bias_scr[:, off:off + cw]
            s = s + (bias if diagonal is True else jnp.where(diagonal, bias, 0.0))
        s_buf[c] = s
        mb_buf[c] = jnp.max(s, axis=0, keepdims=True)

    def values(b, c):
        cols = slice(c * cw, (c + 1) * cw)
        m_old = m_scr[:, cols]
        m_new = jnp.maximum(m_old, mb_buf[c])
        alpha = jnp.exp2(m_old - m_new)
        p = jnp.exp2(s_buf[c] - m_new)
        l_scr[:, cols] = alpha * l_scr[:, cols] + jnp.sum(p, axis=0, keepdims=True)
        pb = p.astype(BF16)
        pv = _dot(vt_ref[b * vt_per], pb[0:vt_w, :])
        for j in range(1, vt_per):
            pv = pv + _dot(vt_ref[b * vt_per + j], pb[j * vt_w:(j + 1) * vt_w, :])
        acc_scr[:, cols] = alpha * acc_scr[:, cols] + pv
        m_scr[:, cols] = m_new

    def step(b, diagonal_next):
        for c in range(n_chunks):
            values(b, c)
            scores(b + 1, c, diagonal_next)

    for c in range(n_chunks):
        scores(0, c, n_full == 0)

    def body(b, carry):
        step(b, None)
        return carry

    lax.fori_loop(0, n_full - 1, body, 0)

    @pl.when(n_full >= 1)
    def _():
        step(n_full - 1, True)

    for c in range(n_chunks):
        values(n_full, c)

    o_lat = (acc_scr[...] / l_scr[...]).T.astype(BF16)
    o_ref[...] = _attn_project(o_lat, wuv_ref, tq)


def _attn_call(qp, k_all, vt_all, wuv, l, tq):
    t = qp.shape[0]
    n_blocks, kb = k_all.shape[0], k_all.shape[1]
    ncol = MLA_HEADS * tq
    cw = min(ATTN_COLS, ncol)
    assert tq == kb and tq % cw == 0 and kb % CHUNK == 0
    return pl.pallas_call(
        functools.partial(_attn_kernel, tq=tq, kb=kb),
        grid=(t // tq,),
        in_specs=[pl.BlockSpec((tq, MLA_HEADS * QK_WIDTH), lambda i: (i, 0)),
                  _const_spec((n_blocks, kb, QK_WIDTH)), _const_spec(vt_all.shape),
                  _layer_spec((MLA_KV_RANK, MLA_WIDTH), l)],
        out_specs=pl.BlockSpec((tq, MLA_WIDTH), lambda i: (i, 0)),
        out_shape=jax.ShapeDtypeStruct((t, MLA_WIDTH), F32),
        scratch_shapes=[pltpu.VMEM((ncol, QK_WIDTH), BF16), pltpu.VMEM((ncol // cw, kb, cw), F32),
                        pltpu.VMEM((ncol // cw, 1, cw), F32), pltpu.VMEM((1, ncol), F32), pltpu.VMEM((1, ncol), F32),
                        pltpu.VMEM((MLA_KV_RANK, ncol), F32), pltpu.VMEM((kb, tq), F32)],
        compiler_params=_cparams("arbitrary"),
    )(qp, k_all, vt_all, wuv)


def _attn_sample_kernel(qp_ref, kv_ref, pe_ref, ckv_ref, kpe_ref, wuv_ref, o_ref, *, tq):
    q = jnp.concatenate([qp_ref[:, hd * QK_WIDTH:(hd + 1) * QK_WIDTH] for hd in range(MLA_HEADS)], axis=0)
    q_lat = q[:, 0:MLA_KV_RANK]
    q_pe = q[:, MLA_KV_RANK:MLA_KV_RANK + MLA_ROPE]
    kv_old, pe_old = kv_ref[0].astype(BF16), pe_ref[0].astype(BF16)
    kv_new, pe_new = ckv_ref[0].astype(BF16), kpe_ref[0].astype(BF16)
    s_old = _dot_t1(q_lat, kv_old) + _dot_t1(q_pe, pe_old)
    s_new = _dot_t1(q_lat, kv_new) + _dot_t1(q_pe, pe_new)
    m = jnp.maximum(jnp.max(s_old, axis=-1, keepdims=True), jnp.max(s_new, axis=-1, keepdims=True))
    p_old = jnp.exp2(s_old - m)
    p_new = jnp.exp2(s_new - m)
    denom = jnp.sum(p_old, axis=-1, keepdims=True) + jnp.sum(p_new, axis=-1, keepdims=True)
    o_lat = (_dot(p_old.astype(BF16), kv_old) + _dot(p_new.astype(BF16), kv_new)) / denom
    o_ref[...] = _attn_project(o_lat.astype(BF16), wuv_ref, tq)


def _attn_sample_call(qp, kv_past, pe_past, c_kv, k_pe, wuv, l):
    _, n_seq, past, _ = kv_past.shape
    tq = qp.shape[0] // n_seq
    per_seq = lambda r, w: pl.BlockSpec((1, r, w), lambda i: (i, 0, 0))
    cached = lambda w: pl.BlockSpec((None, 1, past, w), lambda i: (l, i, 0, 0))
    return pl.pallas_call(
        functools.partial(_attn_sample_kernel, tq=tq),
        grid=(n_seq,),
        in_specs=[pl.BlockSpec((tq, MLA_HEADS * QK_WIDTH), lambda i: (i, 0)), cached(MLA_KV_RANK),
                  per_seq(past, MLA_ROPE), per_seq(tq, MLA_KV_RANK), per_seq(tq, MLA_ROPE),
                  _layer_spec((MLA_KV_RANK, MLA_WIDTH), l)],
        out_specs=pl.BlockSpec((tq, MLA_WIDTH), lambda i: (i, 0)),
        out_shape=jax.ShapeDtypeStruct((qp.shape[0], MLA_WIDTH), F32),
        compiler_params=_cparams("arbitrary"),
    )(qp, kv_past, pe_past, c_kv.reshape(n_seq, tq, MLA_KV_RANK), k_pe.reshape(n_seq, tq, MLA_ROPE), wuv)


def _out_kernel(x_ref, ys_ref, u_ref, oa_ref, ob_ref, hg_ref, mla_ref, d_ref, wglu_ref, bglu_ref, og_ref,
                wout_ref, g2_ref, wup_ref, wdn_ref, fg_ref, o_ref, *, final):
    y = jnp.concatenate([ys_ref[0], ys_ref[1]], axis=-1) + d_ref[...] * u_ref[...]
    z = jax.nn.gelu(y, approximate=True)
    s5 = z * jax.nn.sigmoid(_dot(z.astype(BF16), wglu_ref[...]) + bglu_ref[...])
    og = og_ref[...]
    hgate = hg_ref[...]
    mixed = jnp.concatenate([
        _rms(s5, og[:, 0:256]),
        _rms(oa_ref[...] + ob_ref[...], og[:, 256:512]) * (hgate * jax.nn.sigmoid(hgate)),
        _rms(mla_ref[...], og[:, 512:1024]),
    ], axis=-1)
    x1 = x_ref[...] + _dot(mixed.astype(BF16), wout_ref[...])
    h2 = _rms(x1, g2_ref[...]).astype(BF16)
    acc = x1
    for c in range(D_FF // FF_CHUNK):
        up = _dot(h2, wup_ref[:, c * FF_CHUNK:(c + 1) * FF_CHUNK])
        act = jnp.square(jnp.maximum(up, 0.0)).astype(BF16)
        acc = acc + _dot(act, wdn_ref[c * FF_CHUNK:(c + 1) * FF_CHUNK, :])
    if final:
        acc = _rms(acc, fg_ref[...])
    o_ref[...] = acc


def _out_call(x, ys, u, oa, ob, hg, mla, lw, l, final_g, tm, final):
    t = x.shape[0]
    row = lambda w: pl.BlockSpec((tm, w), lambda i: (i, 0))
    return pl.pallas_call(
        functools.partial(_out_kernel, final=final),
        grid=(t // tm,),
        in_specs=[row(D_MODEL), pl.BlockSpec((2, tm, LANES), lambda i: (0, i, 0)), row(256), row(256), row(256),
                  row(256), row(MLA_WIDTH),
                  _layer_spec((1, 256), l), _layer_spec((256, 256), l), _layer_spec((1, 256), l),
                  _layer_spec((1, D_MODEL), l), _layer_spec((D_MODEL, D_MODEL), l), _layer_spec((1, D_MODEL), l),
                  _layer_spec((D_MODEL, D_FF), l), _layer_spec((D_FF, D_MODEL), l), _const_spec((1, D_MODEL))],
        out_specs=row(D_MODEL),
        out_shape=jax.ShapeDtypeStruct((t, D_MODEL), F32),
        compiler_params=_cparams("arbitrary"),
    )(x, ys, u, oa, ob, hg, mla, lw["s5_d"], lw["s5_w_glu"], lw["s5_b_glu"], lw["out_norm_g"], lw["w_out"],
      lw["norm2_g"], lw["w_up"], lw["w_down"], final_g)


def _rot_cols(w):
    half = MLA_ROPE // 2
    return jnp.concatenate([-w[..., half:], w[..., :half]], axis=-1)


def _prep_mla(w_in, w_uq, w_uk, w_uv):
    nl = w_in.shape[0]
    kpe = w_in[..., 1664:1696]
    pad = jnp.zeros((nl, D_MODEL, 96), F32)
    w_in_p = jnp.concatenate([w_in[..., :1664], kpe, pad, _rot_cols(kpe), pad], axis=-1).astype(BF16)
    uq = w_uq.reshape(nl, MLA_Q_RANK, MLA_HEADS, MLA_NOPE + MLA_ROPE)
    nope = uq[..., :MLA_NOPE].reshape(nl, MLA_Q_RANK, 512)
    pe = uq[..., MLA_NOPE:]
    w_uq_p = jnp.concatenate([nope, pe.reshape(nl, MLA_Q_RANK, 256), _rot_cols(pe).reshape(nl, MLA_Q_RANK, 256)],
                             axis=-1).astype(BF16)
    eye_h = jnp.eye(MLA_HEADS, dtype=F32)
    wuk_bd = jnp.einsum("lchd,hk->lhdkc", w_uk, eye_h).reshape(nl, 512, 1024).astype(BF16)
    place = jnp.einsum("hk,rc->hrkc", eye_h, jnp.eye(MLA_ROPE, LANES, dtype=F32)).reshape(256, 1024).astype(BF16)
    return w_in_p, w_uq_p, wuk_bd, place, w_uv.reshape(nl, MLA_KV_RANK, MLA_WIDTH).astype(BF16)


def _prep_s5(lam_re, lam_im, log_dt, b_re, b_im, c_re, c_im):
    hp = lax.Precision.HIGHEST
    t = S5_T
    dt = jnp.exp(log_dt)[:, None]
    mag1 = jnp.exp(lam_re * dt)
    a_re, a_im = mag1 * jnp.cos(lam_im * dt), mag1 * jnp.sin(lam_im * dt)
    pw_re, pw_im = [jnp.ones_like(a_re)], [jnp.zeros_like(a_im)]
    for _ in range(t):
        pr, pi = pw_re[-1], pw_im[-1]
        pw_re.append(pr * a_re - pi * a_im)
        pw_im.append(pr * a_im + pi * a_re)
    p_re, p_im = jnp.stack(pw_re), jnp.stack(pw_im)
    den = lam_re * lam_re + lam_im * lam_im
    i_re, i_im = lam_re / den, -lam_im / den
    z_re = (a_re - 1.0) * i_re - a_im * i_im
    z_im = (a_re - 1.0) * i_im + a_im * i_re
    bb_re = z_re[..., None] * b_re - z_im[..., None] * b_im
    bb_im = z_re[..., None] * b_im + z_im[..., None] * b_re
    cp_re = c_re[None] * p_re[:, :, None, :] - c_im[None] * p_im[:, :, None, :]
    cp_im = c_re[None] * p_im[:, :, None, :] + c_im[None] * p_re[:, :, None, :]
    kern = (jnp.einsum("tgap,gph->tgah", cp_re[:t], bb_re, precision=hp)
            - jnp.einsum("tgap,gph->tgah", cp_im[:t], bb_im, precision=hp))
    kd = jnp.concatenate([kern[t - 1 - j].transpose(0, 2, 1) for j in range(t)], axis=0)
    kd = jnp.tile(kd.reshape(S5_ROW, S5_GROUP), (1, LANES // S5_GROUP))
    rev_re, rev_im = jnp.stack(pw_re[t - 1::-1]), jnp.stack(pw_im[t - 1::-1])
    w2_re = rev_re[..., None] * bb_re[None] - rev_im[..., None] * bb_im[None]
    w2_im = rev_re[..., None] * bb_im[None] + rev_im[..., None] * bb_re[None]
    w2 = jnp.stack([w2_re, w2_im]).transpose(1, 2, 4, 0, 3).reshape(S5_ROW, 2, S5_STATE)
    w2d = jnp.tile(w2, (1, 1, LANES // S5_STATE)).reshape(S5_ROW, 2 * LANES)
    cp3 = jnp.stack([cp_re[1:], -cp_im[1:]]).transpose(0, 2, 4, 1, 3).reshape(S5_FLAT, t, S5_GROUP)
    cp3d = jnp.tile(cp3, (1, 1, LANES // S5_GROUP)).reshape(S5_FLAT, t * LANES)
    ab = jnp.concatenate([p_re[t].reshape(1, -1), p_im[t].reshape(1, -1)], axis=1)
    return {"kd": kd, "w2d": w2d, "cp3d": cp3d, "ab": ab}


def _hg_consts(r_len):
    tt = HG_ROWS * r_len
    r = jnp.arange(tt)
    same = (r[:, None] // r_len) == (r[None, :] // r_len)
    tril = (same & (r[None, :] <= r[:, None])).astype(BF16)
    hd = jnp.arange(HG_WIDTH) // HG_DK
    hmask = (hd[:, None] == hd[None, :]).astype(F32)
    return {"tril": tril, "blk": same.astype(BF16), "hmask": hmask}


def _rope_tables(pos):
    half = MLA_ROPE // 2
    inv = ROPE_THETA ** (-jnp.arange(half, dtype=F32) / half)
    ang = pos.astype(F32)[:, None] * inv[None, :]
    reps = LANES // half
    return jnp.tile(jnp.cos(ang), (1, reps)), jnp.tile(jnp.sin(ang), (1, reps))


def _state_to_bd(s):
    out = jnp.zeros((s.shape[0], HG_WIDTH, HG_WIDTH), F32)
    for h in range(HG_HEADS):
        out = out.at[:, h * HG_DV:(h + 1) * HG_DV, h * HG_DK:(h + 1) * HG_DK].set(s[:, h].swapaxes(1, 2))
    return out


def _state_from_bd(s):
    blocks = [s[:, h * HG_DV:(h + 1) * HG_DV, h * HG_DK:(h + 1) * HG_DK].swapaxes(1, 2) for h in range(HG_HEADS)]
    return jnp.stack(blocks, axis=1)


def _layer(x, lw, sw, l, cos, sin, lb, hgc, ones_bd, final_g, final, *, prompt, n_seq, s5_h0, hg_s0, kv_past,
           pe_past):
    t = x.shape[0]
    seq = t // n_seq
    tm = ROW_TILE
    u, u_rows, hq, hf, hi, hg, c_kv, k_pe, qp, kk, vt = _in_call(x, lw, l, cos, sin, tm)

    if prompt:
        ys, s5_fin = _s5_prompt_call(u_rows, sw, l, min(256, t // S5_T))
    else:
        ys, s5_fin = _s5_sample_call(u_rows.reshape(n_seq, seq * S5_WIDTH), s5_h0, sw, l)
        ys = ys.reshape(t, 2, LANES).swapaxes(0, 1)

    r_len = 32 if prompt else seq
    oa = _hg_intra_call(hq, hf, hi, lb, l, ones_bd, r_len)
    ob, hg_fin = _hg_inter_call(hq, hf, hi, lb, l, hgc, hg_s0, r_len, carry=prompt)

    if prompt:
        kb = min(KEY_BLOCK_PROMPT, t)
        mla = _attn_call(qp, kk.reshape(t // kb, kb, QK_WIDTH), vt, lw["w_uv"], l, ATTN_QUERIES)
    else:
        mla = _attn_sample_call(qp, kv_past, pe_past[l], c_kv, k_pe, lw["w_uv"], l)

    x_new = _out_call(x, ys, u, oa, ob, hg, mla, lw, l, final_g, tm, final)
    return x_new, c_kv, k_pe, hg_fin, s5_fin


def kernel(x_prompt, x_sample, cache_mla_kv, cache_mla_pe, state_hgrn, state_s5_re, state_s5_im, norm1_g, w_in, s5_lambda_re, s5_lambda_im, s5_log_dt, s5_b_re, s5_b_im, s5_c_re, s5_c_im, s5_d, s5_w_glu, s5_b_glu, hgrn_lb_logits, mla_q_norm_g, mla_w_uq, mla_kv_norm_g, mla_w_uk, mla_w_uv, out_norm_g, w_out, norm2_g, w_up, w_down, final_norm_g):
    depth = w_in.shape[0]
    bp, lp = x_prompt.shape[0], x_prompt.shape[1]
    bs, ls = x_sample.shape[0], x_sample.shape[1]
    past = cache_mla_kv.shape[2]
    assert bp == 1 and ls == 2 * S5_T and bs % HG_ROWS == 0
    assert past % CHUNK == 0 and ls <= CHUNK

    cos_p, sin_p = _rope_tables(jnp.arange(lp, dtype=jnp.int32))
    cos_s, sin_s = _rope_tables(past + jnp.arange(ls, dtype=jnp.int32))
    cos_s, sin_s = jnp.tile(cos_s, (bs, 1)), jnp.tile(sin_s, (bs, 1))

    lb_p = jax.nn.softmax(hgrn_lb_logits.astype(F32), axis=0)
    lb_all = jnp.cumsum(lb_p, axis=0) - lb_p[0]
    hgc_p, hgc_s = _hg_consts(32), _hg_consts(ls)
    hd = jnp.arange(HG_WIDTH) // HG_DK
    ones_bd = (hd[:, None] == hd[None, :]).astype(BF16)
    rows = lambda v: v.reshape(depth, 1, -1).astype(F32)
    final_g = final_norm_g.reshape(1, -1).astype(F32)

    w_in_p, w_uq_p, wuk_bd, place, w_uv_p = _prep_mla(w_in, mla_w_uq, mla_w_uk, mla_w_uv)
    lw = {
        "norm1_g": rows(norm1_g), "w_in": w_in_p, "q_norm_g": rows(mla_q_norm_g), "w_uq": w_uq_p,
        "w_uk": wuk_bd, "place": place, "kv_norm_g": rows(mla_kv_norm_g), "w_uv": w_uv_p,
        "s5_d": rows(s5_d), "s5_w_glu": s5_w_glu.astype(BF16), "s5_b_glu": rows(s5_b_glu),
        "out_norm_g": rows(out_norm_g), "w_out": w_out.astype(BF16), "norm2_g": rows(norm2_g),
        "w_up": w_up.astype(BF16), "w_down": w_down.astype(BF16),
    }
    sw = jax.vmap(_prep_s5)(s5_lambda_re, s5_lambda_im, s5_log_dt, s5_b_re, s5_b_im, s5_c_re, s5_c_im)
    lb = rows(lb_all)
    s5_h0 = jnp.concatenate([state_s5_re.reshape(depth, bs, -1), state_s5_im.reshape(depth, bs, -1)], axis=-1)
    hg_s0 = _state_to_bd(state_hgrn.reshape((depth * bs,) + state_hgrn.shape[2:])).reshape(
        depth, bs, HG_WIDTH, HG_WIDTH)
    hg_zero = jnp.zeros((1, HG_WIDTH, HG_WIDTH), F32)

    xp = x_prompt.reshape(bp * lp, D_MODEL)
    xs = x_sample.reshape(bs * ls, D_MODEL)
    outs_p, outs_s = [], []
    for l in range(depth):
        final = l == depth - 1
        xp, a, b, c, d = _layer(xp, lw, sw, l, cos_p, sin_p, lb, hgc_p, ones_bd, final_g, final, prompt=True,
                                n_seq=1, s5_h0=None, hg_s0=hg_zero, kv_past=None, pe_past=None)
        outs_p.append((a, b, c, d))
        xs, a, b, c, d = _layer(xs, lw, sw, l, cos_s, sin_s, lb, hgc_s, ones_bd, final_g, final, prompt=False,
                                n_seq=bs, s5_h0=s5_h0, hg_s0=hg_s0, kv_past=cache_mla_kv, pe_past=cache_mla_pe)
        outs_s.append((a, b, c, d))

    def gather(outs, nb, sl):
        kv = jnp.stack([o[0].reshape(nb, sl, MLA_KV_RANK) for o in outs])
        pe = jnp.stack([o[1].reshape(nb, sl, MLA_ROPE) for o in outs])
        hg = jnp.stack([_state_from_bd(o[2]) for o in outs])
        half = S5_FLAT // 2
        re = jnp.stack([o[3][:, :half].reshape(nb, S5_GROUPS, S5_STATE) for o in outs])
        im = jnp.stack([o[3][:, half:].reshape(nb, S5_GROUPS, S5_STATE) for o in outs])
        return kv, pe, hg, re, im

    p_kv, p_pe, p_hg, p_re, p_im = gather(outs_p, bp, lp)
    s_kv, s_pe, s_hg, s_re, s_im = gather(outs_s, bs, ls)
    return (xp.reshape(bp, lp, D_MODEL), xs.reshape(bs, ls, D_MODEL),
            p_kv, p_pe, p_hg, p_re, p_im, s_kv, s_pe, s_hg, s_re, s_im)
```

```python
import functools
import math

import jax
import jax.numpy as jnp
from jax import lax
from jax.experimental import pallas as pl
from jax.experimental.pallas import tpu as pltpu

F32 = jnp.float32
BF16 = jnp.bfloat16

D_MODEL = 1024
CHUNK = 64
CHUNK_SHIFT = CHUNK.bit_length() - 1
EPS = 1e-5
NEG_BIG = -1e30

S5_WIDTH = 256
S5_GROUP = 16
S5_GROUPS = 16
S5_STATE = 64
S5_T = 8
S5_ROW = S5_T * S5_WIDTH
S5_FLAT = 2 * S5_GROUPS * S5_STATE

HG_HEADS = 4
HG_DK = 64
HG_DV = 64
HG_WIDTH = 256
HG_ROWS = 16
HG_PITCH_PAD = 4

MLA_HEADS = 8
MLA_Q_RANK = 256
MLA_KV_RANK = 128
MLA_NOPE = 64
MLA_ROPE = 32
MLA_V = 64
MLA_WIDTH = 512
ROPE_THETA = 10000.0
ROW_TILE = 512
KEY_BLOCK_PROMPT = 512
QK_WIDTH = 256
LOG2E = 1.4426950408889634
ATTN_COLS = 512
ATTN_QUERIES = 512

D_FF = 4096
FF_CHUNK = 1024
IN_PAD = 1920

LANES = 128
VMEM_LIMIT = 56 * 1024 * 1024


def _cparams(*sem):
    return pltpu.CompilerParams(dimension_semantics=sem, vmem_limit_bytes=VMEM_LIMIT)


def _const_spec(shape):
    nd = len(shape)
    return pl.BlockSpec(shape, lambda *_: (0,) * nd, pipeline_mode=pl.Buffered(1))


def _layer_spec(shape, l):
    nd = len(shape)
    return pl.BlockSpec((None,) + tuple(shape), lambda *_: (l,) + (0,) * nd, pipeline_mode=pl.Buffered(1))


def _rms(x, g):
    y = x * lax.rsqrt(jnp.mean(x * x, axis=-1, keepdims=True) + EPS)
    return y * g


def _dot(a, b):
    return jnp.dot(a, b, preferred_element_type=F32)


def _dot_t0(a, b):
    return lax.dot_general(a, b, (((0,), (0,)), ((), ())), preferred_element_type=F32)


def _dot_t1(a, b):
    return lax.dot_general(a, b, (((1,), (1,)), ((), ())), preferred_element_type=F32)


def _split3(x):
    hi = x.astype(BF16)
    r1 = x - hi.astype(F32)
    mid = r1.astype(BF16)
    lo = (r1 - mid.astype(F32)).astype(BF16)
    return hi, mid, lo


def _in_kernel(x_ref, g1_ref, win_ref, qg_ref, wuq_ref, wuk_ref, place_ref, kvg_ref, cos_ref, sin_ref,
               u_ref, urow_ref, hq_ref, hf_ref, hi_ref, hg_ref, ckv_ref, kpe_ref, qp_ref, kk_ref, vt_ref, u_scr):
    h = _rms(x_ref[...], g1_ref[...])
    proj = _dot(h.astype(BF16), win_ref[...])
    u_ref[...] = proj[:, 0:256]
    n_row = proj.shape[0] // S5_T
    for hv in range(2):
        u_scr[hv] = proj[:, hv * LANES:(hv + 1) * LANES]
    for s in range(S5_T):
        for hv in range(2):
            lo = s * S5_WIDTH + hv * LANES
            urow_ref[:, lo:lo + LANES] = u_scr[hv, pl.ds(s, n_row, stride=S5_T), :]
    hq_ref[...] = proj[:, 256:512]
    hf_ref[...] = proj[:, 512:768]
    hi_ref[...] = proj[:, 768:1024]
    hg_ref[...] = proj[:, 1024:1280]

    cos = cos_ref[...]
    sin = sin_ref[...]
    cos2 = jnp.concatenate([cos, cos], axis=-1)
    sin2 = jnp.concatenate([sin, sin], axis=-1)

    cqn = _rms(proj[:, 1280:1536], qg_ref[...])
    q = _dot(cqn.astype(BF16), wuq_ref[...])
    scale = (MLA_NOPE + MLA_ROPE) ** -0.5 * LOG2E
    q_pe = (q[:, 512:768] * cos2 + q[:, 768:1024] * sin2) * scale
    q_lat = _dot(q[:, 0:512].astype(BF16), wuk_ref[...]) * scale
    pe_pl = _dot(q_pe.astype(BF16), place_ref[...])
    for hd in range(MLA_HEADS):
        qp_ref[:, hd * 256:hd * 256 + 128] = q_lat[:, hd * 128:(hd + 1) * 128].astype(BF16)
        qp_ref[:, hd * 256 + 128:(hd + 1) * 256] = pe_pl[:, hd * 128:(hd + 1) * 128].astype(BF16)

    c_kv = _rms(proj[:, 1536:1664], kvg_ref[...])
    ckv_ref[...] = c_kv
    kpe = proj[:, 1664:1792] * cos + proj[:, 1792:1920] * sin
    kpe_ref[...] = kpe[:, 0:MLA_ROPE]
    kk_ref[...] = jnp.concatenate([c_kv, kpe], axis=-1).astype(BF16)
    vt_ref[0] = c_kv.T.astype(BF16)


def _in_call(x, lw, l, cos, sin, tm):
    t = x.shape[0]
    row = lambda w: pl.BlockSpec((tm, w), lambda i: (i, 0))
    outs = [jax.ShapeDtypeStruct((t, 256), F32), jax.ShapeDtypeStruct((t // S5_T, S5_ROW), F32)] + [
        jax.ShapeDtypeStruct((t, 256), F32)] * 4 + [
        jax.ShapeDtypeStruct((t, MLA_KV_RANK), F32),
        jax.ShapeDtypeStruct((t, MLA_ROPE), F32),
        jax.ShapeDtypeStruct((t, MLA_HEADS * QK_WIDTH), BF16),
        jax.ShapeDtypeStruct((t, QK_WIDTH), BF16),
        jax.ShapeDtypeStruct((t // tm, MLA_KV_RANK, tm), BF16),
    ]
    out_specs = [row(256), pl.BlockSpec((tm // S5_T, S5_ROW), lambda i: (i, 0))] + [row(256)] * 4 + [
        row(MLA_KV_RANK), row(MLA_ROPE), row(MLA_HEADS * QK_WIDTH), row(QK_WIDTH),
        pl.BlockSpec((1, MLA_KV_RANK, tm), lambda i: (i, 0, 0))]
    return pl.pallas_call(
        _in_kernel,
        grid=(t // tm,),
        in_specs=[row(D_MODEL), _layer_spec((1, D_MODEL), l), _layer_spec((D_MODEL, IN_PAD), l),
                  _layer_spec((1, MLA_Q_RANK), l), _layer_spec((MLA_Q_RANK, 1024), l), _layer_spec((512, 1024), l),
                  _const_spec((256, 1024)), _layer_spec((1, MLA_KV_RANK), l), row(LANES), row(LANES)],
        out_specs=out_specs,
        out_shape=outs,
        scratch_shapes=[pltpu.VMEM((2, tm, LANES), F32)],
        compiler_params=_cparams("arbitrary"),
    )(x, lw["norm1_g"], lw["w_in"], lw["q_norm_g"], lw["w_uq"], lw["w_uk"], lw["place"], lw["kv_norm_g"],
      cos, sin)


def _s5_expand(kd_ref, w2d_ref, cp3d_ref, krev_scr, m2_scr, m3_scr):
    assert S5_ROW == S5_FLAT
    lane = lax.broadcasted_iota(jnp.int32, (S5_ROW, LANES), 1)
    row = lax.broadcasted_iota(jnp.int32, (S5_ROW, LANES), 0)
    g_in = (row // S5_GROUP) % S5_GROUPS
    g_st = (row // S5_STATE) % S5_GROUPS
    per_state = LANES // S5_STATE
    per_chan = LANES // S5_GROUP
    for hv in range(S5_WIDTH // LANES):
        keep = g_in == hv * per_chan + lane // S5_GROUP
        krev_scr[:, hv * LANES:(hv + 1) * LANES] = jnp.where(keep, kd_ref[...], 0.0).astype(BF16)
    for ri in range(2):
        src = w2d_ref[:, ri * LANES:(ri + 1) * LANES]
        for k in range(S5_GROUPS // per_state):
            keep = g_in == k * per_state + lane // S5_STATE
            lo = (ri * (S5_GROUPS // per_state) + k) * LANES
            m2_scr[:, lo:lo + LANES] = jnp.where(keep, src, 0.0).astype(BF16)
    for t in range(S5_T):
        src = cp3d_ref[:, t * LANES:(t + 1) * LANES]
        for hv in range(S5_WIDTH // LANES):
            keep = g_st == hv * per_chan + lane // S5_GROUP
            lo = t * S5_WIDTH + hv * LANES
            m3_scr[:, lo:lo + LANES] = jnp.where(keep, src, 0.0).astype(BF16)


def _s5_in_row(u, krev_ref):
    w = S5_WIDTH
    parts = [_dot(u[:, 0:(t + 1) * w], krev_ref[(S5_T - 1 - t) * w:S5_T * w, :]) for t in range(S5_T)]
    return jnp.concatenate(parts, axis=1)


def _s5_prompt_kernel(u_ref, kd_ref, w2d_ref, cp3d_ref, ab_ref, y_ref, hfin_ref,
                      s_scr, hp_scr, h_scr, krev_ref, m2_ref, m3_ref):
    half = S5_FLAT // 2
    tn = u_ref.shape[0]

    @pl.when(pl.program_id(0) == 0)
    def _():
        h_scr[...] = jnp.zeros_like(h_scr)
        _s5_expand(kd_ref, w2d_ref, cp3d_ref, krev_ref, m2_ref, m3_ref)

    u = u_ref[...].astype(BF16)
    s_scr[...] = _dot(u, m2_ref[...])
    ar = ab_ref[:, 0:half]
    ai = ab_ref[:, half:S5_FLAT]

    def body(i, carry):
        hr, hi = carry
        hp_scr[pl.ds(i, 1), 0:half] = hr
        hp_scr[pl.ds(i, 1), half:S5_FLAT] = hi
        sr = s_scr[pl.ds(i, 1), 0:half]
        si = s_scr[pl.ds(i, 1), half:S5_FLAT]
        return ar * hr - ai * hi + sr, ar * hi + ai * hr + si

    hr, hi = lax.fori_loop(0, tn, body, (h_scr[:, 0:half], h_scr[:, half:S5_FLAT]), unroll=8)
    h_scr[:, 0:half] = hr
    h_scr[:, half:S5_FLAT] = hi
    y = _s5_in_row(u, krev_ref) + _dot(hp_scr[...].astype(BF16), m3_ref[...])
    for s in range(S5_T):
        for hv in range(2):
            lo = s * S5_WIDTH + hv * LANES
            y_ref[hv, pl.ds(s, tn, stride=S5_T), :] = y[:, lo:lo + LANES]
    hfin_ref[...] = h_scr[...]


def _s5_matrix_scratch():
    return [pltpu.VMEM((S5_ROW, S5_WIDTH), BF16), pltpu.VMEM((S5_ROW, S5_FLAT), BF16),
            pltpu.VMEM((S5_FLAT, S5_ROW), BF16)]


def _s5_prompt_call(u_rows, sw, l, tn):
    n = u_rows.shape[0]
    return pl.pallas_call(
        _s5_prompt_kernel,
        grid=(n // tn,),
        in_specs=[pl.BlockSpec((tn, S5_ROW), lambda i: (i, 0)),
                  _layer_spec((S5_ROW, LANES), l), _layer_spec((S5_ROW, 2 * LANES), l),
                  _layer_spec((S5_FLAT, S5_T * LANES), l), _layer_spec((1, S5_FLAT), l)],
        out_specs=[pl.BlockSpec((2, tn * S5_T, LANES), lambda i: (0, i, 0)),
                   pl.BlockSpec((1, S5_FLAT), lambda i: (0, 0))],
        out_shape=[jax.ShapeDtypeStruct((2, n * S5_T, LANES), F32), jax.ShapeDtypeStruct((1, S5_FLAT), F32)],
        scratch_shapes=[pltpu.VMEM((tn, S5_FLAT), F32), pltpu.VMEM((tn, S5_FLAT), F32),
                        pltpu.VMEM((1, S5_FLAT), F32)] + _s5_matrix_scratch(),
        compiler_params=_cparams("arbitrary"),
    )(u_rows, sw["kd"], sw["w2d"], sw["cp3d"], sw["ab"])


def _s5_sample_kernel(u_ref, h0_ref, kd_ref, w2d_ref, cp3d_ref, ab_ref, y_ref, hfin_ref,
                      krev_ref, m2_ref, m3_ref, *, n_rows):
    half = S5_FLAT // 2
    _s5_expand(kd_ref, w2d_ref, cp3d_ref, krev_ref, m2_ref, m3_ref)
    ar = ab_ref[:, 0:half]
    ai = ab_ref[:, half:S5_FLAT]
    us = [u_ref[:, c * S5_ROW:(c + 1) * S5_ROW].astype(BF16) for c in range(n_rows)]
    u_all = jnp.concatenate(us, axis=0)
    b = us[0].shape[0]
    s_all = _dot(u_all, m2_ref[...])
    y1_all = _s5_in_row(u_all, krev_ref)
    hr = h0_ref[:, 0:half]
    hi = h0_ref[:, half:S5_FLAT]
    enter = []
    for c in range(n_rows):
        enter.append(jnp.concatenate([hr, hi], axis=-1))
        sr = s_all[c * b:(c + 1) * b, 0:half]
        si = s_all[c * b:(c + 1) * b, half:S5_FLAT]
        hr, hi = ar * hr - ai * hi + sr, ar * hi + ai * hr + si
    y_all = y1_all + _dot(jnp.concatenate(enter, axis=0).astype(BF16), m3_ref[...])
    for c in range(n_rows):
        y_ref[:, c * S5_ROW:(c + 1) * S5_ROW] = y_all[c * b:(c + 1) * b, :]
    hfin_ref[:, 0:half] = hr
    hfin_ref[:, half:S5_FLAT] = hi


def _s5_sample_call(u_seq, h0, sw, l):
    b, w = u_seq.shape
    n_rows = w // S5_ROW
    full = lambda shape: pl.BlockSpec(shape, lambda i: (0,) * len(shape))
    return pl.pallas_call(
        functools.partial(_s5_sample_kernel, n_rows=n_rows),
        grid=(1,),
        in_specs=[full((b, w)), _layer_spec((b, S5_FLAT), l), _layer_spec((S5_ROW, LANES), l),
                  _layer_spec((S5_ROW, 2 * LANES), l), _layer_spec((S5_FLAT, S5_T * LANES), l),
                  _layer_spec((1, S5_FLAT), l)],
        out_specs=[full((b, w)), full((b, S5_FLAT))],
        out_shape=[jax.ShapeDtypeStruct((b, w), F32), jax.ShapeDtypeStruct((b, S5_FLAT), F32)],
        scratch_shapes=_s5_matrix_scratch(),
        compiler_params=_cparams("arbitrary"),
    )(u_seq, h0, sw["kd"], sw["w2d"], sw["cp3d"], sw["ab"])


def _hg_gates(hq, hf, lb):
    sig = jax.nn.sigmoid(hf)
    f = lb + (1.0 - lb) * sig
    k = (1.0 - lb) * jax.nn.sigmoid(-hf)
    qf = hq * jax.nn.sigmoid(hq)
    return qf, k, f


def _hg_intra_kernel(hq_ref, hf_ref, hi_ref, lb_ref, ones_ref, o_ref, pad_scr, out_scr, k_scr, f_scr, *, r_len):
    lb = lb_ref[...]
    ones_bd = ones_ref[...]
    pitch = r_len + HG_PITCH_PAD
    for a, ref in enumerate((hq_ref, hf_ref, hi_ref)):
        for hv in range(2):
            for j in range(HG_ROWS):
                pad_scr[a, hv, pl.ds(j * pitch, r_len), :] = ref[j * r_len:(j + 1) * r_len, hv * LANES:(hv + 1) * LANES]

    def slab(a, r):
        return jnp.concatenate([pad_scr[a, hv, pl.ds(r, HG_ROWS, stride=pitch), :] for hv in range(2)], axis=-1)

    for r in range(r_len):
        _, k, f = _hg_gates(slab(0, r), slab(1, r), lb)
        k_scr[r] = k
        f_scr[r] = f
    for r in range(r_len):
        hq = slab(0, r)
        qp = hq * jax.nn.sigmoid(hq)
        terms = []
        for s in range(r, -1, -1):
            terms.append((qp * k_scr[s]).astype(BF16))
            if s > 0:
                qp = qp * f_scr[s]
        att = _dot(jnp.concatenate(terms, axis=0), ones_bd)
        acc = None
        for j, s in enumerate(range(r, -1, -1)):
            part = att[j * HG_ROWS:(j + 1) * HG_ROWS, :] * slab(2, s)
            acc = part if acc is None else acc + part
        for hv in range(2):
            out_scr[hv, pl.ds(r, HG_ROWS, stride=pitch), :] = acc[:, hv * LANES:(hv + 1) * LANES]
    for hv in range(2):
        for j in range(HG_ROWS):
            o_ref[j * r_len:(j + 1) * r_len, hv * LANES:(hv + 1) * LANES] = out_scr[hv, pl.ds(j * pitch, r_len), :]


def _hg_intra_call(hq, hf, hi, lb, l, ones_bd, r_len):
    t = hq.shape[0]
    tt = HG_ROWS * r_len
    pad_rows = HG_ROWS * (r_len + HG_PITCH_PAD)
    spec = pl.BlockSpec((tt, HG_WIDTH), lambda i: (i, 0))
    return pl.pallas_call(
        functools.partial(_hg_intra_kernel, r_len=r_len),
        grid=(t // tt,),
        in_specs=[spec, spec, spec, _layer_spec((1, HG_WIDTH), l), _const_spec((HG_WIDTH, HG_WIDTH))],
        out_specs=spec,
        out_shape=jax.ShapeDtypeStruct(hq.shape, F32),
        scratch_shapes=[pltpu.VMEM((3, 2, pad_rows, LANES), F32), pltpu.VMEM((2, pad_rows, LANES), F32),
                        pltpu.VMEM((r_len, HG_ROWS, HG_WIDTH), F32), pltpu.VMEM((r_len, HG_ROWS, HG_WIDTH), F32)],
        compiler_params=_cparams("arbitrary"),
    )(hq, hf, hi, lb, ones_bd)


def _hg_inter_kernel(hq_ref, hf_ref, hi_ref, lb_ref, tril_ref, blk_ref, hmask_ref, vsel_ref, s0_ref,
                     o_ref, sout_ref, s_scr, *, r_len, carry):
    n_sub = HG_ROWS
    if carry:
        @pl.when(pl.program_id(0) == 0)
        def _():
            s_scr[...] = s0_ref[0]

    qf, k, f = _hg_gates(hq_ref[...], hf_ref[...], lb_ref[...])
    g = jnp.log(f)
    g3 = _split3(g)
    tril = tril_ref[...]
    blk = blk_ref[...]
    bl = _dot(tril, g3[0]) + _dot(tril, g3[1]) + _dot(tril, g3[2])
    bsum = _dot(blk, g3[0]) + _dot(blk, g3[1]) + _dot(blk, g3[2])
    q_in = (qf * jnp.exp(bl)).astype(BF16)
    k_out = (k * jnp.exp(bsum - bl)).astype(BF16)
    v = hi_ref[...].astype(BF16)
    hmask = hmask_ref[...]
    vsel = vsel_ref[...]
    decay = jnp.exp(bsum)
    for j in range(n_sub):
        sl = slice(j * r_len, (j + 1) * r_len)
        if carry:
            s_in = s_scr[...]
        else:
            r3 = _split3(s0_ref[j].reshape(HG_WIDTH, HG_DV))
            s_in = (_dot_t1(vsel, r3[0]) + _dot_t1(vsel, r3[1]) + _dot_t1(vsel, r3[2])) * hmask
        o_ref[sl, :] = _dot_t1(q_in[sl], s_in.astype(BF16))
        w_new = _dot_t0(v[sl], k_out[sl]) * hmask
        s_new = decay[j * r_len:j * r_len + 1, :] * s_in + w_new
        if carry:
            s_scr[...] = s_new
        else:
            n3 = _split3(s_new)
            raw = _dot_t0(n3[0], vsel) + _dot_t0(n3[1], vsel) + _dot_t0(n3[2], vsel)
            sout_ref[j] = raw.reshape(HG_HEADS, HG_DK, HG_DV)
    if carry:
        sout_ref[0] = s_scr[...]


def _hg_inter_call(hq, hf, hi, lb, l, consts, s0, r_len, carry):
    t = hq.shape[0]
    tt = HG_ROWS * r_len
    spec = pl.BlockSpec((tt, HG_WIDTH), lambda i: (i, 0))
    if carry:
        s_in_spec = pl.BlockSpec((1, HG_WIDTH, HG_WIDTH), lambda i: (0, 0, 0))
        s_out_spec = s_in_spec
        out_state = jax.ShapeDtypeStruct((1, HG_WIDTH, HG_WIDTH), F32)
    else:
        s_in_spec = pl.BlockSpec((None, HG_ROWS, HG_HEADS, HG_DK, HG_DV), lambda i: (l, i, 0, 0, 0))
        s_out_spec = pl.BlockSpec((HG_ROWS, HG_HEADS, HG_DK, HG_DV), lambda i: (i, 0, 0, 0))
        out_state = jax.ShapeDtypeStruct(s0.shape[1:], F32)
    return pl.pallas_call(
        functools.partial(_hg_inter_kernel, r_len=r_len, carry=carry),
        grid=(t // tt,),
        in_specs=[spec, spec, spec, _layer_spec((1, HG_WIDTH), l), _const_spec((tt, tt)), _const_spec((tt, tt)),
                  _const_spec((HG_WIDTH, HG_WIDTH)), _const_spec((HG_WIDTH, HG_DV)), s_in_spec],
        out_specs=[spec, s_out_spec],
        out_shape=[jax.ShapeDtypeStruct((t, HG_WIDTH), F32), out_state],
        scratch_shapes=[pltpu.VMEM((HG_WIDTH, HG_WIDTH), F32)],
        compiler_params=_cparams("arbitrary"),
    )(hq, hf, hi, lb, consts["tril"], consts["blk"], consts["hmask"], consts["vsel"], s0)


def _attn_project(o_lat, wuv_ref, tq):
    full = _dot(o_lat, wuv_ref[...])
    lane_head = lax.broadcasted_iota(jnp.int32, (tq, MLA_WIDTH), 1) // MLA_V
    out = jnp.zeros((tq, MLA_WIDTH), F32)
    for hd in range(MLA_HEADS):
        out = out + jnp.where(lane_head == hd, full[hd * tq:(hd + 1) * tq, :], 0.0)
    return out


def _attn_kernel(qp_ref, k_ref, vt_ref, wuv_ref, o_ref, q_scr, s_buf, mb_buf, m_scr, l_scr, acc_scr, bias_scr,
                 *, tq, kb):
    i = pl.program_id(0)
    ncol = MLA_HEADS * tq

    @pl.when(i == 0)
    def _():
        k_chunk = lax.broadcasted_iota(jnp.int32, (kb, tq), 0) >> CHUNK_SHIFT
        q_chunk = lax.broadcasted_iota(jnp.int32, (kb, tq), 1) >> CHUNK_SHIFT
        bias_scr[...] = jnp.where(k_chunk <= q_chunk, 0.0, NEG_BIG)

    for hd in range(MLA_HEADS):
        q_scr[hd * tq:(hd + 1) * tq, :] = qp_ref[:, hd * QK_WIDTH:(hd + 1) * QK_WIDTH]
    m_scr[...] = jnp.full_like(m_scr, NEG_BIG)
    l_scr[...] = jnp.zeros_like(l_scr)
    acc_scr[...] = jnp.zeros_like(acc_scr)
    q0 = i * tq
    n_full = q0 // kb

    cw = min(ATTN_COLS, ncol)
    n_chunks = ncol // cw
    vt_w = vt_ref.shape[2]
    vt_per = kb // vt_w

    def scores(b, c, diagonal):
        cols = slice(c * cw, (c + 1) * cw)
        s = _dot_t1(k_ref[b], q_scr[cols, :])
        if diagonal is not None:
            off = (c * cw) % tq
            bias = bias_scr[:, off:off + cw]
            s = s + (bias if diagonal is True else jnp.where(diagonal, bias, 0.0))
        s_buf[c] = s
        mb_buf[c] = jnp.max(s, axis=0, keepdims=True)

    def values(b, c):
        cols = slice(c * cw, (c + 1) * cw)
        m_old = m_scr[:, cols]
        m_new = jnp.maximum(m_old, mb_buf[c])
        alpha = jnp.exp2(m_old - m_new)
        p = jnp.exp2(s_buf[c] - m_new)
        l_scr[:, cols] = alpha * l_scr[:, cols] + jnp.sum(p, axis=0, keepdims=True)
        pb = p.astype(BF16)
        pv = _dot(vt_ref[b * vt_per], pb[0:vt_w, :])
        for j in range(1, vt_per):
            pv = pv + _dot(vt_ref[b * vt_per + j], pb[j * vt_w:(j + 1) * vt_w, :])
        acc_scr[:, cols] = alpha * acc_scr[:, cols] + pv
        m_scr[:, cols] = m_new

    def step(b, diagonal_next):
        for c in range(n_chunks):
            values(b, c)
            scores(b + 1, c, diagonal_next)

    for c in range(n_chunks):
        scores(0, c, n_full == 0)

    def body(b, carry):
        step(b, None)
        return carry

    lax.fori_loop(0, n_full - 1, body, 0)

    @pl.when(n_full >= 1)
    def _():
        step(n_full - 1, True)

    for c in range(n_chunks):
        values(n_full, c)

    o_lat = (acc_scr[...] / l_scr[...]).T.astype(BF16)
    o_ref[...] = _attn_project(o_lat, wuv_ref, tq)


def _attn_call(qp, k_all, vt_all, wuv, l, tq):
    t = qp.shape[0]
    n_blocks, kb = k_all.shape[0], k_all.shape[1]
    ncol = MLA_HEADS * tq
    cw = min(ATTN_COLS, ncol)
    assert tq == kb and tq % cw == 0 and kb % CHUNK == 0
    return pl.pallas_call(
        functools.partial(_attn_kernel, tq=tq, kb=kb),
        grid=(t // tq,),
        in_specs=[pl.BlockSpec((tq, MLA_HEADS * QK_WIDTH), lambda i: (i, 0)),
                  _const_spec((n_blocks, kb, QK_WIDTH)), _const_spec(vt_all.shape),
                  _layer_spec((MLA_KV_RANK, MLA_WIDTH), l)],
        out_specs=pl.BlockSpec((tq, MLA_WIDTH), lambda i: (i, 0)),
        out_shape=jax.ShapeDtypeStruct((t, MLA_WIDTH), F32),
        scratch_shapes=[pltpu.VMEM((ncol, QK_WIDTH), BF16), pltpu.VMEM((ncol // cw, kb, cw), F32),
                        pltpu.VMEM((ncol // cw, 1, cw), F32), pltpu.VMEM((1, ncol), F32), pltpu.VMEM((1, ncol), F32),
                        pltpu.VMEM((MLA_KV_RANK, ncol), F32), pltpu.VMEM((kb, tq), F32)],
        compiler_params=_cparams("arbitrary"),
    )(qp, k_all, vt_all, wuv)


def _attn_sample_kernel(qp_ref, kv_ref, pe_ref, ckv_ref, kpe_ref, wuv_ref, o_ref, *, tq):
    q = jnp.concatenate([qp_ref[:, hd * QK_WIDTH:(hd + 1) * QK_WIDTH] for hd in range(MLA_HEADS)], axis=0)
    q_lat = q[:, 0:MLA_KV_RANK]
    q_pe = q[:, MLA_KV_RANK:MLA_KV_RANK + MLA_ROPE]
    kv_old, pe_old = kv_ref[0].astype(BF16), pe_ref[0].astype(BF16)
    kv_new, pe_new = ckv_ref[0].astype(BF16), kpe_ref[0].astype(BF16)
    s_old = _dot_t1(q_lat, kv_old) + _dot_t1(q_pe, pe_old)
    s_new = _dot_t1(q_lat, kv_new) + _dot_t1(q_pe, pe_new)
    m = jnp.maximum(jnp.max(s_old, axis=-1, keepdims=True), jnp.max(s_new, axis=-1, keepdims=True))
    p_old = jnp.exp2(s_old - m)
    p_new = jnp.exp2(s_new - m)
    denom = jnp.sum(p_old, axis=-1, keepdims=True) + jnp.sum(p_new, axis=-1, keepdims=True)
    o_lat = (_dot(p_old.astype(BF16), kv_old) + _dot(p_new.astype(BF16), kv_new)) / denom
    o_ref[...] = _attn_project(o_lat.astype(BF16), wuv_ref, tq)


def _attn_sample_call(qp, kv_past, pe_past, c_kv, k_pe, wuv, l):
    _, n_seq, past, _ = kv_past.shape
    tq = qp.shape[0] // n_seq
    per_seq = lambda r, w: pl.BlockSpec((1, r, w), lambda i: (i, 0, 0))
    cached = lambda w: pl.BlockSpec((None, 1, past, w), lambda i: (l, i, 0, 0))
    return pl.pallas_call(
        functools.partial(_attn_sample_kernel, tq=tq),
        grid=(n_seq,),
        in_specs=[pl.BlockSpec((tq, MLA_HEADS * QK_WIDTH), lambda i: (i, 0)), cached(MLA_KV_RANK),
                  per_seq(past, MLA_ROPE), per_seq(tq, MLA_KV_RANK), per_seq(tq, MLA_ROPE),
                  _layer_spec((MLA_KV_RANK, MLA_WIDTH), l)],
        out_specs=pl.BlockSpec((tq, MLA_WIDTH), lambda i: (i, 0)),
        out_shape=jax.ShapeDtypeStruct((qp.shape[0], MLA_WIDTH), F32),
        compiler_params=_cparams("arbitrary"),
    )(qp, kv_past, pe_past, c_kv.reshape(n_seq, tq, MLA_KV_RANK), k_pe.reshape(n_seq, tq, MLA_ROPE), wuv)


def _out_kernel(x_ref, ys_ref, u_ref, oa_ref, ob_ref, hg_ref, mla_ref, d_ref, wglu_ref, bglu_ref, og_ref,
                wout_ref, g2_ref, wup_ref, wdn_ref, fg_ref, o_ref, *, final):
    y = jnp.concatenate([ys_ref[0], ys_ref[1]], axis=-1) + d_ref[...] * u_ref[...]
    z = jax.nn.gelu(y, approximate=True)
    s5 = z * jax.nn.sigmoid(_dot(z.astype(BF16), wglu_ref[...]) + bglu_ref[...])
    og = og_ref[...]
    hgate = hg_ref[...]
    mixed = jnp.concatenate([
        _rms(s5, og[:, 0:256]),
        _rms(oa_ref[...] + ob_ref[...], og[:, 256:512]) * (hgate * jax.nn.sigmoid(hgate)),
        _rms(mla_ref[...], og[:, 512:1024]),
    ], axis=-1)
    x1 = x_ref[...] + _dot(mixed.astype(BF16), wout_ref[...])
    h2 = _rms(x1, g2_ref[...]).astype(BF16)
    acc = x1
    for c in range(D_FF // FF_CHUNK):
        up = _dot(h2, wup_ref[:, c * FF_CHUNK:(c + 1) * FF_CHUNK])
        act = jnp.square(jnp.maximum(up, 0.0)).astype(BF16)
        acc = acc + _dot(act, wdn_ref[c * FF_CHUNK:(c + 1) * FF_CHUNK, :])
    if final:
        acc = _rms(acc, fg_ref[...])
    o_ref[...] = acc


def _out_call(x, ys, u, oa, ob, hg, mla, lw, l, final_g, tm, final):
    t = x.shape[0]
    row = lambda w: pl.BlockSpec((tm, w), lambda i: (i, 0))
    return pl.pallas_call(
        functools.partial(_out_kernel, final=final),
        grid=(t // tm,),
        in_specs=[row(D_MODEL), pl.BlockSpec((2, tm, LANES), lambda i: (0, i, 0)), row(256), row(256), row(256),
                  row(256), row(MLA_WIDTH),
                  _layer_spec((1, 256), l), _layer_spec((256, 256), l), _layer_spec((1, 256), l),
                  _layer_spec((1, D_MODEL), l), _layer_spec((D_MODEL, D_MODEL), l), _layer_spec((1, D_MODEL), l),
                  _layer_spec((D_MODEL, D_FF), l), _layer_spec((D_FF, D_MODEL), l), _const_spec((1, D_MODEL))],
        out_specs=row(D_MODEL),
        out_shape=jax.ShapeDtypeStruct((t, D_MODEL), F32),
        compiler_params=_cparams("arbitrary"),
    )(x, ys, u, oa, ob, hg, mla, lw["s5_d"], lw["s5_w_glu"], lw["s5_b_glu"], lw["out_norm_g"], lw["w_out"],
      lw["norm2_g"], lw["w_up"], lw["w_down"], final_g)


def _rot_cols(w):
    half = MLA_ROPE // 2
    return jnp.concatenate([-w[..., half:], w[..., :half]], axis=-1)


def _prep_mla(w_in, w_uq, w_uk, w_uv):
    nl = w_in.shape[0]
    kpe = w_in[..., 1664:1696]
    pad = jnp.zeros((nl, D_MODEL, 96), F32)
    w_in_p = jnp.concatenate([w_in[..., :1664], kpe, pad, _rot_cols(kpe), pad], axis=-1).astype(BF16)
    uq = w_uq.reshape(nl, MLA_Q_RANK, MLA_HEADS, MLA_NOPE + MLA_ROPE)
    nope = uq[..., :MLA_NOPE].reshape(nl, MLA_Q_RANK, 512)
    pe = uq[..., MLA_NOPE:]
    w_uq_p = jnp.concatenate([nope, pe.reshape(nl, MLA_Q_RANK, 256), _rot_cols(pe).reshape(nl, MLA_Q_RANK, 256)],
                             axis=-1).astype(BF16)
    eye_h = jnp.eye(MLA_HEADS, dtype=F32)
    wuk_bd = jnp.einsum("lchd,hk->lhdkc", w_uk, eye_h).reshape(nl, 512, 1024).astype(BF16)
    place = jnp.einsum("hk,rc->hrkc", eye_h, jnp.eye(MLA_ROPE, LANES, dtype=F32)).reshape(256, 1024).astype(BF16)
    return w_in_p, w_uq_p, wuk_bd, place, w_uv.reshape(nl, MLA_KV_RANK, MLA_WIDTH).astype(BF16)


def _prep_s5(lam_re, lam_im, log_dt, b_re, b_im, c_re, c_im):
    hp = lax.Precision.HIGHEST
    t = S5_T
    dt = jnp.exp(log_dt)[:, None]
    mag1 = jnp.exp(lam_re * dt)
    a_re, a_im = mag1 * jnp.cos(lam_im * dt), mag1 * jnp.sin(lam_im * dt)
    pw_re, pw_im = [jnp.ones_like(a_re)], [jnp.zeros_like(a_im)]
    for _ in range(t):
        pr, pi = pw_re[-1], pw_im[-1]
        pw_re.append(pr * a_re - pi * a_im)
        pw_im.append(pr * a_im + pi * a_re)
    p_re, p_im = jnp.stack(pw_re), jnp.stack(pw_im)
    den = lam_re * lam_re + lam_im * lam_im
    i_re, i_im = lam_re / den, -lam_im / den
    z_re = (a_re - 1.0) * i_re - a_im * i_im
    z_im = (a_re - 1.0) * i_im + a_im * i_re
    bb_re = z_re[..., None] * b_re - z_im[..., None] * b_im
    bb_im = z_re[..., None] * b_im + z_im[..., None] * b_re
    cp_re = c_re[None] * p_re[:, :, None, :] - c_im[None] * p_im[:, :, None, :]
    cp_im = c_re[None] * p_im[:, :, None, :] + c_im[None] * p_re[:, :, None, :]
    kern = (jnp.einsum("tgap,gph->tgah", cp_re[:t], bb_re, precision=hp)
            - jnp.einsum("tgap,gph->tgah", cp_im[:t], bb_im, precision=hp))
    kd = jnp.concatenate([kern[t - 1 - j].transpose(0, 2, 1) for j in range(t)], axis=0)
    kd = jnp.tile(kd.reshape(S5_ROW, S5_GROUP), (1, LANES // S5_GROUP))
    rev_re, rev_im = jnp.stack(pw_re[t - 1::-1]), jnp.stack(pw_im[t - 1::-1])
    w2_re = rev_re[..., None] * bb_re[None] - rev_im[..., None] * bb_im[None]
    w2_im = rev_re[..., None] * bb_im[None] + rev_im[..., None] * bb_re[None]
    w2 = jnp.stack([w2_re, w2_im]).transpose(1, 2, 4, 0, 3).reshape(S5_ROW, 2, S5_STATE)
    w2d = jnp.tile(w2, (1, 1, LANES // S5_STATE)).reshape(S5_ROW, 2 * LANES)
    cp3 = jnp.stack([cp_re[1:], -cp_im[1:]]).transpose(0, 2, 4, 1, 3).reshape(S5_FLAT, t, S5_GROUP)
    cp3d = jnp.tile(cp3, (1, 1, LANES // S5_GROUP)).reshape(S5_FLAT, t * LANES)
    ab = jnp.concatenate([p_re[t].reshape(1, -1), p_im[t].reshape(1, -1)], axis=1)
    return {"kd": kd, "w2d": w2d, "cp3d": cp3d, "ab": ab}


def _hg_consts(r_len):
    tt = HG_ROWS * r_len
    r = jnp.arange(tt)
    same = (r[:, None] // r_len) == (r[None, :] // r_len)
    tril = (same & (r[None, :] <= r[:, None])).astype(BF16)
    hd = jnp.arange(HG_WIDTH) // HG_DK
    hmask = (hd[:, None] == hd[None, :]).astype(F32)
    vsel = jnp.tile(jnp.eye(HG_DV, dtype=BF16), (HG_HEADS, 1))
    return {"tril": tril, "blk": same.astype(BF16), "hmask": hmask, "vsel": vsel}


def _rope_tables(pos):
    half = MLA_ROPE // 2
    inv = ROPE_THETA ** (-jnp.arange(half, dtype=F32) / half)
    ang = pos.astype(F32)[:, None] * inv[None, :]
    reps = LANES // half
    return jnp.tile(jnp.cos(ang), (1, reps)), jnp.tile(jnp.sin(ang), (1, reps))


def _state_from_bd(s):
    blocks = [s[:, h * HG_DV:(h + 1) * HG_DV, h * HG_DK:(h + 1) * HG_DK].swapaxes(1, 2) for h in range(HG_HEADS)]
    return jnp.stack(blocks, axis=1)


def _layer(x, lw, sw, l, cos, sin, lb, hgc, ones_bd, final_g, final, *, prompt, n_seq, s5_h0, hg_s0, kv_past,
           pe_past):
    t = x.shape[0]
    seq = t // n_seq
    tm = ROW_TILE
    u, u_rows, hq, hf, hi, hg, c_kv, k_pe, qp, kk, vt = _in_call(x, lw, l, cos, sin, tm)

    if prompt:
        ys, s5_fin = _s5_prompt_call(u_rows, sw, l, min(256, t // S5_T))
    else:
        ys, s5_fin = _s5_sample_call(u_rows.reshape(n_seq, seq * S5_WIDTH), s5_h0, sw, l)
        ys = ys.reshape(t, 2, LANES).swapaxes(0, 1)

    r_len = 32 if prompt else seq
    oa = _hg_intra_call(hq, hf, hi, lb, l, ones_bd, r_len)
    ob, hg_fin = _hg_inter_call(hq, hf, hi, lb, l, hgc, hg_s0, r_len, carry=prompt)

    if prompt:
        kb = min(KEY_BLOCK_PROMPT, t)
        mla = _attn_call(qp, kk.reshape(t // kb, kb, QK_WIDTH), vt, lw["w_uv"], l, ATTN_QUERIES)
    else:
        mla = _attn_sample_call(qp, kv_past, pe_past[l], c_kv, k_pe, lw["w_uv"], l)

    x_new = _out_call(x, ys, u, oa, ob, hg, mla, lw, l, final_g, tm, final)
    return x_new, c_kv, k_pe, hg_fin, s5_fin


def kernel(x_prompt, x_sample, cache_mla_kv, cache_mla_pe, state_hgrn, state_s5_re, state_s5_im, norm1_g, w_in, s5_lambda_re, s5_lambda_im, s5_log_dt, s5_b_re, s5_b_im, s5_c_re, s5_c_im, s5_d, s5_w_glu, s5_b_glu, hgrn_lb_logits, mla_q_norm_g, mla_w_uq, mla_kv_norm_g, mla_w_uk, mla_w_uv, out_norm_g, w_out, norm2_g, w_up, w_down, final_norm_g):
    depth = w_in.shape[0]
    bp, lp = x_prompt.shape[0], x_prompt.shape[1]
    bs, ls = x_sample.shape[0], x_sample.shape[1]
    past = cache_mla_kv.shape[2]
    assert bp == 1 and ls == 2 * S5_T and bs % HG_ROWS == 0
    assert past % CHUNK == 0 and ls <= CHUNK

    cos_p, sin_p = _rope_tables(jnp.arange(lp, dtype=jnp.int32))
    cos_s, sin_s = _rope_tables(past + jnp.arange(ls, dtype=jnp.int32))
    cos_s, sin_s = jnp.tile(cos_s, (bs, 1)), jnp.tile(sin_s, (bs, 1))

    lb_p = jax.nn.softmax(hgrn_lb_logits.astype(F32), axis=0)
    lb_all = jnp.cumsum(lb_p, axis=0) - lb_p[0]
    hgc_p, hgc_s = _hg_consts(32), _hg_consts(ls)
    hd = jnp.arange(HG_WIDTH) // HG_DK
    ones_bd = (hd[:, None] == hd[None, :]).astype(BF16)
    rows = lambda v: v.reshape(depth, 1, -1).astype(F32)
    final_g = final_norm_g.reshape(1, -1).astype(F32)

    w_in_p, w_uq_p, wuk_bd, place, w_uv_p = _prep_mla(w_in, mla_w_uq, mla_w_uk, mla_w_uv)
    lw = {
        "norm1_g": rows(norm1_g), "w_in": w_in_p, "q_norm_g": rows(mla_q_norm_g), "w_uq": w_uq_p,
        "w_uk": wuk_bd, "place": place, "kv_norm_g": rows(mla_kv_norm_g), "w_uv": w_uv_p,
        "s5_d": rows(s5_d), "s5_w_glu": s5_w_glu.astype(BF16), "s5_b_glu": rows(s5_b_glu),
        "out_norm_g": rows(out_norm_g), "w_out": w_out.astype(BF16), "norm2_g": rows(norm2_g),
        "w_up": w_up.astype(BF16), "w_down": w_down.astype(BF16),
    }
    sw = jax.vmap(_prep_s5)(s5_lambda_re, s5_lambda_im, s5_log_dt, s5_b_re, s5_b_im, s5_c_re, s5_c_im)
    lb = rows(lb_all)
    s5_h0 = jnp.concatenate([state_s5_re.reshape(depth, bs, -1), state_s5_im.reshape(depth, bs, -1)], axis=-1)
    hg_zero = jnp.zeros((1, HG_WIDTH, HG_WIDTH), F32)

    xp = x_prompt.reshape(bp * lp, D_MODEL)
    xs = x_sample.reshape(bs * ls, D_MODEL)
    outs_p, outs_s = [], []
    for l in range(depth):
        final = l == depth - 1
        xp, a, b, c, d = _layer(xp, lw, sw, l, cos_p, sin_p, lb, hgc_p, ones_bd, final_g, final, prompt=True,
                                n_seq=1, s5_h0=None, hg_s0=hg_zero, kv_past=None, pe_past=None)
        outs_p.append((a, b, c, d))
        xs, a, b, c, d = _layer(xs, lw, sw, l, cos_s, sin_s, lb, hgc_s, ones_bd, final_g, final, prompt=False,
                                n_seq=bs, s5_h0=s5_h0, hg_s0=state_hgrn.astype(F32), kv_past=cache_mla_kv, pe_past=cache_mla_pe)
        outs_s.append((a, b, c, d))

    def gather(outs, nb, sl):
        kv = jnp.stack([o[0].reshape(nb, sl, MLA_KV_RANK) for o in outs])
        pe = jnp.stack([o[1].reshape(nb, sl, MLA_ROPE) for o in outs])
        hg = jnp.stack([o[2] if o[2].ndim == 4 else _state_from_bd(o[2]) for o in outs])
        half = S5_FLAT // 2
        re = jnp.stack([o[3][:, :half].reshape(nb, S5_GROUPS, S5_STATE) for o in outs])
        im = jnp.stack([o[3][:, half:].reshape(nb, S5_GROUPS, S5_STATE) for o in outs])
        return kv, pe, hg, re, im

    p_kv, p_pe, p_hg, p_re, p_im = gather(outs_p, bp, lp)
    s_kv, s_pe, s_hg, s_re, s_im = gather(outs_s, bs, ls)
    return (xp.reshape(bp, lp, D_MODEL), xs.reshape(bs, ls, D_MODEL),
            p_kv, p_pe, p_hg, p_re, p_im, s_kv, s_pe, s_hg, s_re, s_im)
```

```python
import functools
import math

import jax
import jax.numpy as jnp
from jax import lax
from jax.experimental import pallas as pl
from jax.experimental.pallas import tpu as pltpu

F32 = jnp.float32
BF16 = jnp.bfloat16

D_MODEL = 1024
CHUNK = 64
CHUNK_SHIFT = CHUNK.bit_length() - 1
EPS = 1e-5
NEG_BIG = -1e30

S5_WIDTH = 256
S5_GROUP = 16
S5_GROUPS = 16
S5_STATE = 64
S5_T = 8
S5_ROW = S5_T * S5_WIDTH
S5_FLAT = 2 * S5_GROUPS * S5_STATE

HG_HEADS = 4
HG_DK = 64
HG_DV = 64
HG_WIDTH = 256
HG_ROWS = 16
HG_PITCH_PAD = 4

MLA_HEADS = 8
MLA_Q_RANK = 256
MLA_KV_RANK = 128
MLA_NOPE = 64
MLA_ROPE = 32
MLA_V = 64
MLA_WIDTH = 512
ROPE_THETA = 10000.0
ROW_TILE = 512
KEY_BLOCK_PROMPT = 512
QK_WIDTH = 256
LOG2E = 1.4426950408889634
ATTN_COLS = 512
ATTN_QUERIES = 512

D_FF = 4096
FF_CHUNK = 1024
IN_PAD = 1920

LANES = 128
VMEM_LIMIT = 56 * 1024 * 1024


def _cparams(*sem):
    return pltpu.CompilerParams(dimension_semantics=sem, vmem_limit_bytes=VMEM_LIMIT)


def _const_spec(shape):
    nd = len(shape)
    return pl.BlockSpec(shape, lambda *_: (0,) * nd, pipeline_mode=pl.Buffered(1))


def _layer_spec(shape, l):
    nd = len(shape)
    return pl.BlockSpec((None,) + tuple(shape), lambda *_: (l,) + (0,) * nd, pipeline_mode=pl.Buffered(1))


def _rms(x, g):
    y = x * lax.rsqrt(jnp.mean(x * x, axis=-1, keepdims=True) + EPS)
    return y * g


def _dot(a, b):
    return jnp.dot(a, b, preferred_element_type=F32)


def _dot_t0(a, b):
    return lax.dot_general(a, b, (((0,), (0,)), ((), ())), preferred_element_type=F32)


def _dot_t1(a, b):
    return lax.dot_general(a, b, (((1,), (1,)), ((), ())), preferred_element_type=F32)


def _split3(x):
    hi = x.astype(BF16)
    r1 = x - hi.astype(F32)
    mid = r1.astype(BF16)
    lo = (r1 - mid.astype(F32)).astype(BF16)
    return hi, mid, lo


def _in_kernel(x_ref, g1_ref, win_ref, qg_ref, wuq_ref, wuk_ref, place_ref, kvg_ref, cos_ref, sin_ref,
               u_ref, urow_ref, hq_ref, hf_ref, hi_ref, hg_ref, ckv_ref, kpe_ref, qp_ref, kk_ref, vt_ref, u_scr):
    h = _rms(x_ref[...], g1_ref[...])
    proj = _dot(h.astype(BF16), win_ref[...])
    u_ref[...] = proj[:, 0:256]
    n_row = proj.shape[0] // S5_T
    for hv in range(2):
        u_scr[hv] = proj[:, hv * LANES:(hv + 1) * LANES]
    for s in range(S5_T):
        for hv in range(2):
            lo = s * S5_WIDTH + hv * LANES
            urow_ref[:, lo:lo + LANES] = u_scr[hv, pl.ds(s, n_row, stride=S5_T), :]
    hq_ref[...] = proj[:, 256:512]
    hf_ref[...] = proj[:, 512:768]
    hi_ref[...] = proj[:, 768:1024]
    hg_ref[...] = proj[:, 1024:1280]

    cos = cos_ref[...]
    sin = sin_ref[...]
    cos2 = jnp.concatenate([cos, cos], axis=-1)
    sin2 = jnp.concatenate([sin, sin], axis=-1)

    cqn = _rms(proj[:, 1280:1536], qg_ref[...])
    q = _dot(cqn.astype(BF16), wuq_ref[...])
    scale = (MLA_NOPE + MLA_ROPE) ** -0.5 * LOG2E
    q_pe = (q[:, 512:768] * cos2 + q[:, 768:1024] * sin2) * scale
    q_lat = _dot(q[:, 0:512].astype(BF16), wuk_ref[...]) * scale
    pe_pl = _dot(q_pe.astype(BF16), place_ref[...])
    for hd in range(MLA_HEADS):
        qp_ref[:, hd * 256:hd * 256 + 128] = q_lat[:, hd * 128:(hd + 1) * 128].astype(BF16)
        qp_ref[:, hd * 256 + 128:(hd + 1) * 256] = pe_pl[:, hd * 128:(hd + 1) * 128].astype(BF16)

    c_kv = _rms(proj[:, 1536:1664], kvg_ref[...])
    ckv_ref[...] = c_kv
    kpe = proj[:, 1664:1792] * cos + proj[:, 1792:1920] * sin
    kpe_ref[...] = kpe[:, 0:MLA_ROPE]
    kk_ref[...] = jnp.concatenate([c_kv, kpe], axis=-1).astype(BF16)
    vt_ref[0] = c_kv.T.astype(BF16)


def _in_call(x, lw, l, cos, sin, tm):
    t = x.shape[0]
    row = lambda w: pl.BlockSpec((tm, w), lambda i: (i, 0))
    outs = [jax.ShapeDtypeStruct((t, 256), F32), jax.ShapeDtypeStruct((t // S5_T, S5_ROW), F32)] + [
        jax.ShapeDtypeStruct((t, 256), F32)] * 4 + [
        jax.ShapeDtypeStruct((t, MLA_KV_RANK), F32),
        jax.ShapeDtypeStruct((t, MLA_ROPE), F32),
        jax.ShapeDtypeStruct((t, MLA_HEADS * QK_WIDTH), BF16),
        jax.ShapeDtypeStruct((t, QK_WIDTH), BF16),
        jax.ShapeDtypeStruct((t // tm, MLA_KV_RANK, tm), BF16),
    ]
    out_specs = [row(256), pl.BlockSpec((tm // S5_T, S5_ROW), lambda i: (i, 0))] + [row(256)] * 4 + [
        row(MLA_KV_RANK), row(MLA_ROPE), row(MLA_HEADS * QK_WIDTH), row(QK_WIDTH),
        pl.BlockSpec((1, MLA_KV_RANK, tm), lambda i: (i, 0, 0))]
    return pl.pallas_call(
        _in_kernel,
        grid=(t // tm,),
        in_specs=[row(D_MODEL), _layer_spec((1, D_MODEL), l), _layer_spec((D_MODEL, IN_PAD), l),
                  _layer_spec((1, MLA_Q_RANK), l), _layer_spec((MLA_Q_RANK, 1024), l), _layer_spec((512, 1024), l),
                  _const_spec((256, 1024)), _layer_spec((1, MLA_KV_RANK), l), row(LANES), row(LANES)],
        out_specs=out_specs,
        out_shape=outs,
        scratch_shapes=[pltpu.VMEM((2, tm, LANES), F32)],
        compiler_params=_cparams("arbitrary"),
    )(x, lw["norm1_g"], lw["w_in"], lw["q_norm_g"], lw["w_uq"], lw["w_uk"], lw["place"], lw["kv_norm_g"],
      cos, sin)


def _s5_expand(kd_ref, w2c_ref, cp3c_ref, rep2_ref, rep3_ref, krev_scr, m2_scr, m3_scr):
    assert S5_ROW == S5_FLAT
    lane = lax.broadcasted_iota(jnp.int32, (S5_ROW, LANES), 1)
    row = lax.broadcasted_iota(jnp.int32, (S5_ROW, LANES), 0)
    g_in = (row // S5_GROUP) % S5_GROUPS
    g_st = (row // S5_STATE) % S5_GROUPS
    per_state = LANES // S5_STATE
    per_chan = LANES // S5_GROUP
    for hv in range(S5_WIDTH // LANES):
        keep = g_in == hv * per_chan + lane // S5_GROUP
        krev_scr[:, hv * LANES:(hv + 1) * LANES] = jnp.where(keep, kd_ref[...], 0.0).astype(BF16)
    for ri in range(2):
        src = _dot(w2c_ref[...], rep2_ref[ri])
        for k in range(S5_GROUPS // per_state):
            keep = g_in == k * per_state + lane // S5_STATE
            lo = (ri * (S5_GROUPS // per_state) + k) * LANES
            m2_scr[:, lo:lo + LANES] = jnp.where(keep, src, 0.0).astype(BF16)
    for t in range(S5_T):
        src = _dot(cp3c_ref[...], rep3_ref[t])
        for hv in range(S5_WIDTH // LANES):
            keep = g_st == hv * per_chan + lane // S5_GROUP
            lo = t * S5_WIDTH + hv * LANES
            m3_scr[:, lo:lo + LANES] = jnp.where(keep, src, 0.0).astype(BF16)


def _s5_in_row(u, krev_ref):
    w = S5_WIDTH
    parts = [_dot(u[:, 0:(t + 1) * w], krev_ref[(S5_T - 1 - t) * w:S5_T * w, :]) for t in range(S5_T)]
    return jnp.concatenate(parts, axis=1)


def _s5_prompt_kernel(u_ref, kd_ref, w2c_ref, cp3c_ref, rep2_ref, rep3_ref, ab_ref, y_ref, hfin_ref,
                      s_scr, hp_scr, h_scr, krev_ref, m2_ref, m3_ref):
    half = S5_FLAT // 2
    tn = u_ref.shape[0]

    @pl.when(pl.program_id(0) == 0)
    def _():
        h_scr[...] = jnp.zeros_like(h_scr)
        _s5_expand(kd_ref, w2c_ref, cp3c_ref, rep2_ref, rep3_ref, krev_ref, m2_ref, m3_ref)

    u = u_ref[...].astype(BF16)
    s_scr[...] = _dot(u, m2_ref[...])
    ar = ab_ref[:, 0:half]
    ai = ab_ref[:, half:S5_FLAT]

    def body(i, carry):
        hr, hi = carry
        hp_scr[pl.ds(i, 1), 0:half] = hr
        hp_scr[pl.ds(i, 1), half:S5_FLAT] = hi
        sr = s_scr[pl.ds(i, 1), 0:half]
        si = s_scr[pl.ds(i, 1), half:S5_FLAT]
        return ar * hr - ai * hi + sr, ar * hi + ai * hr + si

    hr, hi = lax.fori_loop(0, tn, body, (h_scr[:, 0:half], h_scr[:, half:S5_FLAT]), unroll=8)
    h_scr[:, 0:half] = hr
    h_scr[:, half:S5_FLAT] = hi
    y = _s5_in_row(u, krev_ref) + _dot(hp_scr[...].astype(BF16), m3_ref[...])
    for s in range(S5_T):
        for hv in range(2):
            lo = s * S5_WIDTH + hv * LANES
            y_ref[hv, pl.ds(s, tn, stride=S5_T), :] = y[:, lo:lo + LANES]
    hfin_ref[...] = h_scr[...]


def _s5_matrix_scratch():
    return [pltpu.VMEM((S5_ROW, S5_WIDTH), BF16), pltpu.VMEM((S5_ROW, S5_FLAT), BF16),
            pltpu.VMEM((S5_FLAT, S5_ROW), BF16)]


def _s5_prompt_call(u_rows, sw, l, tn):
    n = u_rows.shape[0]
    return pl.pallas_call(
        _s5_prompt_kernel,
        grid=(n // tn,),
        in_specs=[pl.BlockSpec((tn, S5_ROW), lambda i: (i, 0)),
                  _layer_spec((S5_ROW, LANES), l), _layer_spec((S5_ROW, LANES), l), _layer_spec((S5_FLAT, LANES), l),
                  _const_spec((2, LANES, LANES)), _const_spec((S5_T, LANES, LANES)), _layer_spec((1, S5_FLAT), l)],
        out_specs=[pl.BlockSpec((2, tn * S5_T, LANES), lambda i: (0, i, 0)),
                   pl.BlockSpec((1, S5_FLAT), lambda i: (0, 0))],
        out_shape=[jax.ShapeDtypeStruct((2, n * S5_T, LANES), F32), jax.ShapeDtypeStruct((1, S5_FLAT), F32)],
        scratch_shapes=[pltpu.VMEM((tn, S5_FLAT), F32), pltpu.VMEM((tn, S5_FLAT), F32),
                        pltpu.VMEM((1, S5_FLAT), F32)] + _s5_matrix_scratch(),
        compiler_params=_cparams("arbitrary"),
    )(u_rows, sw["kd"], sw["w2c"], sw["cp3c"], sw["rep2"], sw["rep3"], sw["ab"])


def _s5_sample_kernel(u_ref, h0_ref, kd_ref, w2c_ref, cp3c_ref, rep2_ref, rep3_ref, ab_ref, y_ref, hfin_ref,
                      krev_ref, m2_ref, m3_ref, *, n_rows):
    half = S5_FLAT // 2
    _s5_expand(kd_ref, w2c_ref, cp3c_ref, rep2_ref, rep3_ref, krev_ref, m2_ref, m3_ref)
    ar = ab_ref[:, 0:half]
    ai = ab_ref[:, half:S5_FLAT]
    us = [u_ref[:, c * S5_ROW:(c + 1) * S5_ROW].astype(BF16) for c in range(n_rows)]
    u_all = jnp.concatenate(us, axis=0)
    b = us[0].shape[0]
    s_all = _dot(u_all, m2_ref[...])
    y1_all = _s5_in_row(u_all, krev_ref)
    hr = h0_ref[:, 0:half]
    hi = h0_ref[:, half:S5_FLAT]
    enter = []
    for c in range(n_rows):
        enter.append(jnp.concatenate([hr, hi], axis=-1))
        sr = s_all[c * b:(c + 1) * b, 0:half]
        si = s_all[c * b:(c + 1) * b, half:S5_FLAT]
        hr, hi = ar * hr - ai * hi + sr, ar * hi + ai * hr + si
    y_all = y1_all + _dot(jnp.concatenate(enter, axis=0).astype(BF16), m3_ref[...])
    for c in range(n_rows):
        y_ref[:, c * S5_ROW:(c + 1) * S5_ROW] = y_all[c * b:(c + 1) * b, :]
    hfin_ref[:, 0:half] = hr
    hfin_ref[:, half:S5_FLAT] = hi


def _s5_sample_call(u_seq, h0, sw, l):
    b, w = u_seq.shape
    n_rows = w // S5_ROW
    full = lambda shape: pl.BlockSpec(shape, lambda i: (0,) * len(shape))
    return pl.pallas_call(
        functools.partial(_s5_sample_kernel, n_rows=n_rows),
        grid=(1,),
        in_specs=[full((b, w)), _layer_spec((b, S5_FLAT), l), _layer_spec((S5_ROW, LANES), l),
                  _layer_spec((S5_ROW, LANES), l), _layer_spec((S5_FLAT, LANES), l),
                  _const_spec((2, LANES, LANES)), _const_spec((S5_T, LANES, LANES)), _layer_spec((1, S5_FLAT), l)],
        out_specs=[full((b, w)), full((b, S5_FLAT))],
        out_shape=[jax.ShapeDtypeStruct((b, w), F32), jax.ShapeDtypeStruct((b, S5_FLAT), F32)],
        scratch_shapes=_s5_matrix_scratch(),
        compiler_params=_cparams("arbitrary"),
    )(u_seq, h0, sw["kd"], sw["w2c"], sw["cp3c"], sw["rep2"], sw["rep3"], sw["ab"])


def _hg_gates(hq, hf, lb):
    sig = jax.nn.sigmoid(hf)
    f = lb + (1.0 - lb) * sig
    k = (1.0 - lb) * jax.nn.sigmoid(-hf)
    qf = hq * jax.nn.sigmoid(hq)
    return qf, k, f


def _hg_intra_kernel(hq_ref, hf_ref, hi_ref, lb_ref, ones_ref, o_ref, pad_scr, out_scr, k_scr, f_scr, *, r_len):
    lb = lb_ref[...]
    ones_bd = ones_ref[...]
    pitch = r_len + HG_PITCH_PAD
    for a, ref in enumerate((hq_ref, hf_ref, hi_ref)):
        for hv in range(2):
            for j in range(HG_ROWS):
                pad_scr[a, hv, pl.ds(j * pitch, r_len), :] = ref[j * r_len:(j + 1) * r_len, hv * LANES:(hv + 1) * LANES]

    def slab(a, r):
        return jnp.concatenate([pad_scr[a, hv, pl.ds(r, HG_ROWS, stride=pitch), :] for hv in range(2)], axis=-1)

    for r in range(r_len):
        _, k, f = _hg_gates(slab(0, r), slab(1, r), lb)
        k_scr[r] = k
        f_scr[r] = f
    for r in range(r_len):
        hq = slab(0, r)
        qp = hq * jax.nn.sigmoid(hq)
        terms = []
        for s in range(r, -1, -1):
            terms.append((qp * k_scr[s]).astype(BF16))
            if s > 0:
                qp = qp * f_scr[s]
        att = _dot(jnp.concatenate(terms, axis=0), ones_bd)
        acc = None
        for j, s in enumerate(range(r, -1, -1)):
            part = att[j * HG_ROWS:(j + 1) * HG_ROWS, :] * slab(2, s)
            acc = part if acc is None else acc + part
        for hv in range(2):
            out_scr[hv, pl.ds(r, HG_ROWS, stride=pitch), :] = acc[:, hv * LANES:(hv + 1) * LANES]
    for hv in range(2):
        for j in range(HG_ROWS):
            o_ref[j * r_len:(j + 1) * r_len, hv * LANES:(hv + 1) * LANES] = out_scr[hv, pl.ds(j * pitch, r_len), :]


def _hg_intra_call(hq, hf, hi, lb, l, ones_bd, r_len):
    t = hq.shape[0]
    tt = HG_ROWS * r_len
    pad_rows = HG_ROWS * (r_len + HG_PITCH_PAD)
    spec = pl.BlockSpec((tt, HG_WIDTH), lambda i: (i, 0))
    return pl.pallas_call(
        functools.partial(_hg_intra_kernel, r_len=r_len),
        grid=(t // tt,),
        in_specs=[spec, spec, spec, _layer_spec((1, HG_WIDTH), l), _const_spec((HG_WIDTH, HG_WIDTH))],
        out_specs=spec,
        out_shape=jax.ShapeDtypeStruct(hq.shape, F32),
        scratch_shapes=[pltpu.VMEM((3, 2, pad_rows, LANES), F32), pltpu.VMEM((2, pad_rows, LANES), F32),
                        pltpu.VMEM((r_len, HG_ROWS, HG_WIDTH), F32), pltpu.VMEM((r_len, HG_ROWS, HG_WIDTH), F32)],
        compiler_params=_cparams("arbitrary"),
    )(hq, hf, hi, lb, ones_bd)


def _hg_inter_kernel(hq_ref, hf_ref, hi_ref, lb_ref, tril_ref, blk_ref, hmask_ref, vsel_ref, s0_ref,
                     o_ref, sout_ref, s_scr, *, r_len, carry):
    n_sub = HG_ROWS
    if carry:
        @pl.when(pl.program_id(0) == 0)
        def _():
            s_scr[...] = s0_ref[0]

    qf, k, f = _hg_gates(hq_ref[...], hf_ref[...], lb_ref[...])
    g = jnp.log(f)
    g3 = _split3(g)
    tril = tril_ref[...]
    blk = blk_ref[...]
    bl = _dot(tril, g3[0]) + _dot(tril, g3[1]) + _dot(tril, g3[2])
    bsum = _dot(blk, g3[0]) + _dot(blk, g3[1]) + _dot(blk, g3[2])
    q_in = (qf * jnp.exp(bl)).astype(BF16)
    k_out = (k * jnp.exp(bsum - bl)).astype(BF16)
    v = hi_ref[...].astype(BF16)
    hmask = hmask_ref[...]
    vsel = vsel_ref[...]
    decay = jnp.exp(bsum)
    for j in range(n_sub):
        sl = slice(j * r_len, (j + 1) * r_len)
        if carry:
            s_in = s_scr[...]
        else:
            r3 = _split3(s0_ref[j].reshape(HG_WIDTH, HG_DV))
            s_in = (_dot_t1(vsel, r3[0]) + _dot_t1(vsel, r3[1]) + _dot_t1(vsel, r3[2])) * hmask
        o_ref[sl, :] = _dot_t1(q_in[sl], s_in.astype(BF16))
        w_new = _dot_t0(v[sl], k_out[sl]) * hmask
        s_new = decay[j * r_len:j * r_len + 1, :] * s_in + w_new
        if carry:
            s_scr[...] = s_new
        else:
            n3 = _split3(s_new)
            raw = _dot_t0(n3[0], vsel) + _dot_t0(n3[1], vsel) + _dot_t0(n3[2], vsel)
            sout_ref[j] = raw.reshape(HG_HEADS, HG_DK, HG_DV)
    if carry:
        sout_ref[0] = s_scr[...]


def _hg_inter_call(hq, hf, hi, lb, l, consts, s0, r_len, carry):
    t = hq.shape[0]
    tt = HG_ROWS * r_len
    spec = pl.BlockSpec((tt, HG_WIDTH), lambda i: (i, 0))
    if carry:
        s_in_spec = pl.BlockSpec((1, HG_WIDTH, HG_WIDTH), lambda i: (0, 0, 0))
        s_out_spec = s_in_spec
        out_state = jax.ShapeDtypeStruct((1, HG_WIDTH, HG_WIDTH), F32)
    else:
        s_in_spec = pl.BlockSpec((None, HG_ROWS, HG_HEADS, HG_DK, HG_DV), lambda i: (l, i, 0, 0, 0))
        s_out_spec = pl.BlockSpec((HG_ROWS, HG_HEADS, HG_DK, HG_DV), lambda i: (i, 0, 0, 0))
        out_state = jax.ShapeDtypeStruct(s0.shape[1:], F32)
    return pl.pallas_call(
        functools.partial(_hg_inter_kernel, r_len=r_len, carry=carry),
        grid=(t // tt,),
        in_specs=[spec, spec, spec, _layer_spec((1, HG_WIDTH), l), _const_spec((tt, tt)), _const_spec((tt, tt)),
                  _const_spec((HG_WIDTH, HG_WIDTH)), _const_spec((HG_WIDTH, HG_DV)), s_in_spec],
        out_specs=[spec, s_out_spec],
        out_shape=[jax.ShapeDtypeStruct((t, HG_WIDTH), F32), out_state],
        scratch_shapes=[pltpu.VMEM((HG_WIDTH, HG_WIDTH), F32)],
        compiler_params=_cparams("arbitrary"),
    )(hq, hf, hi, lb, consts["tril"], consts["blk"], consts["hmask"], consts["vsel"], s0)


def _attn_project(o_lat, wuv_ref, tq):
    full = _dot(o_lat, wuv_ref[...])
    lane_head = lax.broadcasted_iota(jnp.int32, (tq, MLA_WIDTH), 1) // MLA_V
    out = jnp.zeros((tq, MLA_WIDTH), F32)
    for hd in range(MLA_HEADS):
        out = out + jnp.where(lane_head == hd, full[hd * tq:(hd + 1) * tq, :], 0.0)
    return out


def _attn_kernel(qp_ref, k_ref, vt_ref, wuv_ref, o_ref, q_scr, s_buf, mb_buf, m_scr, l_scr, acc_scr, bias_scr,
                 *, tq, kb):
    i = pl.program_id(0)
    ncol = MLA_HEADS * tq

    @pl.when(i == 0)
    def _():
        k_chunk = lax.broadcasted_iota(jnp.int32, (kb, tq), 0) >> CHUNK_SHIFT
        q_chunk = lax.broadcasted_iota(jnp.int32, (kb, tq), 1) >> CHUNK_SHIFT
        bias_scr[...] = jnp.where(k_chunk <= q_chunk, 0.0, NEG_BIG)

    for hd in range(MLA_HEADS):
        q_scr[hd * tq:(hd + 1) * tq, :] = qp_ref[:, hd * QK_WIDTH:(hd + 1) * QK_WIDTH]
    m_scr[...] = jnp.full_like(m_scr, NEG_BIG)
    l_scr[...] = jnp.zeros_like(l_scr)
    acc_scr[...] = jnp.zeros_like(acc_scr)
    q0 = i * tq
    n_full = q0 // kb

    cw = min(ATTN_COLS, ncol)
    n_chunks = ncol // cw
    vt_w = vt_ref.shape[2]
    vt_per = kb // vt_w

    def scores(b, c, diagonal):
        cols = slice(c * cw, (c + 1) * cw)
        s = _dot_t1(k_ref[b], q_scr[cols, :])
        if diagonal is not None:
            off = (c * cw) % tq
            bias = bias_scr[:, off:off + cw]
            s = s + (bias if diagonal is True else jnp.where(diagonal, bias, 0.0))
        s_buf[c] = s
        mb_buf[c] = jnp.max(s, axis=0, keepdims=True)

    def values(b, c):
        cols = slice(c * cw, (c + 1) * cw)
        m_old = m_scr[:, cols]
        m_new = jnp.maximum(m_old, mb_buf[c])
        alpha = jnp.exp2(m_old - m_new)
        p = jnp.exp2(s_buf[c] - m_new)
        l_scr[:, cols] = alpha * l_scr[:, cols] + jnp.sum(p, axis=0, keepdims=True)
        pb = p.astype(BF16)
        pv = _dot(vt_ref[b * vt_per], pb[0:vt_w, :])
        for j in range(1, vt_per):
            pv = pv + _dot(vt_ref[b * vt_per + j], pb[j * vt_w:(j + 1) * vt_w, :])
        acc_scr[:, cols] = alpha * acc_scr[:, cols] + pv
        m_scr[:, cols] = m_new

    def step(b, diagonal_next):
        for c in range(n_chunks):
            values(b, c)
            scores(b + 1, c, diagonal_next)

    for c in range(n_chunks):
        scores(0, c, n_full == 0)

    def body(b, carry):
        step(b, None)
        return carry

    lax.fori_loop(0, n_full - 1, body, 0)

    @pl.when(n_full >= 1)
    def _():
        step(n_full - 1, True)

    for c in range(n_chunks):
        values(n_full, c)

    o_lat = (acc_scr[...] / l_scr[...]).T.astype(BF16)
    o_ref[...] = _attn_project(o_lat, wuv_ref, tq)


def _attn_call(qp, k_all, vt_all, wuv, l, tq):
    t = qp.shape[0]
    n_blocks, kb = k_all.shape[0], k_all.shape[1]
    ncol = MLA_HEADS * tq
    cw = min(ATTN_COLS, ncol)
    assert tq == kb and tq % cw == 0 and kb % CHUNK == 0
    return pl.pallas_call(
        functools.partial(_attn_kernel, tq=tq, kb=kb),
        grid=(t // tq,),
        in_specs=[pl.BlockSpec((tq, MLA_HEADS * QK_WIDTH), lambda i: (i, 0)),
                  _const_spec((n_blocks, kb, QK_WIDTH)), _const_spec(vt_all.shape),
                  _layer_spec((MLA_KV_RANK, MLA_WIDTH), l)],
        out_specs=pl.BlockSpec((tq, MLA_WIDTH), lambda i: (i, 0)),
        out_shape=jax.ShapeDtypeStruct((t, MLA_WIDTH), F32),
        scratch_shapes=[pltpu.VMEM((ncol, QK_WIDTH), BF16), pltpu.VMEM((ncol // cw, kb, cw), F32),
                        pltpu.VMEM((ncol // cw, 1, cw), F32), pltpu.VMEM((1, ncol), F32), pltpu.VMEM((1, ncol), F32),
                        pltpu.VMEM((MLA_KV_RANK, ncol), F32), pltpu.VMEM((kb, tq), F32)],
        compiler_params=_cparams("arbitrary"),
    )(qp, k_all, vt_all, wuv)


def _attn_sample_kernel(qp_ref, kv_ref, pe_ref, ckv_ref, kpe_ref, wuv_ref, o_ref, *, tq):
    q = jnp.concatenate([qp_ref[:, hd * QK_WIDTH:(hd + 1) * QK_WIDTH] for hd in range(MLA_HEADS)], axis=0)
    q_lat = q[:, 0:MLA_KV_RANK]
    q_pe = q[:, MLA_KV_RANK:MLA_KV_RANK + MLA_ROPE]
    kv_old, pe_old = kv_ref[0].astype(BF16), pe_ref[0].astype(BF16)
    kv_new, pe_new = ckv_ref[0].astype(BF16), kpe_ref[0].astype(BF16)
    s_old = _dot_t1(q_lat, kv_old) + _dot_t1(q_pe, pe_old)
    s_new = _dot_t1(q_lat, kv_new) + _dot_t1(q_pe, pe_new)
    m = jnp.maximum(jnp.max(s_old, axis=-1, keepdims=True), jnp.max(s_new, axis=-1, keepdims=True))
    p_old = jnp.exp2(s_old - m)
    p_new = jnp.exp2(s_new - m)
    denom = jnp.sum(p_old, axis=-1, keepdims=True) + jnp.sum(p_new, axis=-1, keepdims=True)
    o_lat = (_dot(p_old.astype(BF16), kv_old) + _dot(p_new.astype(BF16), kv_new)) / denom
    o_ref[...] = _attn_project(o_lat.astype(BF16), wuv_ref, tq)


def _attn_sample_call(qp, kv_past, pe_past, c_kv, k_pe, wuv, l):
    _, n_seq, past, _ = kv_past.shape
    tq = qp.shape[0] // n_seq
    per_seq = lambda r, w: pl.BlockSpec((1, r, w), lambda i: (i, 0, 0))
    cached = lambda w: pl.BlockSpec((None, 1, past, w), lambda i: (l, i, 0, 0))
    return pl.pallas_call(
        functools.partial(_attn_sample_kernel, tq=tq),
        grid=(n_seq,),
        in_specs=[pl.BlockSpec((tq, MLA_HEADS * QK_WIDTH), lambda i: (i, 0)), cached(MLA_KV_RANK),
                  per_seq(past, MLA_ROPE), per_seq(tq, MLA_KV_RANK), per_seq(tq, MLA_ROPE),
                  _layer_spec((MLA_KV_RANK, MLA_WIDTH), l)],
        out_specs=pl.BlockSpec((tq, MLA_WIDTH), lambda i: (i, 0)),
        out_shape=jax.ShapeDtypeStruct((qp.shape[0], MLA_WIDTH), F32),
        compiler_params=_cparams("arbitrary"),
    )(qp, kv_past, pe_past, c_kv.reshape(n_seq, tq, MLA_KV_RANK), k_pe.reshape(n_seq, tq, MLA_ROPE), wuv)


def _out_kernel(x_ref, ys_ref, u_ref, oa_ref, ob_ref, hg_ref, mla_ref, d_ref, wglu_ref, bglu_ref, og_ref,
                wout_ref, g2_ref, wup_ref, wdn_ref, fg_ref, o_ref, *, final):
    y = jnp.concatenate([ys_ref[0], ys_ref[1]], axis=-1) + d_ref[...] * u_ref[...]
    z = jax.nn.gelu(y, approximate=True)
    s5 = z * jax.nn.sigmoid(_dot(z.astype(BF16), wglu_ref[...]) + bglu_ref[...])
    og = og_ref[...]
    hgate = hg_ref[...]
    mixed = jnp.concatenate([
        _rms(s5, og[:, 0:256]),
        _rms(oa_ref[...] + ob_ref[...], og[:, 256:512]) * (hgate * jax.nn.sigmoid(hgate)),
        _rms(mla_ref[...], og[:, 512:1024]),
    ], axis=-1)
    x1 = x_ref[...] + _dot(mixed.astype(BF16), wout_ref[...])
    h2 = _rms(x1, g2_ref[...]).astype(BF16)
    acc = x1
    for c in range(D_FF // FF_CHUNK):
        up = _dot(h2, wup_ref[:, c * FF_CHUNK:(c + 1) * FF_CHUNK])
        act = jnp.square(jnp.maximum(up, 0.0)).astype(BF16)
        acc = acc + _dot(act, wdn_ref[c * FF_CHUNK:(c + 1) * FF_CHUNK, :])
    if final:
        acc = _rms(acc, fg_ref[...])
    o_ref[...] = acc


def _out_call(x, ys, u, oa, ob, hg, mla, lw, l, final_g, tm, final):
    t = x.shape[0]
    row = lambda w: pl.BlockSpec((tm, w), lambda i: (i, 0))
    return pl.pallas_call(
        functools.partial(_out_kernel, final=final),
        grid=(t // tm,),
        in_specs=[row(D_MODEL), pl.BlockSpec((2, tm, LANES), lambda i: (0, i, 0)), row(256), row(256), row(256),
                  row(256), row(MLA_WIDTH),
                  _layer_spec((1, 256), l), _layer_spec((256, 256), l), _layer_spec((1, 256), l),
                  _layer_spec((1, D_MODEL), l), _layer_spec((D_MODEL, D_MODEL), l), _layer_spec((1, D_MODEL), l),
                  _layer_spec((D_MODEL, D_FF), l), _layer_spec((D_FF, D_MODEL), l), _const_spec((1, D_MODEL))],
        out_specs=row(D_MODEL),
        out_shape=jax.ShapeDtypeStruct((t, D_MODEL), F32),
        compiler_params=_cparams("arbitrary"),
    )(x, ys, u, oa, ob, hg, mla, lw["s5_d"], lw["s5_w_glu"], lw["s5_b_glu"], lw["out_norm_g"], lw["w_out"],
      lw["norm2_g"], lw["w_up"], lw["w_down"], final_g)


def _rot_cols(w):
    half = MLA_ROPE // 2
    return jnp.concatenate([-w[..., half:], w[..., :half]], axis=-1)


def _prep_mla(w_in, w_uq, w_uk, w_uv):
    nl = w_in.shape[0]
    kpe = w_in[..., 1664:1696]
    pad = jnp.zeros((nl, D_MODEL, 96), F32)
    w_in_p = jnp.concatenate([w_in[..., :1664], kpe, pad, _rot_cols(kpe), pad], axis=-1).astype(BF16)
    uq = w_uq.reshape(nl, MLA_Q_RANK, MLA_HEADS, MLA_NOPE + MLA_ROPE)
    nope = uq[..., :MLA_NOPE].reshape(nl, MLA_Q_RANK, 512)
    pe = uq[..., MLA_NOPE:]
    w_uq_p = jnp.concatenate([nope, pe.reshape(nl, MLA_Q_RANK, 256), _rot_cols(pe).reshape(nl, MLA_Q_RANK, 256)],
                             axis=-1).astype(BF16)
    eye_h = jnp.eye(MLA_HEADS, dtype=F32)
    wuk_bd = jnp.einsum("lchd,hk->lhdkc", w_uk, eye_h).reshape(nl, 512, 1024).astype(BF16)
    place = jnp.einsum("hk,rc->hrkc", eye_h, jnp.eye(MLA_ROPE, LANES, dtype=F32)).reshape(256, 1024).astype(BF16)
    return w_in_p, w_uq_p, wuk_bd, place, w_uv.reshape(nl, MLA_KV_RANK, MLA_WIDTH).astype(BF16)


def _prep_s5(lam_re, lam_im, log_dt, b_re, b_im, c_re, c_im):
    t = S5_T
    dt = jnp.exp(log_dt)[:, None]
    mag1 = jnp.exp(lam_re * dt)
    a_re, a_im = mag1 * jnp.cos(lam_im * dt), mag1 * jnp.sin(lam_im * dt)
    pw_re, pw_im = [jnp.ones_like(a_re)], [jnp.zeros_like(a_im)]
    for _ in range(t):
        pr, pi = pw_re[-1], pw_im[-1]
        pw_re.append(pr * a_re - pi * a_im)
        pw_im.append(pr * a_im + pi * a_re)
    p_re, p_im = jnp.stack(pw_re), jnp.stack(pw_im)
    den = lam_re * lam_re + lam_im * lam_im
    i_re, i_im = lam_re / den, -lam_im / den
    z_re = (a_re - 1.0) * i_re - a_im * i_im
    z_im = (a_re - 1.0) * i_im + a_im * i_re
    bb_re = z_re[..., None] * b_re - z_im[..., None] * b_im
    bb_im = z_re[..., None] * b_im + z_im[..., None] * b_re
    cp_re = c_re[None] * p_re[:, :, None, :] - c_im[None] * p_im[:, :, None, :]
    cp_im = c_re[None] * p_im[:, :, None, :] + c_im[None] * p_re[:, :, None, :]
    kern = (cp_re[:t][..., None] * bb_re[None, :, None] - cp_im[:t][..., None] * bb_im[None, :, None]).sum(axis=3)
    kd = jnp.concatenate([kern[t - 1 - j].transpose(0, 2, 1) for j in range(t)], axis=0)
    kd = jnp.tile(kd.reshape(S5_ROW, S5_GROUP), (1, LANES // S5_GROUP))
    rev_re, rev_im = jnp.stack(pw_re[t - 1::-1]), jnp.stack(pw_im[t - 1::-1])
    w2_re = rev_re[..., None] * bb_re[None] - rev_im[..., None] * bb_im[None]
    w2_im = rev_re[..., None] * bb_im[None] + rev_im[..., None] * bb_re[None]
    w2c = jnp.stack([w2_re, w2_im]).transpose(1, 2, 4, 0, 3).reshape(S5_ROW, 2 * S5_STATE)
    cp3c = jnp.stack([cp_re[1:], -cp_im[1:]]).transpose(0, 2, 4, 1, 3).reshape(S5_FLAT, t * S5_GROUP)
    ab = jnp.concatenate([p_re[t].reshape(1, -1), p_im[t].reshape(1, -1)], axis=1)
    return {"kd": kd, "w2c": w2c.astype(BF16), "cp3c": cp3c.astype(BF16), "ab": ab}


def _lane_repeaters(width):
    src = jnp.arange(LANES)[:, None]
    dst = jnp.arange(LANES)[None, :]
    return jnp.stack([(src == k * width + dst % width) for k in range(LANES // width)]).astype(BF16)


def _hg_consts(r_len):
    tt = HG_ROWS * r_len
    r = jnp.arange(tt)
    same = (r[:, None] // r_len) == (r[None, :] // r_len)
    tril = (same & (r[None, :] <= r[:, None])).astype(BF16)
    hd = jnp.arange(HG_WIDTH) // HG_DK
    hmask = (hd[:, None] == hd[None, :]).astype(F32)
    vsel = jnp.tile(jnp.eye(HG_DV, dtype=BF16), (HG_HEADS, 1))
    return {"tril": tril, "blk": same.astype(BF16), "hmask": hmask, "vsel": vsel}


def _rope_tables(pos):
    half = MLA_ROPE // 2
    inv = ROPE_THETA ** (-jnp.arange(half, dtype=F32) / half)
    ang = pos.astype(F32)[:, None] * inv[None, :]
    reps = LANES // half
    return jnp.tile(jnp.cos(ang), (1, reps)), jnp.tile(jnp.sin(ang), (1, reps))


def _state_from_bd(s):
    blocks = [s[:, h * HG_DV:(h + 1) * HG_DV, h * HG_DK:(h + 1) * HG_DK].swapaxes(1, 2) for h in range(HG_HEADS)]
    return jnp.stack(blocks, axis=1)


def _layer(x, lw, sw, l, cos, sin, lb, hgc, ones_bd, final_g, final, *, prompt, n_seq, s5_h0, hg_s0, kv_past,
           pe_past):
    t = x.shape[0]
    seq = t // n_seq
    tm = ROW_TILE
    u, u_rows, hq, hf, hi, hg, c_kv, k_pe, qp, kk, vt = _in_call(x, lw, l, cos, sin, tm)

    if prompt:
        ys, s5_fin = _s5_prompt_call(u_rows, sw, l, min(256, t // S5_T))
    else:
        ys, s5_fin = _s5_sample_call(u_rows.reshape(n_seq, seq * S5_WIDTH), s5_h0, sw, l)
        ys = ys.reshape(t, 2, LANES).swapaxes(0, 1)

    r_len = 32 if prompt else seq
    oa = _hg_intra_call(hq, hf, hi, lb, l, ones_bd, r_len)
    ob, hg_fin = _hg_inter_call(hq, hf, hi, lb, l, hgc, hg_s0, r_len, carry=prompt)

    if prompt:
        kb = min(KEY_BLOCK_PROMPT, t)
        mla = _attn_call(qp, kk.reshape(t // kb, kb, QK_WIDTH), vt, lw["w_uv"], l, ATTN_QUERIES)
    else:
        mla = _attn_sample_call(qp, kv_past, pe_past[l], c_kv, k_pe, lw["w_uv"], l)

    x_new = _out_call(x, ys, u, oa, ob, hg, mla, lw, l, final_g, tm, final)
    return x_new, c_kv, k_pe, hg_fin, s5_fin


def kernel(x_prompt, x_sample, cache_mla_kv, cache_mla_pe, state_hgrn, state_s5_re, state_s5_im, norm1_g, w_in, s5_lambda_re, s5_lambda_im, s5_log_dt, s5_b_re, s5_b_im, s5_c_re, s5_c_im, s5_d, s5_w_glu, s5_b_glu, hgrn_lb_logits, mla_q_norm_g, mla_w_uq, mla_kv_norm_g, mla_w_uk, mla_w_uv, out_norm_g, w_out, norm2_g, w_up, w_down, final_norm_g):
    depth = w_in.shape[0]
    bp, lp = x_prompt.shape[0], x_prompt.shape[1]
    bs, ls = x_sample.shape[0], x_sample.shape[1]
    past = cache_mla_kv.shape[2]
    assert bp == 1 and ls == 2 * S5_T and bs % HG_ROWS == 0
    assert past % CHUNK == 0 and ls <= CHUNK

    cos_p, sin_p = _rope_tables(jnp.arange(lp, dtype=jnp.int32))
    cos_s, sin_s = _rope_tables(past + jnp.arange(ls, dtype=jnp.int32))
    cos_s, sin_s = jnp.tile(cos_s, (bs, 1)), jnp.tile(sin_s, (bs, 1))

    lb_p = jax.nn.softmax(hgrn_lb_logits.astype(F32), axis=0)
    lb_all = jnp.cumsum(lb_p, axis=0) - lb_p[0]
    hgc_p, hgc_s = _hg_consts(32), _hg_consts(ls)
    hd = jnp.arange(HG_WIDTH) // HG_DK
    ones_bd = (hd[:, None] == hd[None, :]).astype(BF16)
    rows = lambda v: v.reshape(depth, 1, -1).astype(F32)
    final_g = final_norm_g.reshape(1, -1).astype(F32)

    w_in_p, w_uq_p, wuk_bd, place, w_uv_p = _prep_mla(w_in, mla_w_uq, mla_w_uk, mla_w_uv)
    lw = {
        "norm1_g": rows(norm1_g), "w_in": w_in_p, "q_norm_g": rows(mla_q_norm_g), "w_uq": w_uq_p,
        "w_uk": wuk_bd, "place": place, "kv_norm_g": rows(mla_kv_norm_g), "w_uv": w_uv_p,
        "s5_d": rows(s5_d), "s5_w_glu": s5_w_glu.astype(BF16), "s5_b_glu": rows(s5_b_glu),
        "out_norm_g": rows(out_norm_g), "w_out": w_out.astype(BF16), "norm2_g": rows(norm2_g),
        "w_up": w_up.astype(BF16), "w_down": w_down.astype(BF16),
    }
    sw = jax.vmap(_prep_s5)(s5_lambda_re, s5_lambda_im, s5_log_dt, s5_b_re, s5_b_im, s5_c_re, s5_c_im)
    sw["rep2"], sw["rep3"] = _lane_repeaters(S5_STATE), _lane_repeaters(S5_GROUP)
    lb = rows(lb_all)
    s5_h0 = jnp.concatenate([state_s5_re.reshape(depth, bs, -1), state_s5_im.reshape(depth, bs, -1)], axis=-1)
    hg_zero = jnp.zeros((1, HG_WIDTH, HG_WIDTH), F32)

    xp = x_prompt.reshape(bp * lp, D_MODEL)
    xs = x_sample.reshape(bs * ls, D_MODEL)
    outs_p, outs_s = [], []
    for l in range(depth):
        final = l == depth - 1
        xp, a, b, c, d = _layer(xp, lw, sw, l, cos_p, sin_p, lb, hgc_p, ones_bd, final_g, final, prompt=True,
                                n_seq=1, s5_h0=None, hg_s0=hg_zero, kv_past=None, pe_past=None)
        outs_p.append((a, b, c, d))
        xs, a, b, c, d = _layer(xs, lw, sw, l, cos_s, sin_s, lb, hgc_s, ones_bd, final_g, final, prompt=False,
                                n_seq=bs, s5_h0=s5_h0, hg_s0=state_hgrn.astype(F32), kv_past=cache_mla_kv, pe_past=cache_mla_pe)
        outs_s.append((a, b, c, d))

    def gather(outs, nb, sl):
        kv = jnp.stack([o[0].reshape(nb, sl, MLA_KV_RANK) for o in outs])
        pe = jnp.stack([o[1].reshape(nb, sl, MLA_ROPE) for o in outs])
        hg = jnp.stack([o[2] if o[2].ndim == 4 else _state_from_bd(o[2]) for o in outs])
        half = S5_FLAT // 2
        re = jnp.stack([o[3][:, :half].reshape(nb, S5_GROUPS, S5_STATE) for o in outs])
        im = jnp.stack([o[3][:, half:].reshape(nb, S5_GROUPS, S5_STATE) for o in outs])
        return kv, pe, hg, re, im

    p_kv, p_pe, p_hg, p_re, p_im = gather(outs_p, bp, lp)
    s_kv, s_pe, s_hg, s_re, s_im = gather(outs_s, bs, ls)
    return (xp.reshape(bp, lp, D_MODEL), xs.reshape(bs, ls, D_MODEL),
            p_kv, p_pe, p_hg, p_re, p_im, s_kv, s_pe, s_hg, s_re, s_im)
```

```python
import functools
import math

import jax
import jax.numpy as jnp
from jax import lax
from jax.experimental import pallas as pl
from jax.experimental.pallas import tpu as pltpu

F32 = jnp.float32
BF16 = jnp.bfloat16

D_MODEL = 1024
CHUNK = 64
CHUNK_SHIFT = CHUNK.bit_length() - 1
EPS = 1e-5
NEG_BIG = -1e30

S5_WIDTH = 256
S5_GROUP = 16
S5_GROUPS = 16
S5_STATE = 64
S5_T = 8
S5_ROW = S5_T * S5_WIDTH
S5_FLAT = 2 * S5_GROUPS * S5_STATE

HG_HEADS = 4
HG_DK = 64
HG_DV = 64
HG_WIDTH = 256
HG_ROWS = 16
HG_PITCH_PAD = 4

MLA_HEADS = 8
MLA_Q_RANK = 256
MLA_KV_RANK = 128
MLA_NOPE = 64
MLA_ROPE = 32
MLA_V = 64
MLA_WIDTH = 512
ROPE_THETA = 10000.0
ROW_TILE = 512
KEY_BLOCK_PROMPT = 512
QK_WIDTH = 256
LOG2E = 1.4426950408889634
ATTN_COLS = 512
ATTN_QUERIES = 512

D_FF = 4096
FF_CHUNK = 1024
IN_PAD = 1920

LANES = 128
VMEM_LIMIT = 56 * 1024 * 1024


def _cparams(*sem):
    return pltpu.CompilerParams(dimension_semantics=sem, vmem_limit_bytes=VMEM_LIMIT)


def _const_spec(shape):
    nd = len(shape)
    return pl.BlockSpec(shape, lambda *_: (0,) * nd, pipeline_mode=pl.Buffered(1))


def _layer_spec(shape, l):
    nd = len(shape)
    return pl.BlockSpec((None,) + tuple(shape), lambda *_: (l,) + (0,) * nd, pipeline_mode=pl.Buffered(1))


def _rms(x, g):
    y = x * lax.rsqrt(jnp.mean(x * x, axis=-1, keepdims=True) + EPS)
    return y * g


def _dot(a, b):
    return jnp.dot(a, b, preferred_element_type=F32)


def _dot_t0(a, b):
    return lax.dot_general(a, b, (((0,), (0,)), ((), ())), preferred_element_type=F32)


def _dot_t1(a, b):
    return lax.dot_general(a, b, (((1,), (1,)), ((), ())), preferred_element_type=F32)


def _split3(x):
    hi = x.astype(BF16)
    r1 = x - hi.astype(F32)
    mid = r1.astype(BF16)
    lo = (r1 - mid.astype(F32)).astype(BF16)
    return hi, mid, lo


def _in_kernel(x_ref, g1_ref, win_ref, qg_ref, wuq_ref, wuk_ref, place_ref, kvg_ref, cos_ref, sin_ref,
               u_ref, urow_ref, hq_ref, hf_ref, hi_ref, hg_ref, ckv_ref, kpe_ref, qp_ref, kk_ref, vt_ref, u_scr):
    h = _rms(x_ref[...], g1_ref[...])
    proj = _dot(h.astype(BF16), win_ref[...])
    u_ref[...] = proj[:, 0:256]
    n_row = proj.shape[0] // S5_T
    for hv in range(2):
        u_scr[hv] = proj[:, hv * LANES:(hv + 1) * LANES]
    for s in range(S5_T):
        for hv in range(2):
            lo = s * S5_WIDTH + hv * LANES
            urow_ref[:, lo:lo + LANES] = u_scr[hv, pl.ds(s, n_row, stride=S5_T), :]
    hq_ref[...] = proj[:, 256:512]
    hf_ref[...] = proj[:, 512:768]
    hi_ref[...] = proj[:, 768:1024]
    hg_ref[...] = proj[:, 1024:1280]

    cos = cos_ref[...]
    sin = sin_ref[...]
    cos2 = jnp.concatenate([cos, cos], axis=-1)
    sin2 = jnp.concatenate([sin, sin], axis=-1)

    cqn = _rms(proj[:, 1280:1536], qg_ref[...])
    q = _dot(cqn.astype(BF16), wuq_ref[...])
    scale = (MLA_NOPE + MLA_ROPE) ** -0.5 * LOG2E
    q_pe = (q[:, 512:768] * cos2 + q[:, 768:1024] * sin2) * scale
    q_lat = _dot(q[:, 0:512].astype(BF16), wuk_ref[...]) * scale
    pe_pl = _dot(q_pe.astype(BF16), place_ref[...])
    for hd in range(MLA_HEADS):
        qp_ref[:, hd * 256:hd * 256 + 128] = q_lat[:, hd * 128:(hd + 1) * 128].astype(BF16)
        qp_ref[:, hd * 256 + 128:(hd + 1) * 256] = pe_pl[:, hd * 128:(hd + 1) * 128].astype(BF16)

    c_kv = _rms(proj[:, 1536:1664], kvg_ref[...])
    ckv_ref[...] = c_kv
    kpe = proj[:, 1664:1792] * cos + proj[:, 1792:1920] * sin
    kpe_ref[...] = kpe[:, 0:MLA_ROPE]
    kk_ref[...] = jnp.concatenate([c_kv, kpe], axis=-1).astype(BF16)
    vt_ref[0] = c_kv.T.astype(BF16)


def _in_call(x, lw, l, cos, sin, tm):
    t = x.shape[0]
    row = lambda w: pl.BlockSpec((tm, w), lambda i: (i, 0))
    outs = [jax.ShapeDtypeStruct((t, 256), F32), jax.ShapeDtypeStruct((t // S5_T, S5_ROW), F32)] + [
        jax.ShapeDtypeStruct((t, 256), F32)] * 4 + [
        jax.ShapeDtypeStruct((t, MLA_KV_RANK), F32),
        jax.ShapeDtypeStruct((t, MLA_ROPE), F32),
        jax.ShapeDtypeStruct((t, MLA_HEADS * QK_WIDTH), BF16),
        jax.ShapeDtypeStruct((t, QK_WIDTH), BF16),
        jax.ShapeDtypeStruct((t // tm, MLA_KV_RANK, tm), BF16),
    ]
    out_specs = [row(256), pl.BlockSpec((tm // S5_T, S5_ROW), lambda i: (i, 0))] + [row(256)] * 4 + [
        row(MLA_KV_RANK), row(MLA_ROPE), row(MLA_HEADS * QK_WIDTH), row(QK_WIDTH),
        pl.BlockSpec((1, MLA_KV_RANK, tm), lambda i: (i, 0, 0))]
    return pl.pallas_call(
        _in_kernel,
        grid=(t // tm,),
        in_specs=[row(D_MODEL), _layer_spec((1, D_MODEL), l), _layer_spec((D_MODEL, IN_PAD), l),
                  _layer_spec((1, MLA_Q_RANK), l), _layer_spec((MLA_Q_RANK, 1024), l), _layer_spec((512, 1024), l),
                  _const_spec((256, 1024)), _layer_spec((1, MLA_KV_RANK), l), row(LANES), row(LANES)],
        out_specs=out_specs,
        out_shape=outs,
        scratch_shapes=[pltpu.VMEM((2, tm, LANES), F32)],
        compiler_params=_cparams("arbitrary"),
    )(x, lw["norm1_g"], lw["w_in"], lw["q_norm_g"], lw["w_uq"], lw["w_uk"], lw["place"], lw["kv_norm_g"],
      cos, sin)


def _s5_expand(kd_ref, w2c_ref, cp3c_ref, rep2_ref, rep3_ref, krev_scr, m2_scr, m3_scr):
    assert S5_ROW == S5_FLAT
    lane = lax.broadcasted_iota(jnp.int32, (S5_ROW, LANES), 1)
    row = lax.broadcasted_iota(jnp.int32, (S5_ROW, LANES), 0)
    g_in = (row // S5_GROUP) % S5_GROUPS
    g_st = (row // S5_STATE) % S5_GROUPS
    per_state = LANES // S5_STATE
    per_chan = LANES // S5_GROUP
    for hv in range(S5_WIDTH // LANES):
        keep = g_in == hv * per_chan + lane // S5_GROUP
        krev_scr[:, hv * LANES:(hv + 1) * LANES] = jnp.where(keep, kd_ref[...], 0.0).astype(BF16)
    for ri in range(2):
        src = _dot(w2c_ref[...], rep2_ref[ri])
        for k in range(S5_GROUPS // per_state):
            keep = g_in == k * per_state + lane // S5_STATE
            lo = (ri * (S5_GROUPS // per_state) + k) * LANES
            m2_scr[:, lo:lo + LANES] = jnp.where(keep, src, 0.0).astype(BF16)
    for t in range(S5_T):
        src = _dot(cp3c_ref[...], rep3_ref[t])
        for hv in range(S5_WIDTH // LANES):
            keep = g_st == hv * per_chan + lane // S5_GROUP
            lo = t * S5_WIDTH + hv * LANES
            m3_scr[:, lo:lo + LANES] = jnp.where(keep, src, 0.0).astype(BF16)


def _s5_in_row(u, krev_ref):
    w = S5_WIDTH
    parts = [_dot(u[:, 0:(t + 1) * w], krev_ref[(S5_T - 1 - t) * w:S5_T * w, :]) for t in range(S5_T)]
    return jnp.concatenate(parts, axis=1)


def _s5_prompt_kernel(u_ref, kd_ref, w2c_ref, cp3c_ref, rep2_ref, rep3_ref, ab_ref, y_ref, hfin_ref,
                      s_scr, hp_scr, h_scr, krev_ref, m2_ref, m3_ref):
    half = S5_FLAT // 2
    tn = u_ref.shape[0]

    @pl.when(pl.program_id(0) == 0)
    def _():
        h_scr[...] = jnp.zeros_like(h_scr)
        _s5_expand(kd_ref, w2c_ref, cp3c_ref, rep2_ref, rep3_ref, krev_ref, m2_ref, m3_ref)

    u = u_ref[...].astype(BF16)
    s_scr[...] = _dot(u, m2_ref[...])
    ar = ab_ref[:, 0:half]
    ai = ab_ref[:, half:S5_FLAT]

    def body(i, carry):
        hr, hi = carry
        hp_scr[pl.ds(i, 1), 0:half] = hr
        hp_scr[pl.ds(i, 1), half:S5_FLAT] = hi
        sr = s_scr[pl.ds(i, 1), 0:half]
        si = s_scr[pl.ds(i, 1), half:S5_FLAT]
        return ar * hr - ai * hi + sr, ar * hi + ai * hr + si

    hr, hi = lax.fori_loop(0, tn, body, (h_scr[:, 0:half], h_scr[:, half:S5_FLAT]), unroll=8)
    h_scr[:, 0:half] = hr
    h_scr[:, half:S5_FLAT] = hi
    y = _s5_in_row(u, krev_ref) + _dot(hp_scr[...].astype(BF16), m3_ref[...])
    for s in range(S5_T):
        for hv in range(2):
            lo = s * S5_WIDTH + hv * LANES
            y_ref[hv, pl.ds(s, tn, stride=S5_T), :] = y[:, lo:lo + LANES]
    hfin_ref[...] = h_scr[...]


def _s5_matrix_scratch():
    return [pltpu.VMEM((S5_ROW, S5_WIDTH), BF16), pltpu.VMEM((S5_ROW, S5_FLAT), BF16),
            pltpu.VMEM((S5_FLAT, S5_ROW), BF16)]


def _s5_prompt_call(u_rows, sw, l, tn):
    n = u_rows.shape[0]
    return pl.pallas_call(
        _s5_prompt_kernel,
        grid=(n // tn,),
        in_specs=[pl.BlockSpec((tn, S5_ROW), lambda i: (i, 0)),
                  _layer_spec((S5_ROW, LANES), l), _layer_spec((S5_ROW, LANES), l), _layer_spec((S5_FLAT, LANES), l),
                  _const_spec((2, LANES, LANES)), _const_spec((S5_T, LANES, LANES)), _layer_spec((1, S5_FLAT), l)],
        out_specs=[pl.BlockSpec((2, tn * S5_T, LANES), lambda i: (0, i, 0)),
                   pl.BlockSpec((1, S5_FLAT), lambda i: (0, 0))],
        out_shape=[jax.ShapeDtypeStruct((2, n * S5_T, LANES), F32), jax.ShapeDtypeStruct((1, S5_FLAT), F32)],
        scratch_shapes=[pltpu.VMEM((tn, S5_FLAT), F32), pltpu.VMEM((tn, S5_FLAT), F32),
                        pltpu.VMEM((1, S5_FLAT), F32)] + _s5_matrix_scratch(),
        compiler_params=_cparams("arbitrary"),
    )(u_rows, sw["kd"], sw["w2c"], sw["cp3c"], sw["rep2"], sw["rep3"], sw["ab"])


def _s5_sample_kernel(u_ref, h0_ref, kd_ref, w2c_ref, cp3c_ref, rep2_ref, rep3_ref, ab_ref, y_ref, hfin_ref,
                      krev_ref, m2_ref, m3_ref, *, n_rows):
    half = S5_FLAT // 2
    _s5_expand(kd_ref, w2c_ref, cp3c_ref, rep2_ref, rep3_ref, krev_ref, m2_ref, m3_ref)
    ar = ab_ref[:, 0:half]
    ai = ab_ref[:, half:S5_FLAT]
    us = [u_ref[:, c * S5_ROW:(c + 1) * S5_ROW].astype(BF16) for c in range(n_rows)]
    u_all = jnp.concatenate(us, axis=0)
    b = us[0].shape[0]
    s_all = _dot(u_all, m2_ref[...])
    y1_all = _s5_in_row(u_all, krev_ref)
    hr = h0_ref[:, 0:half]
    hi = h0_ref[:, half:S5_FLAT]
    enter = []
    for c in range(n_rows):
        enter.append(jnp.concatenate([hr, hi], axis=-1))
        sr = s_all[c * b:(c + 1) * b, 0:half]
        si = s_all[c * b:(c + 1) * b, half:S5_FLAT]
        hr, hi = ar * hr - ai * hi + sr, ar * hi + ai * hr + si
    y_all = y1_all + _dot(jnp.concatenate(enter, axis=0).astype(BF16), m3_ref[...])
    for c in range(n_rows):
        y_ref[:, c * S5_ROW:(c + 1) * S5_ROW] = y_all[c * b:(c + 1) * b, :]
    hfin_ref[:, 0:half] = hr
    hfin_ref[:, half:S5_FLAT] = hi


def _s5_sample_call(u_seq, h0, sw, l):
    b, w = u_seq.shape
    n_rows = w // S5_ROW
    full = lambda shape: pl.BlockSpec(shape, lambda i: (0,) * len(shape))
    return pl.pallas_call(
        functools.partial(_s5_sample_kernel, n_rows=n_rows),
        grid=(1,),
        in_specs=[full((b, w)), _layer_spec((b, S5_FLAT), l), _layer_spec((S5_ROW, LANES), l),
                  _layer_spec((S5_ROW, LANES), l), _layer_spec((S5_FLAT, LANES), l),
                  _const_spec((2, LANES, LANES)), _const_spec((S5_T, LANES, LANES)), _layer_spec((1, S5_FLAT), l)],
        out_specs=[full((b, w)), full((b, S5_FLAT))],
        out_shape=[jax.ShapeDtypeStruct((b, w), F32), jax.ShapeDtypeStruct((b, S5_FLAT), F32)],
        scratch_shapes=_s5_matrix_scratch(),
        compiler_params=_cparams("arbitrary"),
    )(u_seq, h0, sw["kd"], sw["w2c"], sw["cp3c"], sw["rep2"], sw["rep3"], sw["ab"])


def _hg_gates(hq, hf, lb):
    sig = jax.nn.sigmoid(hf)
    f = lb + (1.0 - lb) * sig
    k = (1.0 - lb) * jax.nn.sigmoid(-hf)
    qf = hq * jax.nn.sigmoid(hq)
    return qf, k, f


def _hg_intra_kernel(hq_ref, hf_ref, hi_ref, lb_ref, ones_ref, o_ref, pad_scr, out_scr, k_scr, f_scr, *, r_len):
    lb = lb_ref[...]
    ones_bd = ones_ref[...]
    pitch = r_len + HG_PITCH_PAD
    for a, ref in enumerate((hq_ref, hf_ref, hi_ref)):
        for hv in range(2):
            for j in range(HG_ROWS):
                pad_scr[a, hv, pl.ds(j * pitch, r_len), :] = ref[j * r_len:(j + 1) * r_len, hv * LANES:(hv + 1) * LANES]

    def slab(a, r):
        return jnp.concatenate([pad_scr[a, hv, pl.ds(r, HG_ROWS, stride=pitch), :] for hv in range(2)], axis=-1)

    for r in range(r_len):
        _, k, f = _hg_gates(slab(0, r), slab(1, r), lb)
        k_scr[r] = k
        f_scr[r] = f
    for r in range(r_len):
        hq = slab(0, r)
        qp = hq * jax.nn.sigmoid(hq)
        terms = []
        for s in range(r, -1, -1):
            terms.append((qp * k_scr[s]).astype(BF16))
            if s > 0:
                qp = qp * f_scr[s]
        att = _dot(jnp.concatenate(terms, axis=0), ones_bd)
        acc = None
        for j, s in enumerate(range(r, -1, -1)):
            part = att[j * HG_ROWS:(j + 1) * HG_ROWS, :] * slab(2, s)
            acc = part if acc is None else acc + part
        for hv in range(2):
            out_scr[hv, pl.ds(r, HG_ROWS, stride=pitch), :] = acc[:, hv * LANES:(hv + 1) * LANES]
    for hv in range(2):
        for j in range(HG_ROWS):
            o_ref[j * r_len:(j + 1) * r_len, hv * LANES:(hv + 1) * LANES] = out_scr[hv, pl.ds(j * pitch, r_len), :]


def _hg_intra_call(hq, hf, hi, lb, l, ones_bd, r_len):
    t = hq.shape[0]
    tt = HG_ROWS * r_len
    pad_rows = HG_ROWS * (r_len + HG_PITCH_PAD)
    spec = pl.BlockSpec((tt, HG_WIDTH), lambda i: (i, 0))
    return pl.pallas_call(
        functools.partial(_hg_intra_kernel, r_len=r_len),
        grid=(t // tt,),
        in_specs=[spec, spec, spec, _layer_spec((1, HG_WIDTH), l), _const_spec((HG_WIDTH, HG_WIDTH))],
        out_specs=spec,
        out_shape=jax.ShapeDtypeStruct(hq.shape, F32),
        scratch_shapes=[pltpu.VMEM((3, 2, pad_rows, LANES), F32), pltpu.VMEM((2, pad_rows, LANES), F32),
                        pltpu.VMEM((r_len, HG_ROWS, HG_WIDTH), F32), pltpu.VMEM((r_len, HG_ROWS, HG_WIDTH), F32)],
        compiler_params=_cparams("arbitrary"),
    )(hq, hf, hi, lb, ones_bd)


def _hg_inter_kernel(hq_ref, hf_ref, hi_ref, lb_ref, tril_ref, blk_ref, hmask_ref, vsel_ref, s0_ref,
                     o_ref, sout_ref, s_scr, *, r_len, carry):
    n_sub = HG_ROWS
    if carry:
        @pl.when(pl.program_id(0) == 0)
        def _():
            s_scr[...] = s0_ref[0]

    qf, k, f = _hg_gates(hq_ref[...], hf_ref[...], lb_ref[...])
    g = jnp.log(f)
    g3 = _split3(g)
    tril = tril_ref[...]
    blk = blk_ref[...]
    bl = _dot(tril, g3[0]) + _dot(tril, g3[1]) + _dot(tril, g3[2])
    bsum = _dot(blk, g3[0]) + _dot(blk, g3[1]) + _dot(blk, g3[2])
    q_in = (qf * jnp.exp(bl)).astype(BF16)
    k_out = (k * jnp.exp(bsum - bl)).astype(BF16)
    v = hi_ref[...].astype(BF16)
    hmask = hmask_ref[...]
    vsel = vsel_ref[...]
    decay = jnp.exp(bsum)
    for j in range(n_sub):
        sl = slice(j * r_len, (j + 1) * r_len)
        if carry:
            s_in = s_scr[...]
        else:
            r3 = _split3(s0_ref[j].reshape(HG_WIDTH, HG_DV))
            s_in = (_dot_t1(vsel, r3[0]) + _dot_t1(vsel, r3[1]) + _dot_t1(vsel, r3[2])) * hmask
        o_ref[sl, :] = _dot_t1(q_in[sl], s_in.astype(BF16))
        w_new = _dot_t0(v[sl], k_out[sl]) * hmask
        s_new = decay[j * r_len:j * r_len + 1, :] * s_in + w_new
        if carry:
            s_scr[...] = s_new
        else:
            n3 = _split3(s_new)
            raw = _dot_t0(n3[0], vsel) + _dot_t0(n3[1], vsel) + _dot_t0(n3[2], vsel)
            sout_ref[j] = raw.reshape(HG_HEADS, HG_DK, HG_DV)
    if carry:
        sout_ref[0] = s_scr[...]


def _hg_inter_call(hq, hf, hi, lb, l, consts, s0, r_len, carry):
    t = hq.shape[0]
    tt = HG_ROWS * r_len
    spec = pl.BlockSpec((tt, HG_WIDTH), lambda i: (i, 0))
    if carry:
        s_in_spec = pl.BlockSpec((1, HG_WIDTH, HG_WIDTH), lambda i: (0, 0, 0))
        s_out_spec = s_in_spec
        out_state = jax.ShapeDtypeStruct((1, HG_WIDTH, HG_WIDTH), F32)
    else:
        s_in_spec = pl.BlockSpec((None, HG_ROWS, HG_HEADS, HG_DK, HG_DV), lambda i: (l, i, 0, 0, 0))
        s_out_spec = pl.BlockSpec((HG_ROWS, HG_HEADS, HG_DK, HG_DV), lambda i: (i, 0, 0, 0))
        out_state = jax.ShapeDtypeStruct(s0.shape[1:], F32)
    return pl.pallas_call(
        functools.partial(_hg_inter_kernel, r_len=r_len, carry=carry),
        grid=(t // tt,),
        in_specs=[spec, spec, spec, _layer_spec((1, HG_WIDTH), l), _const_spec((tt, tt)), _const_spec((tt, tt)),
                  _const_spec((HG_WIDTH, HG_WIDTH)), _const_spec((HG_WIDTH, HG_DV)), s_in_spec],
        out_specs=[spec, s_out_spec],
        out_shape=[jax.ShapeDtypeStruct((t, HG_WIDTH), F32), out_state],
        scratch_shapes=[pltpu.VMEM((HG_WIDTH, HG_WIDTH), F32)],
        compiler_params=_cparams("arbitrary"),
    )(hq, hf, hi, lb, consts["tril"], consts["blk"], consts["hmask"], consts["vsel"], s0)


def _attn_project(o_lat, wuv_ref, tq):
    o_cat = jnp.concatenate([o_lat[hd * tq:(hd + 1) * tq, :] for hd in range(MLA_HEADS)], axis=1)
    return _dot(o_cat, wuv_ref[...])


def _attn_kernel(qp_ref, k_ref, vt_ref, wuv_ref, o_ref, q_scr, s_buf, mb_buf, m_scr, l_scr, acc_scr, bias_scr,
                 *, tq, kb):
    i = pl.program_id(0)
    ncol = MLA_HEADS * tq

    @pl.when(i == 0)
    def _():
        k_chunk = lax.broadcasted_iota(jnp.int32, (kb, tq), 0) >> CHUNK_SHIFT
        q_chunk = lax.broadcasted_iota(jnp.int32, (kb, tq), 1) >> CHUNK_SHIFT
        bias_scr[...] = jnp.where(k_chunk <= q_chunk, 0.0, NEG_BIG)

    for hd in range(MLA_HEADS):
        q_scr[hd * tq:(hd + 1) * tq, :] = qp_ref[:, hd * QK_WIDTH:(hd + 1) * QK_WIDTH]
    m_scr[...] = jnp.full_like(m_scr, NEG_BIG)
    l_scr[...] = jnp.zeros_like(l_scr)
    acc_scr[...] = jnp.zeros_like(acc_scr)
    q0 = i * tq
    n_full = q0 // kb

    cw = min(ATTN_COLS, ncol)
    n_chunks = ncol // cw
    vt_w = vt_ref.shape[2]
    vt_per = kb // vt_w

    def scores(b, c, diagonal):
        cols = slice(c * cw, (c + 1) * cw)
        s = _dot_t1(k_ref[b], q_scr[cols, :])
        if diagonal is not None:
            off = (c * cw) % tq
            bias = bias_scr[:, off:off + cw]
            s = s + (bias if diagonal is True else jnp.where(diagonal, bias, 0.0))
        s_buf[c] = s
        mb_buf[c] = jnp.max(s, axis=0, keepdims=True)

    def values(b, c):
        cols = slice(c * cw, (c + 1) * cw)
        m_old = m_scr[:, cols]
        m_new = jnp.maximum(m_old, mb_buf[c])
        alpha = jnp.exp2(m_old - m_new)
        p = jnp.exp2(s_buf[c] - m_new)
        l_scr[:, cols] = alpha * l_scr[:, cols] + jnp.sum(p, axis=0, keepdims=True)
        pb = p.astype(BF16)
        pv = _dot(vt_ref[b * vt_per], pb[0:vt_w, :])
        for j in range(1, vt_per):
            pv = pv + _dot(vt_ref[b * vt_per + j], pb[j * vt_w:(j + 1) * vt_w, :])
        acc_scr[:, cols] = alpha * acc_scr[:, cols] + pv
        m_scr[:, cols] = m_new

    def step(b, diagonal_next):
        for c in range(n_chunks):
            values(b, c)
            scores(b + 1, c, diagonal_next)

    for c in range(n_chunks):
        scores(0, c, n_full == 0)

    def body(b, carry):
        step(b, None)
        return carry

    lax.fori_loop(0, n_full - 1, body, 0)

    @pl.when(n_full >= 1)
    def _():
        step(n_full - 1, True)

    for c in range(n_chunks):
        values(n_full, c)

    o_lat = (acc_scr[...] / l_scr[...]).T.astype(BF16)
    o_ref[...] = _attn_project(o_lat, wuv_ref, tq)


def _attn_call(qp, k_all, vt_all, wuv, l, tq):
    t = qp.shape[0]
    n_blocks, kb = k_all.shape[0], k_all.shape[1]
    ncol = MLA_HEADS * tq
    cw = min(ATTN_COLS, ncol)
    assert tq == kb and tq % cw == 0 and kb % CHUNK == 0
    return pl.pallas_call(
        functools.partial(_attn_kernel, tq=tq, kb=kb),
        grid=(t // tq,),
        in_specs=[pl.BlockSpec((tq, MLA_HEADS * QK_WIDTH), lambda i: (i, 0)),
                  _const_spec((n_blocks, kb, QK_WIDTH)), _const_spec(vt_all.shape),
                  _layer_spec((MLA_HEADS * MLA_KV_RANK, MLA_WIDTH), l)],
        out_specs=pl.BlockSpec((tq, MLA_WIDTH), lambda i: (i, 0)),
        out_shape=jax.ShapeDtypeStruct((t, MLA_WIDTH), F32),
        scratch_shapes=[pltpu.VMEM((ncol, QK_WIDTH), BF16), pltpu.VMEM((ncol // cw, kb, cw), F32),
                        pltpu.VMEM((ncol // cw, 1, cw), F32), pltpu.VMEM((1, ncol), F32), pltpu.VMEM((1, ncol), F32),
                        pltpu.VMEM((MLA_KV_RANK, ncol), F32), pltpu.VMEM((kb, tq), F32)],
        compiler_params=_cparams("arbitrary"),
    )(qp, k_all, vt_all, wuv)


def _attn_sample_kernel(qp_ref, kv_ref, pe_ref, ckv_ref, kpe_ref, wuv_ref, o_ref, *, tq):
    q = jnp.concatenate([qp_ref[:, hd * QK_WIDTH:(hd + 1) * QK_WIDTH] for hd in range(MLA_HEADS)], axis=0)
    q_lat = q[:, 0:MLA_KV_RANK]
    q_pe = q[:, MLA_KV_RANK:MLA_KV_RANK + MLA_ROPE]
    kv_old, pe_old = kv_ref[0].astype(BF16), pe_ref[0].astype(BF16)
    kv_new, pe_new = ckv_ref[0].astype(BF16), kpe_ref[0].astype(BF16)
    s_old = _dot_t1(q_lat, kv_old) + _dot_t1(q_pe, pe_old)
    s_new = _dot_t1(q_lat, kv_new) + _dot_t1(q_pe, pe_new)
    m = jnp.maximum(jnp.max(s_old, axis=-1, keepdims=True), jnp.max(s_new, axis=-1, keepdims=True))
    p_old = jnp.exp2(s_old - m)
    p_new = jnp.exp2(s_new - m)
    denom = jnp.sum(p_old, axis=-1, keepdims=True) + jnp.sum(p_new, axis=-1, keepdims=True)
    o_lat = (_dot(p_old.astype(BF16), kv_old) + _dot(p_new.astype(BF16), kv_new)) / denom
    o_ref[...] = _attn_project(o_lat.astype(BF16), wuv_ref, tq)


def _attn_sample_call(qp, kv_past, pe_past, c_kv, k_pe, wuv, l):
    _, n_seq, past, _ = kv_past.shape
    tq = qp.shape[0] // n_seq
    per_seq = lambda r, w: pl.BlockSpec((1, r, w), lambda i: (i, 0, 0))
    cached = lambda w: pl.BlockSpec((None, 1, past, w), lambda i: (l, i, 0, 0))
    return pl.pallas_call(
        functools.partial(_attn_sample_kernel, tq=tq),
        grid=(n_seq,),
        in_specs=[pl.BlockSpec((tq, MLA_HEADS * QK_WIDTH), lambda i: (i, 0)), cached(MLA_KV_RANK),
                  per_seq(past, MLA_ROPE), per_seq(tq, MLA_KV_RANK), per_seq(tq, MLA_ROPE),
                  _layer_spec((MLA_HEADS * MLA_KV_RANK, MLA_WIDTH), l)],
        out_specs=pl.BlockSpec((tq, MLA_WIDTH), lambda i: (i, 0)),
        out_shape=jax.ShapeDtypeStruct((qp.shape[0], MLA_WIDTH), F32),
        compiler_params=_cparams("arbitrary"),
    )(qp, kv_past, pe_past, c_kv.reshape(n_seq, tq, MLA_KV_RANK), k_pe.reshape(n_seq, tq, MLA_ROPE), wuv)


def _out_kernel(x_ref, ys_ref, u_ref, oa_ref, ob_ref, hg_ref, mla_ref, d_ref, wglu_ref, bglu_ref, og_ref,
                wout_ref, g2_ref, wup_ref, wdn_ref, fg_ref, o_ref, *, final):
    y = jnp.concatenate([ys_ref[0], ys_ref[1]], axis=-1) + d_ref[...] * u_ref[...]
    z = jax.nn.gelu(y, approximate=True)
    s5 = z * jax.nn.sigmoid(_dot(z.astype(BF16), wglu_ref[...]) + bglu_ref[...])
    og = og_ref[...]
    hgate = hg_ref[...]
    mixed = jnp.concatenate([
        _rms(s5, og[:, 0:256]),
        _rms(oa_ref[...] + ob_ref[...], og[:, 256:512]) * (hgate * jax.nn.sigmoid(hgate)),
        _rms(mla_ref[...], og[:, 512:1024]),
    ], axis=-1)
    x1 = x_ref[...] + _dot(mixed.astype(BF16), wout_ref[...])
    h2 = _rms(x1, g2_ref[...]).astype(BF16)
    acc = x1
    for c in range(D_FF // FF_CHUNK):
        up = _dot(h2, wup_ref[:, c * FF_CHUNK:(c + 1) * FF_CHUNK])
        act = jnp.square(jnp.maximum(up, 0.0)).astype(BF16)
        acc = acc + _dot(act, wdn_ref[c * FF_CHUNK:(c + 1) * FF_CHUNK, :])
    if final:
        acc = _rms(acc, fg_ref[...])
    o_ref[...] = acc


def _out_call(x, ys, u, oa, ob, hg, mla, lw, l, final_g, tm, final):
    t = x.shape[0]
    row = lambda w: pl.BlockSpec((tm, w), lambda i: (i, 0))
    return pl.pallas_call(
        functools.partial(_out_kernel, final=final),
        grid=(t // tm,),
        in_specs=[row(D_MODEL), pl.BlockSpec((2, tm, LANES), lambda i: (0, i, 0)), row(256), row(256), row(256),
                  row(256), row(MLA_WIDTH),
                  _layer_spec((1, 256), l), _layer_spec((256, 256), l), _layer_spec((1, 256), l),
                  _layer_spec((1, D_MODEL), l), _layer_spec((D_MODEL, D_MODEL), l), _layer_spec((1, D_MODEL), l),
                  _layer_spec((D_MODEL, D_FF), l), _layer_spec((D_FF, D_MODEL), l), _const_spec((1, D_MODEL))],
        out_specs=row(D_MODEL),
        out_shape=jax.ShapeDtypeStruct((t, D_MODEL), F32),
        compiler_params=_cparams("arbitrary"),
    )(x, ys, u, oa, ob, hg, mla, lw["s5_d"], lw["s5_w_glu"], lw["s5_b_glu"], lw["out_norm_g"], lw["w_out"],
      lw["norm2_g"], lw["w_up"], lw["w_down"], final_g)


def _rot_cols(w):
    half = MLA_ROPE // 2
    return jnp.concatenate([-w[..., half:], w[..., :half]], axis=-1)


def _prep_mla(w_in, w_uq, w_uk, w_uv):
    nl = w_in.shape[0]
    kpe = w_in[..., 1664:1696]
    pad = jnp.zeros((nl, D_MODEL, 96), F32)
    w_in_p = jnp.concatenate([w_in[..., :1664], kpe, pad, _rot_cols(kpe), pad], axis=-1).astype(BF16)
    uq = w_uq.reshape(nl, MLA_Q_RANK, MLA_HEADS, MLA_NOPE + MLA_ROPE)
    nope = uq[..., :MLA_NOPE].reshape(nl, MLA_Q_RANK, 512)
    pe = uq[..., MLA_NOPE:]
    w_uq_p = jnp.concatenate([nope, pe.reshape(nl, MLA_Q_RANK, 256), _rot_cols(pe).reshape(nl, MLA_Q_RANK, 256)],
                             axis=-1).astype(BF16)
    eye_h = jnp.eye(MLA_HEADS, dtype=F32)
    wuk_bd = jnp.einsum("lchd,hk->lhdkc", w_uk, eye_h).reshape(nl, 512, 1024).astype(BF16)
    place = jnp.einsum("hk,rc->hrkc", eye_h, jnp.eye(MLA_ROPE, LANES, dtype=F32)).reshape(256, 1024).astype(BF16)
    wuv_bd = jnp.einsum("lchv,hk->lhckv", w_uv, eye_h).reshape(nl, MLA_HEADS * MLA_KV_RANK, MLA_WIDTH).astype(BF16)
    return w_in_p, w_uq_p, wuk_bd, place, wuv_bd


def _prep_s5(lam_re, lam_im, log_dt, b_re, b_im, c_re, c_im):
    t = S5_T
    dt = jnp.exp(log_dt)[:, None]
    mag1 = jnp.exp(lam_re * dt)
    a_re, a_im = mag1 * jnp.cos(lam_im * dt), mag1 * jnp.sin(lam_im * dt)
    pw_re, pw_im = [jnp.ones_like(a_re)], [jnp.zeros_like(a_im)]
    for _ in range(t):
        pr, pi = pw_re[-1], pw_im[-1]
        pw_re.append(pr * a_re - pi * a_im)
        pw_im.append(pr * a_im + pi * a_re)
    p_re, p_im = jnp.stack(pw_re), jnp.stack(pw_im)
    den = lam_re * lam_re + lam_im * lam_im
    i_re, i_im = lam_re / den, -lam_im / den
    z_re = (a_re - 1.0) * i_re - a_im * i_im
    z_im = (a_re - 1.0) * i_im + a_im * i_re
    bb_re = z_re[..., None] * b_re - z_im[..., None] * b_im
    bb_im = z_re[..., None] * b_im + z_im[..., None] * b_re
    cp_re = c_re[None] * p_re[:, :, None, :] - c_im[None] * p_im[:, :, None, :]
    cp_im = c_re[None] * p_im[:, :, None, :] + c_im[None] * p_re[:, :, None, :]
    kern = (cp_re[:t][..., None] * bb_re[None, :, None] - cp_im[:t][..., None] * bb_im[None, :, None]).sum(axis=3)
    kd = jnp.concatenate([kern[t - 1 - j].transpose(0, 2, 1) for j in range(t)], axis=0)
    kd = jnp.tile(kd.reshape(S5_ROW, S5_GROUP), (1, LANES // S5_GROUP))
    rev_re, rev_im = jnp.stack(pw_re[t - 1::-1]), jnp.stack(pw_im[t - 1::-1])
    w2_re = rev_re[..., None] * bb_re[None] - rev_im[..., None] * bb_im[None]
    w2_im = rev_re[..., None] * bb_im[None] + rev_im[..., None] * bb_re[None]
    w2c = jnp.stack([w2_re, w2_im]).transpose(1, 2, 4, 0, 3).reshape(S5_ROW, 2 * S5_STATE)
    cp3c = jnp.stack([cp_re[1:], -cp_im[1:]]).transpose(0, 2, 4, 1, 3).reshape(S5_FLAT, t * S5_GROUP)
    ab = jnp.concatenate([p_re[t].reshape(1, -1), p_im[t].reshape(1, -1)], axis=1)
    return {"kd": kd, "w2c": w2c.astype(BF16), "cp3c": cp3c.astype(BF16), "ab": ab}


def _lane_repeaters(width):
    src = jnp.arange(LANES)[:, None]
    dst = jnp.arange(LANES)[None, :]
    return jnp.stack([(src == k * width + dst % width) for k in range(LANES // width)]).astype(BF16)


def _hg_consts(r_len):
    tt = HG_ROWS * r_len
    r = jnp.arange(tt)
    same = (r[:, None] // r_len) == (r[None, :] // r_len)
    tril = (same & (r[None, :] <= r[:, None])).astype(BF16)
    hd = jnp.arange(HG_WIDTH) // HG_DK
    hmask = (hd[:, None] == hd[None, :]).astype(F32)
    vsel = jnp.tile(jnp.eye(HG_DV, dtype=BF16), (HG_HEADS, 1))
    return {"tril": tril, "blk": same.astype(BF16), "hmask": hmask, "vsel": vsel}


def _rope_tables(pos):
    half = MLA_ROPE // 2
    inv = ROPE_THETA ** (-jnp.arange(half, dtype=F32) / half)
    ang = pos.astype(F32)[:, None] * inv[None, :]
    reps = LANES // half
    return jnp.tile(jnp.cos(ang), (1, reps)), jnp.tile(jnp.sin(ang), (1, reps))


def _state_from_bd(s):
    blocks = [s[:, h * HG_DV:(h + 1) * HG_DV, h * HG_DK:(h + 1) * HG_DK].swapaxes(1, 2) for h in range(HG_HEADS)]
    return jnp.stack(blocks, axis=1)


def _layer(x, lw, sw, l, cos, sin, lb, hgc, ones_bd, final_g, final, *, prompt, n_seq, s5_h0, hg_s0, kv_past,
           pe_past):
    t = x.shape[0]
    seq = t // n_seq
    tm = ROW_TILE
    u, u_rows, hq, hf, hi, hg, c_kv, k_pe, qp, kk, vt = _in_call(x, lw, l, cos, sin, tm)

    if prompt:
        ys, s5_fin = _s5_prompt_call(u_rows, sw, l, min(256, t // S5_T))
    else:
        ys, s5_fin = _s5_sample_call(u_rows.reshape(n_seq, seq * S5_WIDTH), s5_h0, sw, l)
        ys = ys.reshape(t, 2, LANES).swapaxes(0, 1)

    r_len = 32 if prompt else seq
    oa = _hg_intra_call(hq, hf, hi, lb, l, ones_bd, r_len)
    ob, hg_fin = _hg_inter_call(hq, hf, hi, lb, l, hgc, hg_s0, r_len, carry=prompt)

    if prompt:
        kb = min(KEY_BLOCK_PROMPT, t)
        mla = _attn_call(qp, kk.reshape(t // kb, kb, QK_WIDTH), vt, lw["w_uv"], l, ATTN_QUERIES)
    else:
        mla = _attn_sample_call(qp, kv_past, pe_past[l], c_kv, k_pe, lw["w_uv"], l)

    x_new = _out_call(x, ys, u, oa, ob, hg, mla, lw, l, final_g, tm, final)
    return x_new, c_kv, k_pe, hg_fin, s5_fin


def kernel(x_prompt, x_sample, cache_mla_kv, cache_mla_pe, state_hgrn, state_s5_re, state_s5_im, norm1_g, w_in, s5_lambda_re, s5_lambda_im, s5_log_dt, s5_b_re, s5_b_im, s5_c_re, s5_c_im, s5_d, s5_w_glu, s5_b_glu, hgrn_lb_logits, mla_q_norm_g, mla_w_uq, mla_kv_norm_g, mla_w_uk, mla_w_uv, out_norm_g, w_out, norm2_g, w_up, w_down, final_norm_g):
    depth = w_in.shape[0]
    bp, lp = x_prompt.shape[0], x_prompt.shape[1]
    bs, ls = x_sample.shape[0], x_sample.shape[1]
    past = cache_mla_kv.shape[2]
    assert bp == 1 and ls == 2 * S5_T and bs % HG_ROWS == 0
    assert past % CHUNK == 0 and ls <= CHUNK

    cos_p, sin_p = _rope_tables(jnp.arange(lp, dtype=jnp.int32))
    cos_s, sin_s = _rope_tables(past + jnp.arange(ls, dtype=jnp.int32))
    cos_s, sin_s = jnp.tile(cos_s, (bs, 1)), jnp.tile(sin_s, (bs, 1))

    lb_p = jax.nn.softmax(hgrn_lb_logits.astype(F32), axis=0)
    lb_all = jnp.cumsum(lb_p, axis=0) - lb_p[0]
    hgc_p, hgc_s = _hg_consts(32), _hg_consts(ls)
    hd = jnp.arange(HG_WIDTH) // HG_DK
    ones_bd = (hd[:, None] == hd[None, :]).astype(BF16)
    rows = lambda v: v.reshape(depth, 1, -1).astype(F32)
    final_g = final_norm_g.reshape(1, -1).astype(F32)

    w_in_p, w_uq_p, wuk_bd, place, w_uv_p = _prep_mla(w_in, mla_w_uq, mla_w_uk, mla_w_uv)
    lw = {
        "norm1_g": rows(norm1_g), "w_in": w_in_p, "q_norm_g": rows(mla_q_norm_g), "w_uq": w_uq_p,
        "w_uk": wuk_bd, "place": place, "kv_norm_g": rows(mla_kv_norm_g), "w_uv": w_uv_p,
        "s5_d": rows(s5_d), "s5_w_glu": s5_w_glu.astype(BF16), "s5_b_glu": rows(s5_b_glu),
        "out_norm_g": rows(out_norm_g), "w_out": w_out.astype(BF16), "norm2_g": rows(norm2_g),
        "w_up": w_up.astype(BF16), "w_down": w_down.astype(BF16),
    }
    sw = jax.vmap(_prep_s5)(s5_lambda_re, s5_lambda_im, s5_log_dt, s5_b_re, s5_b_im, s5_c_re, s5_c_im)
    sw["rep2"], sw["rep3"] = _lane_repeaters(S5_STATE), _lane_repeaters(S5_GROUP)
    lb = rows(lb_all)
    s5_h0 = jnp.concatenate([state_s5_re.reshape(depth, bs, -1), state_s5_im.reshape(depth, bs, -1)], axis=-1)
    hg_zero = jnp.zeros((1, HG_WIDTH, HG_WIDTH), F32)

    xp = x_prompt.reshape(bp * lp, D_MODEL)
    xs = x_sample.reshape(bs * ls, D_MODEL)
    outs_p, outs_s = [], []
    for l in range(depth):
        final = l == depth - 1
        xp, a, b, c, d = _layer(xp, lw, sw, l, cos_p, sin_p, lb, hgc_p, ones_bd, final_g, final, prompt=True,
                                n_seq=1, s5_h0=None, hg_s0=hg_zero, kv_past=None, pe_past=None)
        outs_p.append((a, b, c, d))
        xs, a, b, c, d = _layer(xs, lw, sw, l, cos_s, sin_s, lb, hgc_s, ones_bd, final_g, final, prompt=False,
                                n_seq=bs, s5_h0=s5_h0, hg_s0=state_hgrn.astype(F32), kv_past=cache_mla_kv, pe_past=cache_mla_pe)
        outs_s.append((a, b, c, d))

    def gather(outs, nb, sl):
        kv = jnp.stack([o[0].reshape(nb, sl, MLA_KV_RANK) for o in outs])
        pe = jnp.stack([o[1].reshape(nb, sl, MLA_ROPE) for o in outs])
        hg = jnp.stack([o[2] if o[2].ndim == 4 else _state_from_bd(o[2]) for o in outs])
        half = S5_FLAT // 2
        re = jnp.stack([o[3][:, :half].reshape(nb, S5_GROUPS, S5_STATE) for o in outs])
        im = jnp.stack([o[3][:, half:].reshape(nb, S5_GROUPS, S5_STATE) for o in outs])
        return kv, pe, hg, re, im

    p_kv, p_pe, p_hg, p_re, p_im = gather(outs_p, bp, lp)
    s_kv, s_pe, s_hg, s_re, s_im = gather(outs_s, bs, ls)
    return (xp.reshape(bp, lp, D_MODEL), xs.reshape(bs, ls, D_MODEL),
            p_kv, p_pe, p_hg, p_re, p_im, s_kv, s_pe, s_hg, s_re, s_im)
```
